```python
import math
import jax
import jax.numpy as jnp
from jax import lax
import numpy as np

D_MODEL = 1024
BATCH = 2
SEQ = 8192
DEPTH = 2
DEC_BATCH = 128
DEC_SEQ = 4
PAST_LEN = 2048
PAGE_SIZE = 128

HEAD_DIM = 64
N_HEADS = D_MODEL // (2 * HEAD_DIM)
N_KV_HEADS = N_HEADS // 2
GQA_GROUP = N_HEADS // N_KV_HEADS
V_DIM = 2 * HEAD_DIM
ATTN_WIDTH = N_HEADS * V_DIM
ATTN_SCALE = HEAD_DIM ** -0.5
Q_BLOCK = 128
MASK_VALUE = -1e30
NUM_BUCKETS = 32
MAX_EXACT = NUM_BUCKETS // 2
MAX_DISTANCE = 128
D_RNN = D_MODEL
N_RNN_BLOCKS = 16
RNN_BLOCK = D_RNN // N_RNN_BLOCKS
CONV_W = 4
LRU_C = 8.0
N_GROUPS = 4
EXPERTS_PER_GROUP = 8
N_EXPERTS = N_GROUPS * EXPERTS_PER_GROUP
TOP_K = 2
D_EXPERT = D_MODEL // 2
NORM_EPS = 1e-6
Q_COLS = N_HEADS * 2 * HEAD_DIM
K_COLS = N_KV_HEADS * 2 * HEAD_DIM
V_COLS = N_KV_HEADS * V_DIM
D_IN = Q_COLS + K_COLS + V_COLS + 2 * D_RNN + 2 * D_MODEL
IN_SPLITS = (Q_COLS, Q_COLS + K_COLS, Q_COLS + K_COLS + V_COLS,
             Q_COLS + K_COLS + V_COLS + D_RNN, Q_COLS + K_COLS + V_COLS + 2 * D_RNN,
             Q_COLS + K_COLS + V_COLS + 2 * D_RNN + D_MODEL)

kernel_name = 'hybrid_diffattn_rglru_hmoe_decode_step'


def rms_norm(x, g):
    xf = x.astype(jnp.float32)
    y = xf * lax.rsqrt(jnp.mean(xf * xf, axis=-1, keepdims=True) + NORM_EPS)
    return (y * g.astype(jnp.float32)).astype(x.dtype)


def rel_bucket(q_pos, k_pos):
    n = jnp.maximum(q_pos[:, None] - k_pos[None, :], 0)
    nf = jnp.maximum(n, 1).astype(jnp.float32)
    large = MAX_EXACT + (jnp.log(nf / MAX_EXACT) / math.log(MAX_DISTANCE / MAX_EXACT)
                         * (NUM_BUCKETS - MAX_EXACT)).astype(jnp.int32)
    return jnp.where(n < MAX_EXACT, n, jnp.minimum(large, NUM_BUCKETS - 1))


def diff_attend(q, k, v, q_pos, k_pos, rel_bias, lam, subln_g, lam_init):
    tq, tk = q_pos.shape[0], k_pos.shape[0]
    s = jnp.einsum('bqhgcd,bkhcd->bhgcqk', q.astype(jnp.float32), k.astype(jnp.float32))
    bias = rel_bias.astype(jnp.float32)[rel_bucket(q_pos, k_pos)]
    bias = jnp.transpose(bias, (2, 0, 1)).reshape(N_KV_HEADS, GQA_GROUP, 1, tq, tk)
    causal = k_pos[None, :] <= q_pos[:, None]
    p = jax.nn.softmax(jnp.where(causal, s + bias, MASK_VALUE), axis=-1)
    a = p[:, :, :, 0] - lam * p[:, :, :, 1]
    o = jnp.einsum('bhgqk,bkhe->bqhge', a, v.astype(jnp.float32))
    o = o * lax.rsqrt(jnp.mean(o * o, axis=-1, keepdims=True) + NORM_EPS)
    o = o * subln_g.astype(jnp.float32) * (1.0 - lam_init)
    return o.reshape(o.shape[0], tq, ATTN_WIDTH).astype(q.dtype)


def attention(q, k, v, q_pos, k_pos, rel_bias, lam, subln_g, lam_init, blocked):
    if blocked:
        b, t = q.shape[0], q.shape[1]
        nb = t // Q_BLOCK
        qb = jnp.swapaxes(q.reshape((b, nb, Q_BLOCK) + q.shape[2:]), 0, 1)
        pb = q_pos.reshape(nb, Q_BLOCK)
        ob = lax.map(lambda xs: diff_attend(xs[0], k, v, xs[1], k_pos, rel_bias, lam, subln_g, lam_init),
                     (qb, pb))
        return jnp.swapaxes(ob, 0, 1).reshape(b, t, ATTN_WIDTH)
    return diff_attend(q, k, v, q_pos, k_pos, rel_bias, lam, subln_g, lam_init)


def _lru_combine(c1, c2):
    a1, b1 = c1
    a2, b2 = c2
    return a2 * a1, a2 * b1 + b2


def rglru(xc, h0, pos, lru_wa, lru_ba, lru_wi, lru_bi, lru_lambda):
    b, t, _ = xc.shape
    xf = xc.astype(jnp.float32)
    xb = xf.reshape(b, t, N_RNN_BLOCKS, RNN_BLOCK)
    r = jax.nn.sigmoid(jnp.einsum('btnc,ncd->btnd', xb, lru_wa.astype(jnp.float32))
                       + lru_ba.astype(jnp.float32)).reshape(b, t, D_RNN)
    i = jax.nn.sigmoid(jnp.einsum('btnc,ncd->btnd', xb, lru_wi.astype(jnp.float32))
                       + lru_bi.astype(jnp.float32)).reshape(b, t, D_RNN)
    log_a = -LRU_C * r * jax.nn.softplus(-lru_lambda.astype(jnp.float32))
    a = jnp.exp(log_a)
    mult = jnp.sqrt(-jnp.expm1(2.0 * log_a))
    mult = jnp.where((pos == 0)[None, :, None], 1.0, mult)
    a_cum, b_cum = lax.associative_scan(_lru_combine, (a, mult * i * xf), axis=1)
    h = a_cum * h0.astype(jnp.float32)[:, None, :] + b_cum
    return h.astype(xc.dtype), h[:, -1].astype(h0.dtype)


def hmoe(u, w_rg, b_rg, w_re, b_re, w_e_gate, w_e_up, w_e_down):
    t = u.reshape(-1, D_MODEL)
    gp = jax.nn.softmax((t @ w_rg + b_rg).astype(jnp.float32), axis=-1)
    gi = jnp.argmax(gp, axis=-1)
    gw = jnp.max(gp, axis=-1)
    el = (t @ w_re + b_re).astype(jnp.float32).reshape(-1, N_GROUPS, EXPERTS_PER_GROUP)
    el = jnp.take_along_axis(el, gi[:, None, None], axis=1)[:, 0]
    ew, ei = lax.top_k(jax.nn.softmax(el, axis=-1), TOP_K)
    ew = ew / jnp.sum(ew, axis=-1, keepdims=True)
    within = jnp.sum(ew[..., None] * jax.nn.one_hot(ei, EXPERTS_PER_GROUP, dtype=jnp.float32), axis=1)
    comb = (gw[:, None, None] * jax.nn.one_hot(gi, N_GROUPS, dtype=jnp.float32)[:, :, None]
            * within[:, None, :]).reshape(-1, N_EXPERTS).astype(u.dtype)
    out = jnp.zeros_like(t)
    for e in range(N_EXPERTS):
        h = jax.nn.silu(t @ w_e_gate[e]) * (t @ w_e_up[e])
        out = out + comb[:, e:e + 1] * (h @ w_e_down[e])
    return out.reshape(u.shape)


def mixer(u, pos, k_past, v_past, h0, conv_buf, lp, lam_init, rel_bias, blocked):
    b, t, _ = u.shape
    z = u @ lp['w_in']
    q, k, v, xr, yr, ga, gr = jnp.split(z, IN_SPLITS, axis=-1)
    q = rms_norm(q.reshape(b, t, N_KV_HEADS, GQA_GROUP, 2, HEAD_DIM), lp['q_norm_g']) * ATTN_SCALE
    k = rms_norm(k.reshape(b, t, N_KV_HEADS, 2, HEAD_DIM), lp['k_norm_g'])
    v = v.reshape(b, t, N_KV_HEADS, V_DIM)
    lam = (jnp.exp(jnp.sum(lp['lam_q1'].astype(jnp.float32) * lp['lam_k1'].astype(jnp.float32)))
           - jnp.exp(jnp.sum(lp['lam_q2'].astype(jnp.float32) * lp['lam_k2'].astype(jnp.float32)))
           + lam_init)
    if k_past is None:
        k_all, v_all, k_pos = k, v, pos
    else:
        k_all = jnp.concatenate([k_past.astype(k.dtype), k], axis=1)
        v_all = jnp.concatenate([v_past.astype(v.dtype), v], axis=1)
        k_pos = jnp.concatenate([jnp.arange(k_past.shape[1], dtype=jnp.int32), pos])
    attn = attention(q, k_all, v_all, pos, k_pos, rel_bias, lam, lp['subln_g'], lam_init, blocked)
    xin = jnp.concatenate([conv_buf.astype(xr.dtype), xr], axis=1)
    xc = lp['conv_b']
    for j in range(CONV_W):
        xc = xc + xin[:, j:j + t] * lp['conv_w'][j]
    h, h_last = rglru(xc, h0, pos, lp['lru_wa'], lp['lru_ba'], lp['lru_wi'], lp['lru_bi'], lp['lru_lambda'])
    rnn = h * jax.nn.gelu(yr)
    merged = jax.nn.sigmoid(ga) * (attn @ lp['w_pa']) + jax.nn.sigmoid(gr) * (rnn @ lp['w_pr'])
    return merged @ lp['w_o'], k, v, h_last, xin[:, -(CONV_W - 1):]


def layer(x, c, pos, k_past, v_past, h0, conv_buf, lp, lam_init, rel_bias, blocked):
    mods = jax.nn.silu(c) @ lp['w_mod'] + lp['b_mod']
    sh1, sc1, g1, sh2, sc2, g2 = jnp.split(mods[:, None, :], 6, axis=-1)
    u = rms_norm(x, lp['norm1_g']) * (1.0 + sc1) + sh1
    mix, k_new, v_new, h_last, buf_new = mixer(u, pos, k_past, v_past, h0, conv_buf, lp, lam_init,
                                               rel_bias, blocked)
    x = x + g1 * mix
    u2 = rms_norm(x, lp['norm2_g']) * (1.0 + sc2) + sh2
    x = x + g2 * hmoe(u2, lp['w_rg'], lp['b_rg'], lp['w_re'], lp['b_re'],
                      lp['w_e_gate'], lp['w_e_up'], lp['w_e_down'])
    return x, k_new, v_new, h_last, buf_new


def setup_inputs(seed: int = 0) -> dict:
    key = jax.random.key(seed)
    ks = iter(jax.random.split(key, 48))
    f32 = jnp.float32

    def nrm(shape, scale):
        return jax.random.normal(next(ks), shape, f32) * scale

    n_pages = PAST_LEN // PAGE_SIZE
    n_used = DEC_BATCH * n_pages
    n_pool = n_used + max(1, n_used // 4)
    page_table = jax.random.permutation(next(ks), n_pool)[:n_used].reshape(DEC_BATCH, n_pages).astype(jnp.int32)
    ua = jax.random.uniform(next(ks), (DEPTH, D_RNN), f32, 0.9, 0.999)
    sa = ua ** (1.0 / LRU_C)
    lru_lambda = jnp.log(sa) - jnp.log1p(-sa)
    dsc = D_MODEL ** -0.5
    return {
        'x_prompt': nrm((BATCH, SEQ, D_MODEL), 1.0),
        'x_sample': nrm((DEC_BATCH, DEC_SEQ, D_MODEL), 1.0),
        'cache_k': nrm((DEPTH, n_pool, PAGE_SIZE, N_KV_HEADS, 2 * HEAD_DIM), 1.0),
        'cache_v': nrm((DEPTH, n_pool, PAGE_SIZE, N_KV_HEADS, V_DIM), 1.0),
        'state_rnn': nrm((DEPTH, DEC_BATCH, D_RNN), 0.5),
        'state_conv': nrm((DEPTH, DEC_BATCH, CONV_W - 1, D_RNN), 1.0),
        'page_table': page_table,
        'c_prompt': nrm((BATCH, D_MODEL), 1.0),
        'c_sample': nrm((DEC_BATCH, D_MODEL), 1.0),
        'rel_bias': nrm((NUM_BUCKETS, N_HEADS), 0.5),
        'w_mod': nrm((DEPTH, D_MODEL, 6 * D_MODEL), 0.5 * dsc),
        'b_mod': nrm((DEPTH, 6 * D_MODEL), 0.02),
        'norm1_g': 1.0 + nrm((DEPTH, D_MODEL), 0.05),
        'norm2_g': 1.0 + nrm((DEPTH, D_MODEL), 0.05),
        'w_in': nrm((DEPTH, D_MODEL, D_IN), dsc),
        'q_norm_g': 1.0 + nrm((DEPTH, HEAD_DIM), 0.05),
        'k_norm_g': 1.0 + nrm((DEPTH, HEAD_DIM), 0.05),
        'lam_q1': nrm((DEPTH, HEAD_DIM), 0.1),
        'lam_k1': nrm((DEPTH, HEAD_DIM), 0.1),
        'lam_q2': nrm((DEPTH, HEAD_DIM), 0.1),
        'lam_k2': nrm((DEPTH, HEAD_DIM), 0.1),
        'subln_g': 1.0 + nrm((DEPTH, V_DIM), 0.05),
        'conv_w': nrm((DEPTH, CONV_W, D_RNN), CONV_W ** -0.5),
        'conv_b': nrm((DEPTH, D_RNN), 0.02),
        'lru_wa': nrm((DEPTH, N_RNN_BLOCKS, RNN_BLOCK, RNN_BLOCK), RNN_BLOCK ** -0.5),
        'lru_ba': nrm((DEPTH, N_RNN_BLOCKS, RNN_BLOCK), 0.02),
        'lru_wi': nrm((DEPTH, N_RNN_BLOCKS, RNN_BLOCK, RNN_BLOCK), RNN_BLOCK ** -0.5),
        'lru_bi': nrm((DEPTH, N_RNN_BLOCKS, RNN_BLOCK), 0.02),
        'lru_lambda': lru_lambda,
        'w_pa': nrm((DEPTH, ATTN_WIDTH, D_MODEL), ATTN_WIDTH ** -0.5),
        'w_pr': nrm((DEPTH, D_RNN, D_MODEL), D_RNN ** -0.5),
        'w_o': nrm((DEPTH, D_MODEL, D_MODEL), dsc),
        'w_rg': nrm((DEPTH, D_MODEL, N_GROUPS), dsc),
        'b_rg': nrm((DEPTH, N_GROUPS), 0.01),
        'w_re': nrm((DEPTH, D_MODEL, N_EXPERTS), dsc),
        'b_re': nrm((DEPTH, N_EXPERTS), 0.01),
        'w_e_gate': nrm((DEPTH, N_EXPERTS, D_MODEL, D_EXPERT), dsc),
        'w_e_up': nrm((DEPTH, N_EXPERTS, D_MODEL, D_EXPERT), dsc),
        'w_e_down': nrm((DEPTH, N_EXPERTS, D_EXPERT, D_MODEL), D_EXPERT ** -0.5),
    }


def reference(x_prompt, x_sample, cache_k, cache_v, state_rnn, state_conv, page_table, c_prompt, c_sample,
              rel_bias, w_mod, b_mod, norm1_g, norm2_g, w_in, q_norm_g, k_norm_g, lam_q1, lam_k1, lam_q2,
              lam_k2, subln_g, conv_w, conv_b, lru_wa, lru_ba, lru_wi, lru_bi, lru_lambda, w_pa, w_pr, w_o,
              w_rg, b_rg, w_re, b_re, w_e_gate, w_e_up, w_e_down):
    bp, tp = x_prompt.shape[0], x_prompt.shape[1]
    bs, ts = x_sample.shape[0], x_sample.shape[1]
    past_len = page_table.shape[1] * PAGE_SIZE
    pos_p = jnp.arange(tp, dtype=jnp.int32)
    pos_s = past_len + jnp.arange(ts, dtype=jnp.int32)
    xp, xs = x_prompt, x_sample
    kp_l, vp_l, hp_l, cp_l = [], [], [], []
    ks_l, vs_l, hs_l, cs_l = [], [], [], []
    for l in range(DEPTH):
        lp = {'w_mod': w_mod[l], 'b_mod': b_mod[l], 'norm1_g': norm1_g[l], 'norm2_g': norm2_g[l],
              'w_in': w_in[l], 'q_norm_g': q_norm_g[l], 'k_norm_g': k_norm_g[l],
              'lam_q1': lam_q1[l], 'lam_k1': lam_k1[l], 'lam_q2': lam_q2[l], 'lam_k2': lam_k2[l],
              'subln_g': subln_g[l], 'conv_w': conv_w[l], 'conv_b': conv_b[l],
              'lru_wa': lru_wa[l], 'lru_ba': lru_ba[l], 'lru_wi': lru_wi[l], 'lru_bi': lru_bi[l],
              'lru_lambda': lru_lambda[l], 'w_pa': w_pa[l], 'w_pr': w_pr[l], 'w_o': w_o[l],
              'w_rg': w_rg[l], 'b_rg': b_rg[l], 'w_re': w_re[l], 'b_re': b_re[l],
              'w_e_gate': w_e_gate[l], 'w_e_up': w_e_up[l], 'w_e_down': w_e_down[l]}
        lam_init = 0.8 - 0.6 * math.exp(-0.3 * l)
        h0 = jnp.zeros((bp, D_RNN), state_rnn.dtype)
        buf0 = jnp.zeros((bp, CONV_W - 1, D_RNN), xp.dtype)
        xp, k_new, v_new, h_last, buf_new = layer(xp, c_prompt, pos_p, None, None, h0, buf0, lp, lam_init,
                                                  rel_bias, True)
        kp_l.append(k_new.reshape(bp, tp, N_KV_HEADS, 2 * HEAD_DIM))
        vp_l.append(v_new)
        hp_l.append(h_last)
        cp_l.append(buf_new)
        k_past = cache_k[l, page_table].reshape(bs, past_len, N_KV_HEADS, 2, HEAD_DIM)
        v_past = cache_v[l, page_table].reshape(bs, past_len, N_KV_HEADS, V_DIM)
        xs, k_new, v_new, h_last, buf_new = layer(xs, c_sample, pos_s, k_past, v_past, state_rnn[l],
                                                  state_conv[l], lp, lam_init, rel_bias, False)
        ks_l.append(k_new.reshape(bs, ts, N_KV_HEADS, 2 * HEAD_DIM))
        vs_l.append(v_new)
        hs_l.append(h_last)
        cs_l.append(buf_new)
    return (xp, xs, jnp.stack(kp_l), jnp.stack(vp_l), jnp.stack(hp_l), jnp.stack(cp_l),
            jnp.stack(ks_l), jnp.stack(vs_l), jnp.stack(hs_l), jnp.stack(cs_l))
```

```python
import functools
import math

import jax
import jax.numpy as jnp
from jax import lax
from jax.experimental import pallas as pl
from jax.experimental.pallas import tpu as pltpu

F32 = jnp.float32
BF16 = jnp.bfloat16

D_MODEL = 1024
HEAD_DIM = 64
N_HEADS = 8
N_KV_HEADS = 4
GQA_GROUP = 2
V_DIM = 128
ATTN_SCALE = HEAD_DIM ** -0.5
MASK_VALUE = -1e30
NUM_BUCKETS = 32
MAX_EXACT = 16
MAX_DISTANCE = 128
N_RNN_BLOCKS = 16
RNN_BLOCK = 64
CONV_W = 4
LRU_C = 8.0
N_GROUPS = 4
EXPERTS_PER_GROUP = 8
N_EXPERTS = 32
D_EXPERT = 512
NORM_EPS = 1e-6
PAGE_SIZE = 128
Q_COLS = 1024
K_COLS = 512
V_COLS = 512
D_IN = 6144

LANES = 128
SUBLANES = 8
MXU_DIM = 256
VMEM_LIMIT_BYTES = 56 * 1024 * 1024

TOKEN_TILE = 256
ATTN_BLOCK = 512
TIME_TILE = 256
EXPERT_TILE = 256
ROUTER_LANES = 128
ONES_ROWS = 16

_NT = (((1,), (1,)), ((), ()))


def _cparams(sem):
    return pltpu.CompilerParams(dimension_semantics=sem, vmem_limit_bytes=VMEM_LIMIT_BYTES)


def _dot(a, b):
    return jnp.dot(a, b, preferred_element_type=F32)


def _resident(shape):
    n = len(shape)
    return pl.BlockSpec(shape, lambda *_: (0,) * n)


_MODS_COLS = 1536


def _mods_body(c_ref, w_ref, b_ref, o_ref):
    c = c_ref[...]
    s = c * jax.nn.sigmoid(c)
    o_ref[...] = jnp.dot(s, w_ref[...], precision=lax.Precision.HIGHEST,
                         preferred_element_type=F32) + b_ref[...]


def _mods(c_all, w_mod, b_mod):
    rows = c_all.shape[0]
    depth, _, n = w_mod.shape
    return pl.pallas_call(
        _mods_body,
        grid=(depth, n // _MODS_COLS),
        in_specs=[pl.BlockSpec((rows, D_MODEL), lambda l, j: (0, 0)),
                  pl.BlockSpec((None, D_MODEL, _MODS_COLS), lambda l, j: (l, 0, j)),
                  pl.BlockSpec((None, 1, _MODS_COLS), lambda l, j: (l, 0, j))],
        out_specs=pl.BlockSpec((None, rows, _MODS_COLS), lambda l, j: (l, 0, j)),
        out_shape=jax.ShapeDtypeStruct((depth, rows, n), F32),
        compiler_params=_cparams(("arbitrary", "arbitrary")),
        name="mods",
    )(c_all, w_mod, b_mod.reshape(depth, 1, n))


def _rms_rows(x, g):
    ms = jnp.mean(x * x, axis=-1, keepdims=True)
    return x * lax.rsqrt(ms + NORM_EPS) * g


def _chunk_mean_square(z, p):
    parts = []
    for j in range(z.shape[1] // MXU_DIM):
        zz = z[:, j * MXU_DIM:(j + 1) * MXU_DIM]
        parts.append(_dot((zz * zz).astype(BF16), p))
    return jnp.concatenate(parts, axis=1)


def _in_proj_body(x_ref, sc_ref, sh_ref, g_ref, w_ref, qg_ref, kg_ref, p_ref,
                  q_ref, k_ref, kb_ref, v_ref, xr_ref, yr_ref, ga_ref, gr_ref):
    u = (_rms_rows(x_ref[...], g_ref[...]) * (1.0 + sc_ref[...]) + sh_ref[...]).astype(BF16)
    p = p_ref[...]
    zq = _dot(u, w_ref[:, 0:Q_COLS])
    q = zq * lax.rsqrt(_chunk_mean_square(zq, p) + NORM_EPS) * qg_ref[...]
    q_ref[...] = (q * ATTN_SCALE).astype(BF16)
    zk = _dot(u, w_ref[:, Q_COLS:Q_COLS + K_COLS])
    k = zk * lax.rsqrt(_chunk_mean_square(zk, p) + NORM_EPS) * kg_ref[...]
    k_ref[...] = k
    kb_ref[...] = k.astype(BF16)
    c0 = Q_COLS + K_COLS
    v_ref[...] = _dot(u, w_ref[:, c0:c0 + V_COLS])
    c0 += V_COLS
    xr_ref[...] = _dot(u, w_ref[:, c0:c0 + D_MODEL])
    c0 += D_MODEL
    yr_ref[...] = _dot(u, w_ref[:, c0:c0 + D_MODEL])
    c0 += D_MODEL
    ga_ref[...] = _dot(u, w_ref[:, c0:c0 + D_MODEL])
    c0 += D_MODEL
    gr_ref[...] = _dot(u, w_ref[:, c0:c0 + D_MODEL])


def _mod_spec(mods, tile, rows_per_group, col_block):
    if mods.ndim == 3:
        tiles_per_group = rows_per_group // tile
        return pl.BlockSpec((None, 1, D_MODEL), lambda i: (i // tiles_per_group, 0, col_block))
    return pl.BlockSpec((tile, D_MODEL), lambda i: (i, col_block))


def _in_proj(x, mods, rows_per_group, norm_g, w_in_bf, qg_row, kg_row, chunk_p):
    t = x.shape[0]
    tile = min(TOKEN_TILE, t)
    row = lambda n: pl.BlockSpec((tile, n), lambda i: (i, 0))
    outs = [(Q_COLS, BF16), (K_COLS, F32), (K_COLS, BF16), (V_COLS, F32),
            (D_MODEL, F32), (D_MODEL, F32), (D_MODEL, F32), (D_MODEL, F32)]
    return pl.pallas_call(
        _in_proj_body,
        grid=(t // tile,),
        in_specs=[row(D_MODEL),
                  _mod_spec(mods, tile, rows_per_group, 1),
                  _mod_spec(mods, tile, rows_per_group, 0),
                  _resident((1, D_MODEL)),
                  _resident((D_MODEL, D_IN)),
                  _resident((1, Q_COLS)),
                  _resident((1, K_COLS)),
                  _resident((MXU_DIM, MXU_DIM))],
        out_specs=[row(n) for n, _ in outs],
        out_shape=[jax.ShapeDtypeStruct((t, n), dt) for n, dt in outs],
        compiler_params=_cparams(("arbitrary",)),
        name="in_proj",
    )(x, mods, mods, norm_g, w_in_bf, qg_row, kg_row, chunk_p)


def _bucket_of_distance(n):
    n = jnp.maximum(n, 0)
    nf = jnp.maximum(n, 1).astype(F32)
    large = MAX_EXACT + (jnp.log(nf / MAX_EXACT) / math.log(MAX_DISTANCE / MAX_EXACT)
                         * (NUM_BUCKETS - MAX_EXACT)).astype(jnp.int32)
    return jnp.where(n < MAX_EXACT, n, jnp.minimum(large, NUM_BUCKETS - 1))


def _bias_of_distance(rel_bias, n):
    b = rel_bias.astype(F32)[_bucket_of_distance(n)]
    b = jnp.where((n >= 0)[..., None], b, MASK_VALUE)
    return jnp.moveaxis(b, -1, 0)


def _attn_prompt_body(scal_ref, q_ref, k_ref, vt_ref, bdiag_ref, bprev_ref, sg_ref,
                      o_ref, qpad_sc, m_sc, acc_sc):
    h = pl.program_id(1)
    qi = pl.program_id(2)
    blk = q_ref.shape[0]
    lam = scal_ref[0]
    out_scale = scal_ref[1]

    lane = lax.broadcasted_iota(jnp.int32, (blk, LANES), 1)
    for g in range(GQA_GROUP):
        q2 = q_ref[:, g * LANES:(g + 1) * LANES]
        zero = jnp.zeros_like(q2)
        qpad_sc[2 * g] = jnp.where(lane < HEAD_DIM, q2, zero)
        qpad_sc[2 * g + 1] = jnp.where(lane >= HEAD_DIM, q2, zero)
    m_sc[...] = jnp.full(m_sc.shape, -jnp.inf, F32)
    acc_sc[...] = jnp.zeros(acc_sc.shape, F32)

    def kstep(kj, bias_of_group):
        k_blk = k_ref[pl.ds(pl.multiple_of(kj * blk, blk), blk), :]
        vt = vt_ref[kj]
        for g in range(GQA_GROUP):
            bias = bias_of_group(g)
            for c in range(2):
                idx = 2 * g + c
                st = lax.dot_general(k_blk, qpad_sc[idx], _NT, preferred_element_type=F32) + bias
                m_old = m_sc[idx]
                m_new = jnp.maximum(m_old, jnp.max(st, axis=0, keepdims=True))
                p = jnp.exp(st - m_new).astype(BF16)
                acc_sc[idx] = jnp.exp(m_old - m_new) * acc_sc[idx] + _dot(vt, p)
                m_sc[idx] = m_new

    def far_step(kj, carry):
        kstep(kj, lambda g: scal_ref[2 + h * GQA_GROUP + g])
        return carry

    lax.fori_loop(0, jnp.maximum(qi - 1, 0), far_step, 0)

    @pl.when(qi >= 1)
    def _():
        kstep(qi - 1, lambda g: bprev_ref[g])

    kstep(qi, lambda g: bdiag_ref[g])

    for g in range(GQA_GROUP):
        a0 = acc_sc[2 * g]
        a1 = acc_sc[2 * g + 1]
        o = a0[:V_DIM] / a0[V_DIM:V_DIM + 1] - lam * (a1[:V_DIM] / a1[V_DIM:V_DIM + 1])
        o = o * lax.rsqrt(jnp.mean(o * o, axis=0, keepdims=True) + NORM_EPS)
        o = o * sg_ref[...] * out_scale
        o_ref[:, g * V_DIM:(g + 1) * V_DIM] = o.T.astype(BF16)


def _attn_prompt(scal, q_bf, k_bf, vt_ext, bias_diag, bias_prev, sg_col, batch, seq):
    blk = ATTN_BLOCK
    nblk = seq // blk
    vrows = V_DIM + ONES_ROWS
    grid_spec = pltpu.PrefetchScalarGridSpec(
        num_scalar_prefetch=1,
        grid=(batch, N_KV_HEADS, nblk),
        in_specs=[pl.BlockSpec((blk, GQA_GROUP * 2 * HEAD_DIM), lambda b, h, i, s: (b * nblk + i, h)),
                  pl.BlockSpec((seq, 2 * HEAD_DIM), lambda b, h, i, s: (b, h)),
                  pl.BlockSpec((None, nblk, vrows, blk), lambda b, h, i, s: (b * N_KV_HEADS + h, 0, 0, 0)),
                  pl.BlockSpec((GQA_GROUP, blk, blk), lambda b, h, i, s: (h, 0, 0)),
                  pl.BlockSpec((GQA_GROUP, blk, blk), lambda b, h, i, s: (h, 0, 0)),
                  pl.BlockSpec((V_DIM, 1), lambda b, h, i, s: (0, 0))],
        out_specs=pl.BlockSpec((blk, GQA_GROUP * V_DIM), lambda b, h, i, s: (b * nblk + i, h)),
        scratch_shapes=[pltpu.VMEM((2 * GQA_GROUP, blk, LANES), BF16),
                        pltpu.VMEM((2 * GQA_GROUP, 1, blk), F32),
                        pltpu.VMEM((2 * GQA_GROUP, vrows, blk), F32)],
    )
    return pl.pallas_call(
        _attn_prompt_body,
        grid_spec=grid_spec,
        out_shape=jax.ShapeDtypeStruct((batch * seq, N_HEADS * V_DIM), BF16),
        compiler_params=_cparams(("arbitrary", "arbitrary", "arbitrary")),
        name="attn_prompt",
    )(scal, q_bf, k_bf, vt_ext, bias_diag, bias_prev, sg_col)


def _attn_sample_body(n_pages, pt_ref, layer_ref, scal_ref, qbd_ref, knew_ref, vnew_ref, bias_ref, sg_ref,
                      *refs):
    k_pages = refs[:n_pages]
    v_pages = refs[n_pages:2 * n_pages]
    o_ref, kb_sc, vb_sc = refs[2 * n_pages:]
    past = n_pages * PAGE_SIZE
    n_new = knew_ref.shape[2]
    lam = scal_ref[0]
    out_scale = scal_ref[1]
    half = qbd_ref.shape[2] // 2

    @pl.when(pl.program_id(0) == 0)
    def _():
        tail = kb_sc.shape[1] - past - n_new
        kb_sc[:, past + n_new:, :] = jnp.zeros((N_KV_HEADS, tail, LANES), BF16)
        vb_sc[:, past + n_new:, :] = jnp.zeros((N_KV_HEADS, tail, LANES), BF16)

    for h in range(N_KV_HEADS):
        for j in range(n_pages):
            rows = slice(j * PAGE_SIZE, (j + 1) * PAGE_SIZE)
            kb_sc[h, rows, :] = k_pages[j][0, 0, :, h, :].astype(BF16)
            vb_sc[h, rows, :] = v_pages[j][0, 0, :, h, :].astype(BF16)
        kb_sc[h, past:past + n_new, :] = knew_ref[0, h]
        vb_sc[h, past:past + n_new, :] = vnew_ref[0, h]
        s = lax.dot_general(qbd_ref[0, h], kb_sc[h], _NT, preferred_element_type=F32) + bias_ref[h]
        m = jnp.max(s, axis=-1, keepdims=True)
        p = jnp.exp(s - m)
        p = p / jnp.sum(p, axis=-1, keepdims=True)
        a = p[:half] - lam * p[half:]
        o = _dot(a.astype(BF16), vb_sc[h])
        o = o * lax.rsqrt(jnp.mean(o * o, axis=-1, keepdims=True) + NORM_EPS)
        o_ref[0, h] = o * sg_ref[...] * out_scale


def _attn_sample(page_table_flat, layer_idx, scal, qbd, knew, vnew, bias_s, sg_row, cache_k, cache_v, n_pages):
    nb = qbd.shape[0]
    rows = qbd.shape[2]
    n_new = knew.shape[2]
    keys = n_pages * PAGE_SIZE + LANES

    def page_spec(j):
        return pl.BlockSpec((1, 1, PAGE_SIZE, N_KV_HEADS, 2 * HEAD_DIM),
                            lambda b, pt, l: (l[0], pt[b * n_pages + j], 0, 0, 0))

    grid_spec = pltpu.PrefetchScalarGridSpec(
        num_scalar_prefetch=2,
        grid=(nb,),
        in_specs=[pl.BlockSpec(memory_space=pltpu.SMEM),
                  pl.BlockSpec((1, N_KV_HEADS, rows, LANES), lambda b, pt, l: (b, 0, 0, 0)),
                  pl.BlockSpec((1, N_KV_HEADS, n_new, LANES), lambda b, pt, l: (b, 0, 0, 0)),
                  pl.BlockSpec((1, N_KV_HEADS, n_new, LANES), lambda b, pt, l: (b, 0, 0, 0)),
                  pl.BlockSpec((N_KV_HEADS, rows, keys), lambda b, pt, l: (0, 0, 0)),
                  pl.BlockSpec((1, V_DIM), lambda b, pt, l: (0, 0))]
                 + [page_spec(j) for j in range(n_pages)] * 2,
        out_specs=pl.BlockSpec((1, N_KV_HEADS, rows // 2, V_DIM), lambda b, pt, l: (b, 0, 0, 0)),
        scratch_shapes=[pltpu.VMEM((N_KV_HEADS, keys, LANES), BF16),
                        pltpu.VMEM((N_KV_HEADS, keys, LANES), BF16)],
    )
    return pl.pallas_call(
        functools.partial(_attn_sample_body, n_pages),
        grid_spec=grid_spec,
        out_shape=jax.ShapeDtypeStruct((nb, N_KV_HEADS, rows // 2, V_DIM), F32),
        compiler_params=_cparams(("arbitrary",)),
        name="attn_sample",
    )(page_table_flat, layer_idx, scal, qbd, knew, vnew, bias_s, sg_row,
      *([cache_k] * n_pages), *([cache_v] * n_pages))


def _lru_coeffs(xc, wa_ref, wi_ref, ba_ref, bi_ref, sp_ref):
    xb = xc.astype(BF16)
    ra, ri = [], []
    for j in range(xc.shape[1] // MXU_DIM):
        cols = slice(j * MXU_DIM, (j + 1) * MXU_DIM)
        ra.append(_dot(xb[:, cols], wa_ref[j]))
        ri.append(_dot(xb[:, cols], wi_ref[j]))
    r = jax.nn.sigmoid(jnp.concatenate(ra, axis=1) + ba_ref[...])
    i = jax.nn.sigmoid(jnp.concatenate(ri, axis=1) + bi_ref[...])
    log_a = -LRU_C * r * sp_ref[...]
    a = jnp.exp(log_a)
    mult = jnp.sqrt(-jnp.tanh(log_a) * (a * a + 1.0))
    return a, mult, i


def _rglru_prompt_body(xr_ref, yr_ref, cw_ref, cb_ref, wa_ref, wi_ref, ba_ref, bi_ref, sp_ref,
                       rnn_ref, hl_ref, xe_sc, a_sc, b_sc, h_sc):
    i = pl.program_id(1)
    tt = xr_ref.shape[0]

    @pl.when(i == 0)
    def _():
        xe_sc[0:SUBLANES, :] = jnp.zeros((SUBLANES, D_MODEL), F32)
        h_sc[...] = jnp.zeros(h_sc.shape, F32)

    xe_sc[SUBLANES:SUBLANES + tt, :] = xr_ref[...]
    xc = cb_ref[...]
    for j in range(CONV_W):
        off = SUBLANES - (CONV_W - 1) + j
        xc = xc + xe_sc[off:off + tt, :] * cw_ref[j:j + 1, :]
    a, mult, gate_i = _lru_coeffs(xc, wa_ref, wi_ref, ba_ref, bi_ref, sp_ref)
    row = lax.broadcasted_iota(jnp.int32, (tt, 1), 0)
    mult = jnp.where((row == 0) & (i == 0), 1.0, mult)
    a_sc[...] = a
    b_sc[...] = mult * gate_i * xc

    def step(t, h):
        h = a_sc[pl.ds(t, 1), :] * h + b_sc[pl.ds(t, 1), :]
        b_sc[pl.ds(t, 1), :] = h
        return h

    h = lax.fori_loop(0, tt, step, h_sc[...], unroll=8)
    h_sc[...] = h
    hl_ref[...] = h
    rnn_ref[...] = (b_sc[...] * jax.nn.gelu(yr_ref[...])).astype(BF16)
    xe_sc[0:SUBLANES, :] = xe_sc[tt:tt + SUBLANES, :]


def _rglru_prompt(xr, yr, cw, cb, wa_bd, wi_bd, ba, bi, sp, batch, seq):
    tt = min(TIME_TILE, seq)
    nt = seq // tt
    row = pl.BlockSpec((tt, D_MODEL), lambda b, i: (b * nt + i, 0))
    nbd = D_MODEL // MXU_DIM
    return pl.pallas_call(
        _rglru_prompt_body,
        grid=(batch, nt),
        in_specs=[row, row,
                  pl.BlockSpec((CONV_W, D_MODEL), lambda b, i: (0, 0)),
                  pl.BlockSpec((1, D_MODEL), lambda b, i: (0, 0)),
                  pl.BlockSpec((nbd, MXU_DIM, MXU_DIM), lambda b, i: (0, 0, 0)),
                  pl.BlockSpec((nbd, MXU_DIM, MXU_DIM), lambda b, i: (0, 0, 0)),
                  pl.BlockSpec((1, D_MODEL), lambda b, i: (0, 0)),
                  pl.BlockSpec((1, D_MODEL), lambda b, i: (0, 0)),
                  pl.BlockSpec((1, D_MODEL), lambda b, i: (0, 0))],
        out_specs=[row, pl.BlockSpec((None, 1, D_MODEL), lambda b, i: (b, 0, 0))],
        out_shape=[jax.ShapeDtypeStruct((batch * seq, D_MODEL), BF16),
                   jax.ShapeDtypeStruct((batch, 1, D_MODEL), F32)],
        scratch_shapes=[pltpu.VMEM((tt + SUBLANES, D_MODEL), F32),
                        pltpu.VMEM((tt, D_MODEL), F32),
                        pltpu.VMEM((tt, D_MODEL), F32),
                        pltpu.VMEM((1, D_MODEL), F32)],
        compiler_params=_cparams(("arbitrary", "arbitrary")),
        name="rglru_prompt",
    )(xr, yr, cw, cb, wa_bd, wi_bd, ba, bi, sp)


def _rglru_sample_body(n_steps, first_is_start, xr_ref, yr_ref, buf_ref, h0_ref, cw_ref, cb_ref, wa_ref, wi_ref, ba_ref, bi_ref,
                       sp_ref, rnn_ref, hl_ref):
    nb = h0_ref.shape[0]
    taps = [buf_ref[j] for j in range(CONV_W - 1)] + [xr_ref[t * nb:(t + 1) * nb, :] for t in range(n_steps)]
    h = h0_ref[...]
    for t in range(n_steps):
        xc = cb_ref[...]
        for j in range(CONV_W):
            xc = xc + taps[t + j] * cw_ref[j:j + 1, :]
        a, mult, gate_i = _lru_coeffs(xc, wa_ref, wi_ref, ba_ref, bi_ref, sp_ref)
        if first_is_start and t == 0:
            mult = jnp.ones_like(mult)
        h = a * h + mult * gate_i * xc
        rnn_ref[t * nb:(t + 1) * nb, :] = (h * jax.nn.gelu(yr_ref[t * nb:(t + 1) * nb, :])).astype(BF16)
    hl_ref[...] = h


def _rglru_sample(xr, yr, buf, h0, cw, cb, wa_bd, wi_bd, ba, bi, sp, n_steps, first_is_start):
    t = xr.shape[0]
    return pl.pallas_call(
        functools.partial(_rglru_sample_body, n_steps, first_is_start),
        out_shape=[jax.ShapeDtypeStruct((t, D_MODEL), BF16),
                   jax.ShapeDtypeStruct(h0.shape, F32)],
        compiler_params=pltpu.CompilerParams(vmem_limit_bytes=VMEM_LIMIT_BYTES),
        name="rglru_sample",
    )(xr, yr, buf, h0, cw, cb, wa_bd, wi_bd, ba, bi, sp)


def _route(logits):
    lane = lax.broadcasted_iota(jnp.int32, logits.shape, 1).astype(F32)
    none = float(ROUTER_LANES)
    is_group = lane < N_GROUPS
    lg = jnp.where(is_group, logits, -jnp.inf)
    eg = jnp.exp(lg - jnp.max(lg, axis=-1, keepdims=True))
    gp = eg / jnp.sum(eg, axis=-1, keepdims=True)
    gw = jnp.max(gp, axis=-1, keepdims=True)
    gi = jnp.min(jnp.where(is_group & (gp == gw), lane, none), axis=-1, keepdims=True)
    first = N_GROUPS + gi * EXPERTS_PER_GROUP
    in_group = (lane >= first) & (lane < first + EXPERTS_PER_GROUP)
    le = jnp.where(in_group, logits, -jnp.inf)
    ee = jnp.exp(le - jnp.max(le, axis=-1, keepdims=True))
    pe = ee / jnp.sum(ee, axis=-1, keepdims=True)
    p1 = jnp.max(pe, axis=-1, keepdims=True)
    i1 = jnp.min(jnp.where(in_group & (pe == p1), lane, none), axis=-1, keepdims=True)
    rest = in_group & (lane != i1)
    pr = jnp.where(rest, pe, -1.0)
    p2 = jnp.max(pr, axis=-1, keepdims=True)
    i2 = jnp.min(jnp.where(rest & (pr == p2), lane, none), axis=-1, keepdims=True)
    den = p1 + p2
    w1 = gw * (p1 / den)
    w2 = gw * (p2 / den)
    rec = jnp.where(lane == 0.0, i1 - N_GROUPS, 0.0)
    rec = jnp.where(lane == 1.0, i2 - N_GROUPS, rec)
    rec = jnp.where(lane == 2.0, w1, rec)
    return jnp.where(lane == 3.0, w2, rec)


def _merge_body(x_ref, attn_ref, rnn_ref, ga_ref, gr_ref, g1_ref, sc2_ref, sh2_ref, n2g_ref,
                wpa_ref, wpr_ref, wo_ref, wrh_ref, wrl_ref, br_ref,
                x1_ref, u2_ref, rec_ref):
    pa = _dot(attn_ref[...], wpa_ref[...])
    pr = _dot(rnn_ref[...], wpr_ref[...])
    merged = jax.nn.sigmoid(ga_ref[...]) * pa + jax.nn.sigmoid(gr_ref[...]) * pr
    x1 = x_ref[...] + g1_ref[...] * _dot(merged.astype(BF16), wo_ref[...])
    x1_ref[...] = x1
    u2 = _rms_rows(x1, n2g_ref[...]) * (1.0 + sc2_ref[...]) + sh2_ref[...]
    u2_ref[...] = u2
    u_hi = u2.astype(BF16)
    u_lo = (u2 - u_hi.astype(F32)).astype(BF16)
    logits = (_dot(u_hi, wrh_ref[...]) + _dot(u_lo, wrh_ref[...]) + _dot(u_hi, wrl_ref[...])) + br_ref[...]
    rec_ref[...] = _route(logits)


def _merge(x, attn, rnn, ga, gr, mods, rows_per_group, n2g, wpa, wpr, wo, wr_hi, wr_lo, br):
    t = x.shape[0]
    tile = min(TOKEN_TILE, t)
    row = lambda n: pl.BlockSpec((tile, n), lambda i: (i, 0))
    sq = _resident((D_MODEL, D_MODEL))
    return pl.pallas_call(
        _merge_body,
        grid=(t // tile,),
        in_specs=[row(D_MODEL), row(D_MODEL), row(D_MODEL), row(D_MODEL), row(D_MODEL),
                  _mod_spec(mods, tile, rows_per_group, 2),
                  _mod_spec(mods, tile, rows_per_group, 4),
                  _mod_spec(mods, tile, rows_per_group, 3),
                  _resident((1, D_MODEL)), sq, sq, sq,
                  _resident((D_MODEL, ROUTER_LANES)), _resident((D_MODEL, ROUTER_LANES)),
                  _resident((1, ROUTER_LANES))],
        out_specs=[row(D_MODEL), row(D_MODEL), row(ROUTER_LANES)],
        out_shape=[jax.ShapeDtypeStruct((t, D_MODEL), F32),
                   jax.ShapeDtypeStruct((t, D_MODEL), F32),
                   jax.ShapeDtypeStruct((t, ROUTER_LANES), F32)],
        compiler_params=_cparams(("arbitrary",)),
        name="merge",
    )(x, attn, rnn, ga, gr, mods, mods, mods, n2g, wpa, wpr, wo, wr_hi, wr_lo, br)


def _experts_body(te_ref, nv_ref, src_ref, dst_ref,
                  u_hbm, wg_ref, wu_ref, wd_ref, wrow_ref,
                  y_hbm,
                  ubuf, ybuf, wg_sc, wu_sc, wd_sc, sem_in, sem_out):
    i = pl.program_id(0)
    n_tiles = pl.num_programs(0)
    tile = ybuf.shape[0]
    slot = i % 2

    def gather_row(tile_idx, buf_slot, r):
        return pltpu.make_async_copy(u_hbm.at[pl.ds(src_ref[tile_idx * tile + r], 1)],
                                     ubuf.at[buf_slot, pl.ds(r, 1)], sem_in.at[buf_slot])

    def scatter_row(tile_idx, r):
        return pltpu.make_async_copy(ybuf.at[pl.ds(r, 1)],
                                     y_hbm.at[pl.ds(dst_ref[tile_idx * tile + r], 1)], sem_out.at[0])

    def start_gather(tile_idx, buf_slot):
        def body(r, c):
            gather_row(tile_idx, buf_slot, r).start()
            return c
        lax.fori_loop(0, nv_ref[tile_idx], body, 0)

    @pl.when(i == 0)
    def _():
        ubuf[...] = jnp.zeros(ubuf.shape, F32)
        start_gather(0, 0)

    @pl.when(i + 1 < n_tiles)
    def _():
        start_gather(i + 1, 1 - slot)

    def wait_in(r, c):
        gather_row(i, slot, r).wait()
        return c
    lax.fori_loop(0, nv_ref[i], wait_in, 0)

    @pl.when(i >= 1)
    def _():
        def wait_out(r, c):
            scatter_row(i - 1, r).wait()
            return c
        lax.fori_loop(0, nv_ref[i - 1], wait_out, 0)

    @pl.when((i == 0) | (te_ref[i] != te_ref[jnp.maximum(i - 1, 0)]))
    def _():
        wg_sc[...] = wg_ref[...].astype(BF16)
        wu_sc[...] = wu_ref[...].astype(BF16)
        wd_sc[...] = wd_ref[...].astype(BF16)

    @pl.when(nv_ref[i] > 0)
    def _():
        u = ubuf[slot].astype(BF16)
        hg = _dot(u, wg_sc[...])
        hid = (hg * jax.nn.sigmoid(hg)) * _dot(u, wu_sc[...])
        ybuf[...] = wrow_ref[...] * _dot(hid.astype(BF16), wd_sc[...])

        def out_body(r, c):
            scatter_row(i, r).start()
            return c
        lax.fori_loop(0, nv_ref[i], out_body, 0)

    @pl.when(i == n_tiles - 1)
    def _():
        def wait_last(r, c):
            scatter_row(i, r).wait()
            return c
        lax.fori_loop(0, nv_ref[i], wait_last, 0)


def _experts(tile_expert, tile_valid, src_rows, dst_rows, u_all, w_gate, w_up, w_down, w_rows, n_out_rows):
    tile = EXPERT_TILE
    n_tiles = tile_expert.shape[0]
    grid_spec = pltpu.PrefetchScalarGridSpec(
        num_scalar_prefetch=4,
        grid=(n_tiles,),
        in_specs=[pl.BlockSpec(memory_space=pl.ANY),
                  pl.BlockSpec((None, D_MODEL, D_EXPERT), lambda i, te, nv, s, d: (te[i], 0, 0)),
                  pl.BlockSpec((None, D_MODEL, D_EXPERT), lambda i, te, nv, s, d: (te[i], 0, 0)),
                  pl.BlockSpec((None, D_EXPERT, D_MODEL), lambda i, te, nv, s, d: (te[i], 0, 0)),
                  pl.BlockSpec((tile, 1), lambda i, te, nv, s, d: (i, 0))],
        out_specs=pl.BlockSpec(memory_space=pl.ANY),
        scratch_shapes=[pltpu.VMEM((2, tile, D_MODEL), F32),
                        pltpu.VMEM((tile, D_MODEL), F32),
                        pltpu.VMEM((D_MODEL, D_EXPERT), BF16),
                        pltpu.VMEM((D_MODEL, D_EXPERT), BF16),
                        pltpu.VMEM((D_EXPERT, D_MODEL), BF16),
                        pltpu.SemaphoreType.DMA((2,)),
                        pltpu.SemaphoreType.DMA((1,))],
    )
    return pl.pallas_call(
        _experts_body,
        grid_spec=grid_spec,
        out_shape=jax.ShapeDtypeStruct((n_out_rows, D_MODEL), F32),
        compiler_params=_cparams(("arbitrary",)),
        name="experts",
    )(tile_expert, tile_valid, src_rows, dst_rows, u_all, w_gate, w_up, w_down, w_rows)


def _dispatch_plan(rec_all, slot_stride):
    t_all = rec_all.shape[0]
    tile = EXPERT_TILE
    n_assign = 2 * t_all
    n_tiles = -(-n_assign // tile) + N_EXPERTS
    e_flat = rec_all[:, 0:2].astype(jnp.int32).T.reshape(-1)
    w_flat = rec_all[:, 2:4].T.reshape(-1)
    onehot = (e_flat[:, None] == jnp.arange(N_EXPERTS, dtype=jnp.int32)[None, :]).astype(jnp.int32)
    counts = jnp.sum(onehot, axis=0)
    rank = jnp.take_along_axis(jnp.cumsum(onehot, axis=0) - onehot, e_flat[:, None], axis=1)[:, 0]
    tiles_per_e = (counts + tile - 1) // tile
    tile_end = jnp.cumsum(tiles_per_e)
    tile_start = tile_end - tiles_per_e
    pos = tile_start[e_flat] * tile + rank
    a_idx = jnp.arange(n_assign, dtype=jnp.int32)
    src_rows = jnp.zeros((n_tiles * tile,), jnp.int32).at[pos].set(a_idx % t_all)
    dst_rows = jnp.zeros((n_tiles * tile,), jnp.int32).at[pos].set(
        (a_idx // t_all) * slot_stride + a_idx % t_all)
    w_rows = jnp.zeros((n_tiles * tile,), F32).at[pos].set(w_flat)
    n_used = tile_end[-1]
    tile_ids = jnp.arange(n_tiles, dtype=jnp.int32)
    te = jnp.searchsorted(tile_end, jnp.minimum(tile_ids, n_used - 1), side="right").astype(jnp.int32)
    te = jnp.minimum(te, N_EXPERTS - 1)
    valid = jnp.clip(counts[te] - (tile_ids - tile_start[te]) * tile, 0, tile)
    valid = jnp.where(tile_ids < n_used, valid, 0).astype(jnp.int32)
    return te, valid, src_rows, dst_rows, w_rows.reshape(-1, 1)


def _combine_body(x_ref, y0_ref, y1_ref, g2_ref, o_ref):
    o_ref[...] = x_ref[...] + g2_ref[...] * (y0_ref[...] + y1_ref[...])


def _combine(x1, y2, mods, rows_per_group, row_offset, slot_stride):
    t = x1.shape[0]
    tile = min(TOKEN_TILE, t)
    assert row_offset % tile == 0 and slot_stride % tile == 0
    off0 = row_offset // tile
    off1 = (slot_stride + row_offset) // tile
    return pl.pallas_call(
        _combine_body,
        grid=(t // tile,),
        in_specs=[pl.BlockSpec((tile, D_MODEL), lambda i: (i, 0)),
                  pl.BlockSpec((tile, D_MODEL), lambda i: (i + off0, 0)),
                  pl.BlockSpec((tile, D_MODEL), lambda i: (i + off1, 0)),
                  _mod_spec(mods, tile, rows_per_group, 5)],
        out_specs=pl.BlockSpec((tile, D_MODEL), lambda i: (i, 0)),
        out_shape=jax.ShapeDtypeStruct((t, D_MODEL), F32),
        compiler_params=_cparams(("arbitrary",)),
        name="combine",
    )(x1, y2, y2, mods)


def _block_diag(w):
    per = MXU_DIM // RNN_BLOCK
    w4 = w.reshape(N_RNN_BLOCKS // per, per, RNN_BLOCK, RNN_BLOCK)
    eye = jnp.eye(per, dtype=w.dtype)
    return jnp.einsum("jmcd,mn->jmcnd", w4, eye).reshape(N_RNN_BLOCKS // per, MXU_DIM, MXU_DIM)


def kernel(x_prompt, x_sample, cache_k, cache_v, state_rnn, state_conv, page_table, c_prompt, c_sample, rel_bias, w_mod, b_mod, norm1_g, norm2_g, w_in, q_norm_g, k_norm_g, lam_q1, lam_k1, lam_q2, lam_k2, subln_g, conv_w, conv_b, lru_wa, lru_ba, lru_wi, lru_bi, lru_lambda, w_pa, w_pr, w_o, w_rg, b_rg, w_re, b_re, w_e_gate, w_e_up, w_e_down):
    bp, tp, _ = x_prompt.shape
    bs, ts, _ = x_sample.shape
    depth = w_in.shape[0]
    n_pages = page_table.shape[1]
    past_len = n_pages * PAGE_SIZE
    t_p = bp * tp
    t_s = bs * ts
    t_all = t_p + t_s
    slot_stride = -(-t_all // TOKEN_TILE) * TOKEN_TILE
    blk = ATTN_BLOCK
    nblk = tp // blk
    assert tp % blk == 0 and tp % TIME_TILE == 0 and t_p % TOKEN_TILE == 0
    assert t_s % SUBLANES == 0 and t_p % min(TOKEN_TILE, t_s) == 0

    rows_c = -(-(bp + bs) // SUBLANES) * SUBLANES
    c_all = jnp.zeros((rows_c, D_MODEL), F32).at[:bp].set(c_prompt).at[bp:bp + bs].set(c_sample)
    mods = _mods(c_all, w_mod, b_mod)
    mods_p = mods[:, :bp].reshape(depth, bp, 1, 6 * D_MODEL)
    mods_s = jnp.tile(mods[:, bp:bp + bs], (1, ts, 1))

    w_in_bf = w_in.astype(BF16)
    w_pa_bf, w_pr_bf, w_o_bf = w_pa.astype(BF16), w_pr.astype(BF16), w_o.astype(BF16)
    qg_row = jnp.tile(q_norm_g, (1, Q_COLS // HEAD_DIM)).reshape(depth, 1, Q_COLS)
    kg_row = jnp.tile(k_norm_g, (1, K_COLS // HEAD_DIM)).reshape(depth, 1, K_COLS)
    seg = jnp.arange(MXU_DIM) // HEAD_DIM
    chunk_p = ((seg[:, None] == seg[None, :]).astype(F32) / HEAD_DIM).astype(BF16)
    wr = jnp.concatenate([w_rg, w_re], axis=-1)
    wr = jnp.pad(wr, ((0, 0), (0, 0), (0, ROUTER_LANES - wr.shape[-1])))
    wr_hi = wr.astype(BF16)
    wr_lo = (wr - wr_hi.astype(F32)).astype(BF16)
    br = jnp.pad(jnp.concatenate([b_rg, b_re], axis=-1), ((0, 0), (0, ROUTER_LANES - N_GROUPS - N_EXPERTS)))
    br = br.reshape(depth, 1, ROUTER_LANES)
    wa_bd = jax.vmap(_block_diag)(lru_wa).astype(BF16)
    wi_bd = jax.vmap(_block_diag)(lru_wi).astype(BF16)
    ba = lru_ba.reshape(depth, 1, D_MODEL)
    bi = lru_bi.reshape(depth, 1, D_MODEL)
    sp = jax.nn.softplus(-lru_lambda.astype(F32)).reshape(depth, 1, D_MODEL)
    cb = conv_b.reshape(depth, 1, D_MODEL)
    lam_f = lambda a, b: jnp.exp(jnp.sum(a.astype(F32) * b.astype(F32), axis=-1))
    lam_base = lam_f(lam_q1, lam_k1) - lam_f(lam_q2, lam_k2)

    kq = jnp.arange(blk, dtype=jnp.int32)
    n_diag = kq[None, :] - kq[:, None]
    bias_diag = _bias_of_distance(rel_bias, n_diag)
    bias_prev = _bias_of_distance(rel_bias, n_diag + blk)
    far_bias = rel_bias.astype(F32)[NUM_BUCKETS - 1]
    keys_s = past_len + LANES
    kpos = jnp.arange(keys_s, dtype=jnp.int32)
    qpos = past_len + jnp.arange(ts, dtype=jnp.int32)
    n_s = jnp.where(kpos[None, :] < past_len + ts, qpos[:, None] - kpos[None, :], -1)
    b_s = _bias_of_distance(rel_bias, n_s).reshape(N_KV_HEADS, GQA_GROUP, ts, keys_s)
    b_s = jnp.transpose(b_s, (0, 2, 1, 3)).reshape(N_KV_HEADS, 1, ts * GQA_GROUP, keys_s)
    bias_s = jnp.broadcast_to(b_s, (N_KV_HEADS, 2, ts * GQA_GROUP, keys_s)).reshape(
        N_KV_HEADS, 2 * ts * GQA_GROUP, keys_s)
    pt_flat = page_table.reshape(-1).astype(jnp.int32)
    n_new = -(-ts // SUBLANES) * SUBLANES

    xp = x_prompt.reshape(t_p, D_MODEL)
    xs = jnp.swapaxes(x_sample, 0, 1).reshape(t_s, D_MODEL)
    outs = [[] for _ in range(8)]
    for l in range(depth):
        lam_init = 0.8 - 0.6 * math.exp(-0.3 * l)
        lam = lam_base[l] + lam_init
        sg_col = subln_g[l].astype(F32).reshape(V_DIM, 1)
        sg_row = subln_g[l].astype(F32).reshape(1, V_DIM)
        scal_p = jnp.concatenate([jnp.stack([lam, jnp.asarray(1.0 - lam_init, F32)]), far_bias]).astype(F32)
        scal_s = scal_p[:2]
        lp = (norm1_g[l].reshape(1, D_MODEL), w_in_bf[l], qg_row[l], kg_row[l], chunk_p)
        rg = (conv_w[l], cb[l], wa_bd[l], wi_bd[l], ba[l], bi[l], sp[l])
        mg = (norm2_g[l].reshape(1, D_MODEL), w_pa_bf[l], w_pr_bf[l], w_o_bf[l], wr_hi[l], wr_lo[l], br[l])

        q, k, kb, v, xr, yr, ga, gr = _in_proj(xp, mods_p[l], tp, *lp)
        vt = jnp.swapaxes(v.reshape(bp, nblk, blk, N_KV_HEADS, V_DIM).astype(BF16), 2, 4)
        vt = jnp.transpose(vt, (0, 3, 1, 2, 4)).reshape(bp * N_KV_HEADS, nblk, V_DIM, blk)
        ones = jnp.zeros((bp * N_KV_HEADS, nblk, ONES_ROWS, blk), BF16).at[:, :, 0].set(1.0)
        vt_ext = jnp.concatenate([vt, ones], axis=2)
        attn = _attn_prompt(scal_p, q, kb, vt_ext, bias_diag, bias_prev, sg_col, bp, tp)
        rnn, h_last = _rglru_prompt(xr, yr, *rg, bp, tp)
        x1p, u2p, recp = _merge(xp, attn, rnn, ga, gr, mods_p[l], tp, *mg)
        outs[0].append(k.reshape(bp, tp, N_KV_HEADS, 2 * HEAD_DIM))
        outs[1].append(v.reshape(bp, tp, N_KV_HEADS, V_DIM))
        outs[2].append(h_last.reshape(bp, D_MODEL))
        outs[3].append(xr.reshape(bp, tp, D_MODEL)[:, tp - (CONV_W - 1):])

        q, k, kb, v, xr, yr, ga, gr = _in_proj(xs, mods_s[l], ts, *lp)
        q5 = q.reshape(ts, bs, N_KV_HEADS, GQA_GROUP, 2, HEAD_DIM)
        q5 = jnp.transpose(q5, (1, 2, 4, 0, 3, 5))
        zq = jnp.zeros_like(q5[:, :, 0])
        qbd = jnp.stack([jnp.concatenate([q5[:, :, 0], zq], axis=-1),
                         jnp.concatenate([zq, q5[:, :, 1]], axis=-1)], axis=2)
        qbd = qbd.reshape(bs, N_KV_HEADS, 2 * ts * GQA_GROUP, 2 * HEAD_DIM)
        new_rows = lambda z: jnp.pad(jnp.transpose(z.reshape(ts, bs, N_KV_HEADS, LANES), (1, 2, 0, 3)),
                                     ((0, 0), (0, 0), (0, n_new - ts), (0, 0)))
        o_s = _attn_sample(pt_flat, jnp.full((1,), l, jnp.int32), scal_s, qbd, new_rows(kb),
                           new_rows(v.astype(BF16)), bias_s, sg_row, cache_k, cache_v, n_pages)
        attn = jnp.transpose(o_s.reshape(bs, N_KV_HEADS, ts, GQA_GROUP, V_DIM), (2, 0, 1, 3, 4))
        attn = attn.reshape(t_s, N_HEADS * V_DIM).astype(BF16)
        buf = jnp.swapaxes(state_conv[l], 0, 1)
        rnn, h_last = _rglru_sample(xr, yr, buf, state_rnn[l], *rg, ts, past_len == 0)
        x1s, u2s, recs = _merge(xs, attn, rnn, ga, gr, mods_s[l], ts, *mg)
        to_bt = lambda z, n: jnp.swapaxes(z.reshape(ts, bs, n), 0, 1)
        outs[4].append(to_bt(k, K_COLS).reshape(bs, ts, N_KV_HEADS, 2 * HEAD_DIM))
        outs[5].append(to_bt(v, V_COLS).reshape(bs, ts, N_KV_HEADS, V_DIM))
        outs[6].append(h_last)
        xin = jnp.concatenate([state_conv[l], to_bt(xr, D_MODEL)], axis=1)
        outs[7].append(xin[:, -(CONV_W - 1):])

        u_all = jnp.concatenate([u2p, u2s], axis=0)
        plan = _dispatch_plan(jnp.concatenate([recp, recs], axis=0), slot_stride)
        y2 = _experts(*plan[:4], u_all, w_e_gate[l], w_e_up[l], w_e_down[l], plan[4], 2 * slot_stride)
        xp = _combine(x1p, y2, mods_p[l], tp, 0, slot_stride)
        xs = _combine(x1s, y2, mods_s[l], ts, t_p, slot_stride)

    y_sample = jnp.swapaxes(xs.reshape(ts, bs, D_MODEL), 0, 1)
    stack = lambda i: jnp.stack(outs[i])
    return (xp.reshape(bp, tp, D_MODEL), y_sample, stack(0), stack(1), stack(2), stack(3),
            stack(4), stack(5), stack(6), stack(7))
```

```python
import functools
import math

import jax
import jax.numpy as jnp
from jax import lax
from jax.experimental import pallas as pl
from jax.experimental.pallas import tpu as pltpu

F32 = jnp.float32
BF16 = jnp.bfloat16

D_MODEL = 1024
HEAD_DIM = 64
N_HEADS = 8
N_KV_HEADS = 4
GQA_GROUP = 2
V_DIM = 128
ATTN_SCALE = HEAD_DIM ** -0.5
MASK_VALUE = -1e30
NUM_BUCKETS = 32
MAX_EXACT = 16
MAX_DISTANCE = 128
N_RNN_BLOCKS = 16
RNN_BLOCK = 64
CONV_W = 4
LRU_C = 8.0
N_GROUPS = 4
EXPERTS_PER_GROUP = 8
N_EXPERTS = 32
D_EXPERT = 512
NORM_EPS = 1e-6
PAGE_SIZE = 128
Q_COLS = 1024
K_COLS = 512
V_COLS = 512
D_IN = 6144

LANES = 128
SUBLANES = 8
MXU_DIM = 256
VMEM_LIMIT_BYTES = 56 * 1024 * 1024

TOKEN_TILE = 256
ATTN_BLOCK = 512
TIME_TILE = 256
EXPERT_TILE = 256
ROUTER_LANES = 128
ONES_ROWS = 16

_NT = (((1,), (1,)), ((), ()))


def _cparams(sem):
    return pltpu.CompilerParams(dimension_semantics=sem, vmem_limit_bytes=VMEM_LIMIT_BYTES)


def _dot(a, b):
    return jnp.dot(a, b, preferred_element_type=F32)


def _resident(shape):
    n = len(shape)
    return pl.BlockSpec(shape, lambda *_: (0,) * n)


_MODS_COLS = 1536


def _mods_body(c_ref, w_ref, b_ref, o_ref):
    c = c_ref[...]
    s = c * jax.nn.sigmoid(c)
    o_ref[...] = jnp.dot(s, w_ref[...], precision=lax.Precision.HIGHEST,
                         preferred_element_type=F32) + b_ref[...]


def _mods(c_all, w_mod, b_mod):
    rows = c_all.shape[0]
    depth, _, n = w_mod.shape
    return pl.pallas_call(
        _mods_body,
        grid=(depth, n // _MODS_COLS),
        in_specs=[pl.BlockSpec((rows, D_MODEL), lambda l, j: (0, 0)),
                  pl.BlockSpec((None, D_MODEL, _MODS_COLS), lambda l, j: (l, 0, j)),
                  pl.BlockSpec((None, 1, _MODS_COLS), lambda l, j: (l, 0, j))],
        out_specs=pl.BlockSpec((None, rows, _MODS_COLS), lambda l, j: (l, 0, j)),
        out_shape=jax.ShapeDtypeStruct((depth, rows, n), F32),
        compiler_params=_cparams(("arbitrary", "arbitrary")),
        name="mods",
    )(c_all, w_mod, b_mod.reshape(depth, 1, n))


def _rms_rows(x, g):
    ms = jnp.mean(x * x, axis=-1, keepdims=True)
    return x * lax.rsqrt(ms + NORM_EPS) * g


def _chunk_mean_square(z, p):
    parts = []
    for j in range(z.shape[1] // MXU_DIM):
        zz = z[:, j * MXU_DIM:(j + 1) * MXU_DIM]
        parts.append(_dot((zz * zz).astype(BF16), p))
    return jnp.concatenate(parts, axis=1)


def _in_proj_body(q_scale, x_ref, sc_ref, sh_ref, g_ref, w_ref, qg_ref, kg_ref, p_ref,
                  q_ref, k_ref, kb_ref, v_ref, xr_ref, yr_ref, ga_ref, gr_ref):
    u = (_rms_rows(x_ref[...], g_ref[...]) * (1.0 + sc_ref[...]) + sh_ref[...]).astype(BF16)
    p = p_ref[...]
    zq = _dot(u, w_ref[:, 0:Q_COLS])
    q = zq * lax.rsqrt(_chunk_mean_square(zq, p) + NORM_EPS) * qg_ref[...]
    q_ref[...] = (q * q_scale).astype(BF16)
    zk = _dot(u, w_ref[:, Q_COLS:Q_COLS + K_COLS])
    k = zk * lax.rsqrt(_chunk_mean_square(zk, p) + NORM_EPS) * kg_ref[...]
    k_ref[...] = k
    kb_ref[...] = k.astype(BF16)
    c0 = Q_COLS + K_COLS
    v_ref[...] = _dot(u, w_ref[:, c0:c0 + V_COLS])
    c0 += V_COLS
    xr_ref[...] = _dot(u, w_ref[:, c0:c0 + D_MODEL])
    c0 += D_MODEL
    yr_ref[...] = _dot(u, w_ref[:, c0:c0 + D_MODEL])
    c0 += D_MODEL
    ga_ref[...] = _dot(u, w_ref[:, c0:c0 + D_MODEL])
    c0 += D_MODEL
    gr_ref[...] = _dot(u, w_ref[:, c0:c0 + D_MODEL])


def _mod_spec(mods, tile, rows_per_group, col_block):
    if mods.ndim == 3:
        tiles_per_group = rows_per_group // tile
        return pl.BlockSpec((None, 1, D_MODEL), lambda i: (i // tiles_per_group, 0, col_block))
    return pl.BlockSpec((tile, D_MODEL), lambda i: (i, col_block))


def _in_proj(x, mods, rows_per_group, q_scale, norm_g, w_in_bf, qg_row, kg_row, chunk_p):
    t = x.shape[0]
    tile = min(TOKEN_TILE, t)
    row = lambda n: pl.BlockSpec((tile, n), lambda i: (i, 0))
    outs = [(Q_COLS, BF16), (K_COLS, F32), (K_COLS, BF16), (V_COLS, F32),
            (D_MODEL, F32), (D_MODEL, F32), (D_MODEL, F32), (D_MODEL, F32)]
    return pl.pallas_call(
        functools.partial(_in_proj_body, q_scale),
        grid=(t // tile,),
        in_specs=[row(D_MODEL),
                  _mod_spec(mods, tile, rows_per_group, 1),
                  _mod_spec(mods, tile, rows_per_group, 0),
                  _resident((1, D_MODEL)),
                  _resident((D_MODEL, D_IN)),
                  _resident((1, Q_COLS)),
                  _resident((1, K_COLS)),
                  _resident((MXU_DIM, MXU_DIM))],
        out_specs=[row(n) for n, _ in outs],
        out_shape=[jax.ShapeDtypeStruct((t, n), dt) for n, dt in outs],
        compiler_params=_cparams(("arbitrary",)),
        name="in_proj",
    )(x, mods, mods, norm_g, w_in_bf, qg_row, kg_row, chunk_p)


def _bucket_of_distance(n):
    n = jnp.maximum(n, 0)
    nf = jnp.maximum(n, 1).astype(F32)
    large = MAX_EXACT + (jnp.log(nf / MAX_EXACT) / math.log(MAX_DISTANCE / MAX_EXACT)
                         * (NUM_BUCKETS - MAX_EXACT)).astype(jnp.int32)
    return jnp.where(n < MAX_EXACT, n, jnp.minimum(large, NUM_BUCKETS - 1))


def _bias_of_distance(rel_bias, n):
    b = rel_bias.astype(F32)[_bucket_of_distance(n)]
    b = jnp.where((n >= 0)[..., None], b, MASK_VALUE)
    return jnp.moveaxis(b, -1, 0)


_S_LAM, _S_OUT = 0, 1
_S_FAR = 2
_S_BMAX = _S_FAR + N_HEADS
_S_BRANGE = _S_BMAX + N_HEADS
BOUND_MARGIN = 1.01
SAFE_EXPONENT_SPAN = 100.0


def _attn_prompt_body(scal_ref, q_ref, k_ref, vt_ref, bdiag_ref, bprev_ref, sg_ref,
                      o_ref, qpad_sc, kmax_sc, bq_sc, m_sc, acc_sc):
    h = pl.program_id(1)
    qi = pl.program_id(2)
    blk = q_ref.shape[0]
    ncombo = 2 * GQA_GROUP
    lam = scal_ref[_S_LAM]
    out_scale = scal_ref[_S_OUT]

    @pl.when(qi == 0)
    def _():
        kk = k_ref[...].astype(F32)
        r = lax.broadcasted_iota(jnp.int32, (LANES, LANES), 0) // HEAD_DIM
        c = lax.broadcasted_iota(jnp.int32, (LANES, LANES), 1) // HEAD_DIM
        same = jnp.where(r == c, 1.0, 0.0).astype(BF16)
        ksq = _dot((kk * kk).astype(BF16), same)
        kmax_sc[...] = jnp.sqrt(jnp.max(ksq, axis=0, keepdims=True))

    lane = lax.broadcasted_iota(jnp.int32, (blk, LANES), 1)
    ones = jnp.ones((SUBLANES, LANES), BF16)
    span = jnp.zeros((1, 1), F32)
    for g in range(GQA_GROUP):
        q2 = q_ref[:, g * LANES:(g + 1) * LANES]
        for c in range(2):
            idx = 2 * g + c
            qp = jnp.where((lane < HEAD_DIM) == (c == 0), q2, jnp.zeros_like(q2))
            qpad_sc[idx] = qp
            qf = qp.astype(F32)
            qsq = lax.dot_general(ones, (qf * qf).astype(BF16), _NT, preferred_element_type=F32)[0:1]
            kmax = jnp.max(kmax_sc[:, c * HEAD_DIM:(c + 1) * HEAD_DIM], axis=1, keepdims=True)
            bqk = BOUND_MARGIN * jnp.sqrt(qsq) * kmax
            bq_sc[idx] = bqk + scal_ref[_S_BMAX + h * GQA_GROUP + g]
            span = jnp.maximum(span, 2.0 * jnp.max(bqk, axis=1, keepdims=True)
                               + scal_ref[_S_BRANGE + h * GQA_GROUP + g])
    acc_sc[...] = jnp.zeros(acc_sc.shape, F32)
    far_bias = lambda g: scal_ref[_S_FAR + h * GQA_GROUP + g]
    n_far = jnp.maximum(qi - 1, 0)

    def k_block(kj):
        return k_ref[pl.ds(pl.multiple_of(kj * blk, blk), blk), :]

    def fast_step(kj, shift_of_group, tile_of_group):
        k_blk = k_block(kj)
        vt = vt_ref[kj]
        qk = lambda idx: lax.dot_general(k_blk, qpad_sc[idx], _NT, preferred_element_type=F32)
        st_next = qk(0)
        for idx in range(ncombo):
            st = st_next
            if idx + 1 < ncombo:
                st_next = qk(idx + 1)
            g = idx // 2
            if tile_of_group is not None:
                st = st + tile_of_group(g)
            bound = bq_sc[idx] if shift_of_group is None else bq_sc[idx] - shift_of_group(g)
            acc_sc[idx] += _dot(vt, jnp.exp2(st - bound).astype(BF16))

    def exact_step(kj, shift_of_group, tile_of_group):
        k_blk = k_block(kj)
        vt = vt_ref[kj]
        for idx in range(ncombo):
            g = idx // 2
            st = lax.dot_general(k_blk, qpad_sc[idx], _NT, preferred_element_type=F32)
            if tile_of_group is not None:
                st = st + tile_of_group(g)
            shift = 0.0 if shift_of_group is None else shift_of_group(g)
            m_old = m_sc[idx]
            m_new = jnp.maximum(m_old, jnp.max(st, axis=0, keepdims=True) + shift)
            p = jnp.exp2(st - (m_new - shift)).astype(BF16)
            acc_sc[idx] = jnp.exp2(m_old - m_new) * acc_sc[idx] + _dot(vt, p)
            m_sc[idx] = m_new

    def run(step):
        def far_step(kj, carry):
            step(kj, far_bias, None)
            return carry
        lax.fori_loop(0, n_far, far_step, 0)

        @pl.when(qi >= 1)
        def _():
            step(qi - 1, None, lambda g: bprev_ref[g])
        step(qi, None, lambda g: bdiag_ref[g])

    def run_exact():
        m_sc[...] = jnp.full(m_sc.shape, -jnp.inf, F32)
        run(exact_step)

    lax.cond(span[0, 0] <= SAFE_EXPONENT_SPAN, lambda: run(fast_step), run_exact)

    for g in range(GQA_GROUP):
        a0 = acc_sc[2 * g]
        a1 = acc_sc[2 * g + 1]
        o = a0[:V_DIM] / a0[V_DIM:V_DIM + 1] - lam * (a1[:V_DIM] / a1[V_DIM:V_DIM + 1])
        o = o * lax.rsqrt(jnp.mean(o * o, axis=0, keepdims=True) + NORM_EPS)
        o = o * sg_ref[...] * out_scale
        o_ref[:, g * V_DIM:(g + 1) * V_DIM] = o.T.astype(BF16)


def _attn_prompt(scal, q_bf, k_bf, vt_ext, bias_diag, bias_prev, sg_col, batch, seq):
    blk = ATTN_BLOCK
    nblk = seq // blk
    vrows = V_DIM + ONES_ROWS
    ncombo = 2 * GQA_GROUP
    grid_spec = pltpu.PrefetchScalarGridSpec(
        num_scalar_prefetch=1,
        grid=(batch, N_KV_HEADS, nblk),
        in_specs=[pl.BlockSpec((blk, GQA_GROUP * 2 * HEAD_DIM), lambda b, h, i, s: (b * nblk + i, h)),
                  pl.BlockSpec((seq, 2 * HEAD_DIM), lambda b, h, i, s: (b, h)),
                  pl.BlockSpec((None, nblk, vrows, blk), lambda b, h, i, s: (b * N_KV_HEADS + h, 0, 0, 0)),
                  pl.BlockSpec((GQA_GROUP, blk, blk), lambda b, h, i, s: (h, 0, 0)),
                  pl.BlockSpec((GQA_GROUP, blk, blk), lambda b, h, i, s: (h, 0, 0)),
                  pl.BlockSpec((V_DIM, 1), lambda b, h, i, s: (0, 0))],
        out_specs=pl.BlockSpec((blk, GQA_GROUP * V_DIM), lambda b, h, i, s: (b * nblk + i, h)),
        scratch_shapes=[pltpu.VMEM((ncombo, blk, LANES), BF16),
                        pltpu.VMEM((1, LANES), F32),
                        pltpu.VMEM((ncombo, 1, blk), F32),
                        pltpu.VMEM((ncombo, 1, blk), F32),
                        pltpu.VMEM((ncombo, vrows, blk), F32)],
    )
    return pl.pallas_call(
        _attn_prompt_body,
        grid_spec=grid_spec,
        out_shape=jax.ShapeDtypeStruct((batch * seq, N_HEADS * V_DIM), BF16),
        compiler_params=_cparams(("arbitrary", "arbitrary", "arbitrary")),
        name="attn_prompt",
    )(scal, q_bf, k_bf, vt_ext, bias_diag, bias_prev, sg_col)


def _attn_sample_body(n_pages, pt_ref, layer_ref, scal_ref, q_ref, knew_ref, vnew_ref, bias_ref, sg_ref, *refs):
    k_pages = refs[:n_pages]
    v_pages = refs[n_pages:2 * n_pages]
    o_ref, kb_sc, vb_sc = refs[2 * n_pages:]
    page_rows = k_pages[0].shape[2]
    past = n_pages * page_rows
    n_new = knew_ref.shape[1]
    lam = scal_ref[0]
    out_scale = scal_ref[1]
    half = q_ref.shape[1] // 2

    @pl.when(pl.program_id(0) == 0)
    def _():
        tail = kb_sc.shape[0] - past - n_new
        kb_sc[past + n_new:, :] = jnp.zeros((tail, LANES), BF16)
        vb_sc[past + n_new:, :] = jnp.zeros((tail, LANES), BF16)

    for j in range(n_pages):
        kb_sc[j * page_rows:(j + 1) * page_rows, :] = k_pages[j][0, 0].astype(BF16)
        vb_sc[j * page_rows:(j + 1) * page_rows, :] = v_pages[j][0, 0].astype(BF16)
    kb_sc[past:past + n_new, :] = knew_ref[0]
    vb_sc[past:past + n_new, :] = vnew_ref[0]
    s = lax.dot_general(q_ref[0], kb_sc[...], _NT, preferred_element_type=F32) + bias_ref[...]
    p = jnp.exp(s - jnp.max(s, axis=-1, keepdims=True))
    p = p / jnp.sum(p, axis=-1, keepdims=True)
    a = p[:half] - lam * p[half:]
    o = _dot(a.astype(BF16), vb_sc[...])
    o = o * lax.rsqrt(jnp.mean(o * o, axis=-1, keepdims=True) + NORM_EPS)
    o_ref[0] = o * sg_ref[...] * out_scale


def _attn_sample(page_table_flat, layer_idx, scal, q_rows, knew, vnew, bias_s, sg_row, cache_k, cache_v, n_pages):
    nb, rows, _ = q_rows.shape
    n_new = knew.shape[1]
    page_rows = cache_k.shape[2]
    keys = bias_s.shape[1]

    def page_spec(j):
        return pl.BlockSpec((1, 1, page_rows, LANES), lambda b, pt, l: (l[0], pt[b * n_pages + j], 0, 0))

    grid_spec = pltpu.PrefetchScalarGridSpec(
        num_scalar_prefetch=2,
        grid=(nb,),
        in_specs=[pl.BlockSpec(memory_space=pltpu.SMEM),
                  pl.BlockSpec((1, rows, LANES), lambda b, pt, l: (b, 0, 0)),
                  pl.BlockSpec((1, n_new, LANES), lambda b, pt, l: (b, 0, 0)),
                  pl.BlockSpec((1, n_new, LANES), lambda b, pt, l: (b, 0, 0)),
                  pl.BlockSpec((rows, keys), lambda b, pt, l: (0, 0)),
                  pl.BlockSpec((1, V_DIM), lambda b, pt, l: (0, 0))]
                 + [page_spec(j) for j in range(n_pages)] * 2,
        out_specs=pl.BlockSpec((1, rows // 2, V_DIM), lambda b, pt, l: (b, 0, 0)),
        scratch_shapes=[pltpu.VMEM((keys, LANES), BF16),
                        pltpu.VMEM((keys, LANES), BF16)],
    )
    return pl.pallas_call(
        functools.partial(_attn_sample_body, n_pages),
        grid_spec=grid_spec,
        out_shape=jax.ShapeDtypeStruct((nb, rows // 2, V_DIM), F32),
        compiler_params=_cparams(("arbitrary",)),
        name="attn_sample",
    )(page_table_flat, layer_idx, scal, q_rows, knew, vnew, bias_s, sg_row,
      *([cache_k] * n_pages), *([cache_v] * n_pages))


def _lru_coeffs(xc, wa_ref, wi_ref, ba_ref, bi_ref, sp_ref):
    xb = xc.astype(BF16)
    ra, ri = [], []
    for j in range(xc.shape[1] // MXU_DIM):
        cols = slice(j * MXU_DIM, (j + 1) * MXU_DIM)
        ra.append(_dot(xb[:, cols], wa_ref[j]))
        ri.append(_dot(xb[:, cols], wi_ref[j]))
    r = jax.nn.sigmoid(jnp.concatenate(ra, axis=1) + ba_ref[...])
    i = jax.nn.sigmoid(jnp.concatenate(ri, axis=1) + bi_ref[...])
    log_a = -LRU_C * r * sp_ref[...]
    a = jnp.exp(log_a)
    mult = jnp.sqrt(-jnp.tanh(log_a) * (a * a + 1.0))
    return a, mult, i


def _rglru_prompt_body(xr_ref, yr_ref, cw_ref, cb_ref, wa_ref, wi_ref, ba_ref, bi_ref, sp_ref,
                       rnn_ref, hl_ref, xe_sc, a_sc, b_sc, h_sc):
    i = pl.program_id(1)
    tt = xr_ref.shape[0]

    @pl.when(i == 0)
    def _():
        xe_sc[0:SUBLANES, :] = jnp.zeros((SUBLANES, D_MODEL), F32)
        h_sc[...] = jnp.zeros(h_sc.shape, F32)

    xe_sc[SUBLANES:SUBLANES + tt, :] = xr_ref[...]
    xc = cb_ref[...]
    for j in range(CONV_W):
        off = SUBLANES - (CONV_W - 1) + j
        xc = xc + xe_sc[off:off + tt, :] * cw_ref[j:j + 1, :]
    a, mult, gate_i = _lru_coeffs(xc, wa_ref, wi_ref, ba_ref, bi_ref, sp_ref)
    row = lax.broadcasted_iota(jnp.int32, (tt, 1), 0)
    mult = jnp.where((row == 0) & (i == 0), 1.0, mult)
    a_sc[...] = a
    b_sc[...] = mult * gate_i * xc

    def step(t, h):
        h = a_sc[pl.ds(t, 1), :] * h + b_sc[pl.ds(t, 1), :]
        b_sc[pl.ds(t, 1), :] = h
        return h

    h = lax.fori_loop(0, tt, step, h_sc[...], unroll=8)
    h_sc[...] = h
    hl_ref[...] = h
    rnn_ref[...] = (b_sc[...] * jax.nn.gelu(yr_ref[...])).astype(BF16)
    xe_sc[0:SUBLANES, :] = xe_sc[tt:tt + SUBLANES, :]


def _rglru_prompt(xr, yr, cw, cb, wa_bd, wi_bd, ba, bi, sp, batch, seq):
    tt = min(TIME_TILE, seq)
    nt = seq // tt
    row = pl.BlockSpec((tt, D_MODEL), lambda b, i: (b * nt + i, 0))
    nbd = D_MODEL // MXU_DIM
    return pl.pallas_call(
        _rglru_prompt_body,
        grid=(batch, nt),
        in_specs=[row, row,
                  pl.BlockSpec((CONV_W, D_MODEL), lambda b, i: (0, 0)),
                  pl.BlockSpec((1, D_MODEL), lambda b, i: (0, 0)),
                  pl.BlockSpec((nbd, MXU_DIM, MXU_DIM), lambda b, i: (0, 0, 0)),
                  pl.BlockSpec((nbd, MXU_DIM, MXU_DIM), lambda b, i: (0, 0, 0)),
                  pl.BlockSpec((1, D_MODEL), lambda b, i: (0, 0)),
                  pl.BlockSpec((1, D_MODEL), lambda b, i: (0, 0)),
                  pl.BlockSpec((1, D_MODEL), lambda b, i: (0, 0))],
        out_specs=[row, pl.BlockSpec((None, 1, D_MODEL), lambda b, i: (b, 0, 0))],
        out_shape=[jax.ShapeDtypeStruct((batch * seq, D_MODEL), BF16),
                   jax.ShapeDtypeStruct((batch, 1, D_MODEL), F32)],
        scratch_shapes=[pltpu.VMEM((tt + SUBLANES, D_MODEL), F32),
                        pltpu.VMEM((tt, D_MODEL), F32),
                        pltpu.VMEM((tt, D_MODEL), F32),
                        pltpu.VMEM((1, D_MODEL), F32)],
        compiler_params=_cparams(("arbitrary", "arbitrary")),
        name="rglru_prompt",
    )(xr, yr, cw, cb, wa_bd, wi_bd, ba, bi, sp)


def _rglru_sample_body(n_steps, first_is_start, xr_ref, yr_ref, buf_ref, h0_ref, cw_ref, cb_ref, wa_ref, wi_ref, ba_ref, bi_ref,
                       sp_ref, rnn_ref, hl_ref):
    nb = h0_ref.shape[0]
    taps = [buf_ref[j] for j in range(CONV_W - 1)] + [xr_ref[t * nb:(t + 1) * nb, :] for t in range(n_steps)]
    h = h0_ref[...]
    for t in range(n_steps):
        xc = cb_ref[...]
        for j in range(CONV_W):
            xc = xc + taps[t + j] * cw_ref[j:j + 1, :]
        a, mult, gate_i = _lru_coeffs(xc, wa_ref, wi_ref, ba_ref, bi_ref, sp_ref)
        if first_is_start and t == 0:
            mult = jnp.ones_like(mult)
        h = a * h + mult * gate_i * xc
        rnn_ref[t * nb:(t + 1) * nb, :] = (h * jax.nn.gelu(yr_ref[t * nb:(t + 1) * nb, :])).astype(BF16)
    hl_ref[...] = h


def _rglru_sample(xr, yr, buf, h0, cw, cb, wa_bd, wi_bd, ba, bi, sp, n_steps, first_is_start):
    t = xr.shape[0]
    return pl.pallas_call(
        functools.partial(_rglru_sample_body, n_steps, first_is_start),
        out_shape=[jax.ShapeDtypeStruct((t, D_MODEL), BF16),
                   jax.ShapeDtypeStruct(h0.shape, F32)],
        compiler_params=pltpu.CompilerParams(vmem_limit_bytes=VMEM_LIMIT_BYTES),
        name="rglru_sample",
    )(xr, yr, buf, h0, cw, cb, wa_bd, wi_bd, ba, bi, sp)


def _route(logits):
    lane = lax.broadcasted_iota(jnp.int32, logits.shape, 1).astype(F32)
    none = float(ROUTER_LANES)
    is_group = lane < N_GROUPS
    lg = jnp.where(is_group, logits, -jnp.inf)
    eg = jnp.exp(lg - jnp.max(lg, axis=-1, keepdims=True))
    gp = eg / jnp.sum(eg, axis=-1, keepdims=True)
    gw = jnp.max(gp, axis=-1, keepdims=True)
    gi = jnp.min(jnp.where(is_group & (gp == gw), lane, none), axis=-1, keepdims=True)
    first = N_GROUPS + gi * EXPERTS_PER_GROUP
    in_group = (lane >= first) & (lane < first + EXPERTS_PER_GROUP)
    le = jnp.where(in_group, logits, -jnp.inf)
    ee = jnp.exp(le - jnp.max(le, axis=-1, keepdims=True))
    pe = ee / jnp.sum(ee, axis=-1, keepdims=True)
    p1 = jnp.max(pe, axis=-1, keepdims=True)
    i1 = jnp.min(jnp.where(in_group & (pe == p1), lane, none), axis=-1, keepdims=True)
    rest = in_group & (lane != i1)
    pr = jnp.where(rest, pe, -1.0)
    p2 = jnp.max(pr, axis=-1, keepdims=True)
    i2 = jnp.min(jnp.where(rest & (pr == p2), lane, none), axis=-1, keepdims=True)
    den = p1 + p2
    w1 = gw * (p1 / den)
    w2 = gw * (p2 / den)
    rec = jnp.where(lane == 0.0, i1 - N_GROUPS, 0.0)
    rec = jnp.where(lane == 1.0, i2 - N_GROUPS, rec)
    rec = jnp.where(lane == 2.0, w1, rec)
    rec = jnp.where(lane == 3.0, w2, rec)
    return rec, jnp.where(lane == i1, 1.0, 0.0), jnp.where(lane == i2, 1.0, 0.0)


def _merge_body(x_ref, attn_ref, rnn_ref, ga_ref, gr_ref, g1_ref, sc2_ref, sh2_ref, n2g_ref,
                wpa_ref, wpr_ref, wo_ref, wrh_ref, wrl_ref, br_ref, ltri_ref, cnt_in_ref,
                x1_ref, u2_ref, rec_ref, cnt_ref, cnt_sc):
    pa = _dot(attn_ref[...], wpa_ref[...])
    pr = _dot(rnn_ref[...], wpr_ref[...])
    merged = jax.nn.sigmoid(ga_ref[...]) * pa + jax.nn.sigmoid(gr_ref[...]) * pr
    x1 = x_ref[...] + g1_ref[...] * _dot(merged.astype(BF16), wo_ref[...])
    x1_ref[...] = x1
    u2 = _rms_rows(x1, n2g_ref[...]) * (1.0 + sc2_ref[...]) + sh2_ref[...]
    u2_ref[...] = u2
    u_hi = u2.astype(BF16)
    u_lo = (u2 - u_hi.astype(F32)).astype(BF16)
    logits = (_dot(u_hi, wrh_ref[...]) + _dot(u_lo, wrh_ref[...]) + _dot(u_hi, wrl_ref[...])) + br_ref[...]
    rec, hot1, hot2 = _route(logits)

    @pl.when(pl.program_id(0) == 0)
    def _():
        cnt_sc[...] = cnt_in_ref[...]
    cnt = cnt_sc[...]
    tot1 = jnp.sum(hot1, axis=0, keepdims=True)
    tot2 = jnp.sum(hot2, axis=0, keepdims=True)
    before1 = _dot(ltri_ref[...], hot1.astype(BF16)) + cnt
    before2 = _dot(ltri_ref[...], hot2.astype(BF16)) + (cnt + tot1)
    rank1 = jnp.sum(hot1 * before1, axis=-1, keepdims=True)
    rank2 = jnp.sum(hot2 * before2, axis=-1, keepdims=True)
    cnt_sc[...] = cnt + tot1 + tot2
    cnt_ref[...] = cnt_sc[...]
    lane = lax.broadcasted_iota(jnp.int32, rec.shape, 1)
    rec = jnp.where(lane == 4, rank1, rec)
    rec_ref[...] = jnp.where(lane == 5, rank2, rec)


def _merge(x, attn, rnn, ga, gr, mods, rows_per_group, cnt_in, n2g, wpa, wpr, wo, wr_hi, wr_lo, br):
    t = x.shape[0]
    tile = min(TOKEN_TILE, t)
    row = lambda n: pl.BlockSpec((tile, n), lambda i: (i, 0))
    sq = _resident((D_MODEL, D_MODEL))
    r = jnp.arange(tile)
    ltri = (r[None, :] < r[:, None]).astype(BF16)
    return pl.pallas_call(
        _merge_body,
        grid=(t // tile,),
        in_specs=[row(D_MODEL), row(D_MODEL), row(D_MODEL), row(D_MODEL), row(D_MODEL),
                  _mod_spec(mods, tile, rows_per_group, 2),
                  _mod_spec(mods, tile, rows_per_group, 4),
                  _mod_spec(mods, tile, rows_per_group, 3),
                  _resident((1, D_MODEL)), sq, sq, sq,
                  _resident((D_MODEL, ROUTER_LANES)), _resident((D_MODEL, ROUTER_LANES)),
                  _resident((1, ROUTER_LANES)), _resident((tile, tile)), _resident((1, ROUTER_LANES))],
        out_specs=[row(D_MODEL), row(D_MODEL), row(ROUTER_LANES), _resident((1, ROUTER_LANES))],
        out_shape=[jax.ShapeDtypeStruct((t, D_MODEL), F32),
                   jax.ShapeDtypeStruct((t, D_MODEL), F32),
                   jax.ShapeDtypeStruct((t, ROUTER_LANES), F32),
                   jax.ShapeDtypeStruct((1, ROUTER_LANES), F32)],
        scratch_shapes=[pltpu.VMEM((1, ROUTER_LANES), F32)],
        compiler_params=_cparams(("arbitrary",)),
        name="merge",
    )(x, attn, rnn, ga, gr, mods, mods, mods, n2g, wpa, wpr, wo, wr_hi, wr_lo, br, ltri, cnt_in)


ROW_UNROLL = 8


def _for_rows(n_rows, fn):
    def body(j, carry):
        for s in range(ROW_UNROLL):
            fn(j * ROW_UNROLL + s)
        return carry
    lax.fori_loop(0, n_rows // ROW_UNROLL, body, 0)


def _dispatch_body(t_all, pos_ref, u_ref, init_hbm, us_hbm, sem):
    del init_hbm
    i = pl.program_id(0)
    tile = u_ref.shape[0]
    base = i * tile

    def row_copy(r, slot):
        return pltpu.make_async_copy(u_ref.at[pl.ds(r, 1)],
                                     us_hbm.at[pl.ds(pos_ref[slot * t_all + base + r], 1)], sem.at[0])

    def issue(r):
        row_copy(r, 0).start()
        row_copy(r, 1).start()

    def wait_rows(n):
        for _ in range(2):
            pltpu.make_async_copy(u_ref.at[pl.ds(0, n)], us_hbm.at[pl.ds(0, n)], sem.at[0]).wait()

    n_full = t_all // tile
    rem = t_all - n_full * tile

    @pl.when(i < n_full)
    def _():
        _for_rows(tile, issue)
        wait_rows(tile)

    if rem:
        @pl.when(i == n_full)
        def _():
            _for_rows(rem, issue)
            wait_rows(rem)


def _dispatch(pos_flat, u_all, n_sorted_rows):
    t_all = u_all.shape[0]
    tile = TOKEN_TILE
    assert (t_all % tile) % ROW_UNROLL == 0
    grid_spec = pltpu.PrefetchScalarGridSpec(
        num_scalar_prefetch=1,
        grid=(pl.cdiv(t_all, tile),),
        in_specs=[pl.BlockSpec((tile, D_MODEL), lambda i, pos: (i, 0)),
                  pl.BlockSpec(memory_space=pl.ANY)],
        out_specs=pl.BlockSpec(memory_space=pl.ANY),
        scratch_shapes=[pltpu.SemaphoreType.DMA((1,))],
    )
    return pl.pallas_call(
        functools.partial(_dispatch_body, t_all),
        grid_spec=grid_spec,
        out_shape=jax.ShapeDtypeStruct((n_sorted_rows, D_MODEL), F32),
        input_output_aliases={2: 0},
        compiler_params=_cparams(("arbitrary",)),
        name="dispatch",
    )(pos_flat, u_all, jnp.zeros((n_sorted_rows, D_MODEL), F32))


def _experts_body(te_ref, nv_ref, last_ref, u_ref, wg_ref, wu_ref, wd_ref, y_ref, wg_sc, wu_sc, wd_sc):
    del last_ref
    i = pl.program_id(0)

    @pl.when((i == 0) | (te_ref[i] != te_ref[jnp.maximum(i - 1, 0)]))
    def _():
        wg_sc[...] = wg_ref[...].astype(BF16)
        wu_sc[...] = wu_ref[...].astype(BF16)
        wd_sc[...] = wd_ref[...].astype(BF16)

    @pl.when(nv_ref[i] > 0)
    def _():
        u = u_ref[...].astype(BF16)
        hg = _dot(u, wg_sc[...])
        hid = (hg * jax.nn.sigmoid(hg)) * _dot(u, wu_sc[...])
        y_ref[...] = _dot(hid.astype(BF16), wd_sc[...])

    @pl.when(nv_ref[i] == 0)
    def _():
        y_ref[...] = jnp.zeros(y_ref.shape, F32)


def _experts(tile_expert, tile_valid, last_tile, u_sorted, w_gate, w_up, w_down):
    tile = EXPERT_TILE
    n_tiles = tile_expert.shape[0]
    wspec = lambda a, b: pl.BlockSpec((None, a, b), lambda i, te, nv, last: (te[i], 0, 0))
    grid_spec = pltpu.PrefetchScalarGridSpec(
        num_scalar_prefetch=3,
        grid=(n_tiles,),
        in_specs=[pl.BlockSpec((tile, D_MODEL), lambda i, te, nv, last: (jnp.minimum(i, last[0]), 0)),
                  wspec(D_MODEL, D_EXPERT), wspec(D_MODEL, D_EXPERT), wspec(D_EXPERT, D_MODEL)],
        out_specs=pl.BlockSpec((tile, D_MODEL), lambda i, te, nv, last: (i, 0)),
        scratch_shapes=[pltpu.VMEM((D_MODEL, D_EXPERT), BF16),
                        pltpu.VMEM((D_MODEL, D_EXPERT), BF16),
                        pltpu.VMEM((D_EXPERT, D_MODEL), BF16)],
    )
    return pl.pallas_call(
        _experts_body,
        grid_spec=grid_spec,
        out_shape=jax.ShapeDtypeStruct(u_sorted.shape, F32),
        compiler_params=_cparams(("arbitrary",)),
        name="experts",
    )(tile_expert, tile_valid, last_tile, u_sorted, w_gate, w_up, w_down)


def _dispatch_plan(rec_all, counts_row):
    t_all = rec_all.shape[0]
    tile = EXPERT_TILE
    n_tiles = -(-2 * t_all // tile) + N_EXPERTS
    counts = counts_row[0, N_GROUPS:N_GROUPS + N_EXPERTS].astype(jnp.int32)
    tiles_per_e = (counts + tile - 1) // tile
    tile_end = jnp.cumsum(tiles_per_e)
    tile_start = tile_end - tiles_per_e
    experts = jnp.arange(N_EXPERTS, dtype=jnp.int32)
    e = rec_all[:, 0:2].astype(jnp.int32)
    rank = rec_all[:, 4:6].astype(jnp.int32)
    row_start = jnp.sum(jnp.where(e[:, :, None] == experts, tile_start * tile, 0), axis=-1)
    pos_flat = (row_start + rank).T.reshape(-1)
    n_used = tile_end[-1]
    tile_ids = jnp.arange(n_tiles, dtype=jnp.int32)
    clamped = jnp.minimum(tile_ids, n_used - 1)
    te = jnp.sum((clamped[:, None] >= tile_end[None, :]).astype(jnp.int32), axis=-1)
    te = jnp.minimum(te, N_EXPERTS - 1)
    first = jnp.sum(jnp.where(te[:, None] == experts, tile_start, 0), axis=-1)
    cnt_t = jnp.sum(jnp.where(te[:, None] == experts, counts, 0), axis=-1)
    valid = jnp.clip(cnt_t - (tile_ids - first) * tile, 0, tile)
    valid = jnp.where(tile_ids < n_used, valid, 0).astype(jnp.int32)
    return pos_flat, te, valid, (n_used - 1).reshape(1).astype(jnp.int32), n_tiles * tile


def _combine_body(t_all, row_offset, pos_ref, x_ref, rec_ref, g2_ref, ys_hbm, o_ref, ybuf, sem):
    i = pl.program_id(0)
    n = pl.num_programs(0)
    tile = x_ref.shape[0]
    slot = i % 2

    def start_gather(tile_idx, buf):
        base = row_offset + tile_idx * tile

        def issue(r):
            for k in range(2):
                pltpu.make_async_copy(ys_hbm.at[pl.ds(pos_ref[k * t_all + base + r], 1)],
                                      ybuf.at[buf, k, pl.ds(r, 1)], sem.at[buf]).start()
        _for_rows(tile, issue)

    @pl.when(i == 0)
    def _():
        start_gather(0, 0)

    @pl.when(i + 1 < n)
    def _():
        start_gather(i + 1, 1 - slot)

    for k in range(2):
        pltpu.make_async_copy(ys_hbm.at[pl.ds(0, tile)], ybuf.at[slot, k], sem.at[slot]).wait()
    w1 = rec_ref[:, 2:3]
    w2 = rec_ref[:, 3:4]
    o_ref[...] = x_ref[...] + g2_ref[...] * (w1 * ybuf[slot, 0] + w2 * ybuf[slot, 1])


def _combine(pos_flat, x1, rec, y_sorted, mods, rows_per_group, row_offset, t_all):
    t = x1.shape[0]
    tile = min(TOKEN_TILE, t)
    assert tile % ROW_UNROLL == 0
    if mods.ndim == 3:
        tiles_per_group = rows_per_group // tile
        g2_spec = pl.BlockSpec((None, 1, D_MODEL), lambda i, pos: (i // tiles_per_group, 0, 5))
    else:
        g2_spec = pl.BlockSpec((tile, D_MODEL), lambda i, pos: (i, 5))
    grid_spec = pltpu.PrefetchScalarGridSpec(
        num_scalar_prefetch=1,
        grid=(t // tile,),
        in_specs=[pl.BlockSpec((tile, D_MODEL), lambda i, pos: (i, 0)),
                  pl.BlockSpec((tile, ROUTER_LANES), lambda i, pos: (i, 0)),
                  g2_spec,
                  pl.BlockSpec(memory_space=pl.ANY)],
        out_specs=pl.BlockSpec((tile, D_MODEL), lambda i, pos: (i, 0)),
        scratch_shapes=[pltpu.VMEM((2, 2, tile, D_MODEL), F32),
                        pltpu.SemaphoreType.DMA((2,))],
    )
    return pl.pallas_call(
        functools.partial(_combine_body, t_all, row_offset),
        grid_spec=grid_spec,
        out_shape=jax.ShapeDtypeStruct((t, D_MODEL), F32),
        compiler_params=_cparams(("arbitrary",)),
        name="combine",
    )(pos_flat, x1, rec, mods, y_sorted)


def _block_diag(w):
    per = MXU_DIM // RNN_BLOCK
    w4 = w.reshape(N_RNN_BLOCKS // per, per, RNN_BLOCK, RNN_BLOCK)
    eye = jnp.eye(per, dtype=w.dtype)
    return jnp.einsum("jmcd,mn->jmcnd", w4, eye).reshape(N_RNN_BLOCKS // per, MXU_DIM, MXU_DIM)


def kernel(x_prompt, x_sample, cache_k, cache_v, state_rnn, state_conv, page_table, c_prompt, c_sample, rel_bias, w_mod, b_mod, norm1_g, norm2_g, w_in, q_norm_g, k_norm_g, lam_q1, lam_k1, lam_q2, lam_k2, subln_g, conv_w, conv_b, lru_wa, lru_ba, lru_wi, lru_bi, lru_lambda, w_pa, w_pr, w_o, w_rg, b_rg, w_re, b_re, w_e_gate, w_e_up, w_e_down):
    bp, tp, _ = x_prompt.shape
    bs, ts, _ = x_sample.shape
    depth = w_in.shape[0]
    n_pool = cache_k.shape[1]
    n_pages = page_table.shape[1]
    past_len = n_pages * PAGE_SIZE
    t_p = bp * tp
    t_s = bs * ts
    t_all = t_p + t_s
    blk = ATTN_BLOCK
    nblk = tp // blk
    assert tp % blk == 0 and tp % TIME_TILE == 0 and t_p % TOKEN_TILE == 0
    assert t_s % SUBLANES == 0 and t_p % min(TOKEN_TILE, t_s) == 0
    log2e = math.log2(math.e)

    rows_c = -(-(bp + bs) // SUBLANES) * SUBLANES
    c_all = jnp.zeros((rows_c, D_MODEL), F32).at[:bp].set(c_prompt).at[bp:bp + bs].set(c_sample)
    mods = _mods(c_all, w_mod, b_mod)
    mods_p = mods[:, :bp].reshape(depth, bp, 1, 6 * D_MODEL)
    mods_s = jnp.tile(mods[:, bp:bp + bs], (1, ts, 1))

    w_in_bf = w_in.astype(BF16)
    w_pa_bf, w_pr_bf, w_o_bf = w_pa.astype(BF16), w_pr.astype(BF16), w_o.astype(BF16)
    qg_row = jnp.tile(q_norm_g, (1, Q_COLS // HEAD_DIM)).reshape(depth, 1, Q_COLS)
    kg_row = jnp.tile(k_norm_g, (1, K_COLS // HEAD_DIM)).reshape(depth, 1, K_COLS)
    seg = jnp.arange(MXU_DIM) // HEAD_DIM
    chunk_p = ((seg[:, None] == seg[None, :]).astype(F32) / HEAD_DIM).astype(BF16)
    wr = jnp.concatenate([w_rg, w_re], axis=-1)
    wr = jnp.pad(wr, ((0, 0), (0, 0), (0, ROUTER_LANES - wr.shape[-1])))
    wr_hi = wr.astype(BF16)
    wr_lo = (wr - wr_hi.astype(F32)).astype(BF16)
    br = jnp.pad(jnp.concatenate([b_rg, b_re], axis=-1), ((0, 0), (0, ROUTER_LANES - N_GROUPS - N_EXPERTS)))
    br = br.reshape(depth, 1, ROUTER_LANES)
    wa_bd = jax.vmap(_block_diag)(lru_wa).astype(BF16)
    wi_bd = jax.vmap(_block_diag)(lru_wi).astype(BF16)
    ba = lru_ba.reshape(depth, 1, D_MODEL)
    bi = lru_bi.reshape(depth, 1, D_MODEL)
    sp = jax.nn.softplus(-lru_lambda.astype(F32)).reshape(depth, 1, D_MODEL)
    cb = conv_b.reshape(depth, 1, D_MODEL)
    lam_f = lambda a, b: jnp.exp(jnp.sum(a.astype(F32) * b.astype(F32), axis=-1))
    lam_base = lam_f(lam_q1, lam_k1) - lam_f(lam_q2, lam_k2)

    kq = jnp.arange(blk, dtype=jnp.int32)
    n_diag = kq[None, :] - kq[:, None]
    bias_diag = _bias_of_distance(rel_bias, n_diag) * log2e
    bias_prev = _bias_of_distance(rel_bias, n_diag + blk) * log2e
    rb = rel_bias.astype(F32) * log2e
    bias_scal = jnp.concatenate([rb[NUM_BUCKETS - 1],
                                 jnp.max(rb, axis=0), jnp.max(rb, axis=0) - jnp.min(rb, axis=0)])

    page_rows = PAGE_SIZE * N_KV_HEADS
    n_new = -(-ts * N_KV_HEADS // (2 * SUBLANES)) * (2 * SUBLANES)
    keys_s = n_pages * page_rows + -(-n_new // LANES) * LANES
    n_pos = past_len + ts
    dist = jnp.arange(-(ts - 1), n_pos, dtype=jnp.int32)
    flipped = _bias_of_distance(rel_bias, dist)[:, ::-1]
    by_token = jnp.stack([flipped[:, ts - 1 - t:ts - 1 - t + n_pos] for t in range(ts)], axis=1)
    by_row = jnp.transpose(by_token.reshape(N_KV_HEADS, GQA_GROUP, ts, n_pos), (0, 2, 1, 3))
    own_head = (jnp.arange(N_KV_HEADS)[:, None, None, None, None]
                == jnp.arange(N_KV_HEADS)[None, None, None, None, :])
    bias_rows = jnp.where(own_head, by_row[..., None], MASK_VALUE)
    bias_rows = bias_rows.reshape(N_KV_HEADS * ts * GQA_GROUP, n_pos * N_KV_HEADS)
    bias_rows = jnp.pad(bias_rows, ((0, 0), (0, keys_s - n_pos * N_KV_HEADS)), constant_values=MASK_VALUE)
    bias_s = jnp.concatenate([bias_rows, bias_rows], axis=0)
    pt_flat = page_table.reshape(-1).astype(jnp.int32)
    cache_k_rows = cache_k.reshape(depth, n_pool, page_rows, LANES)
    cache_v_rows = cache_v.reshape(depth, n_pool, page_rows, LANES)

    xp = x_prompt.reshape(t_p, D_MODEL)
    xs = jnp.swapaxes(x_sample, 0, 1).reshape(t_s, D_MODEL)
    outs = [[] for _ in range(8)]
    for l in range(depth):
        lam_init = 0.8 - 0.6 * math.exp(-0.3 * l)
        lam = lam_base[l] + lam_init
        sg_col = subln_g[l].astype(F32).reshape(V_DIM, 1)
        sg_row = subln_g[l].astype(F32).reshape(1, V_DIM)
        scal_s = jnp.stack([lam, jnp.asarray(1.0 - lam_init, F32)]).astype(F32)
        scal_p = jnp.concatenate([scal_s, bias_scal])
        lp = (norm1_g[l].reshape(1, D_MODEL), w_in_bf[l], qg_row[l], kg_row[l], chunk_p)
        rg = (conv_w[l], cb[l], wa_bd[l], wi_bd[l], ba[l], bi[l], sp[l])
        mg = (norm2_g[l].reshape(1, D_MODEL), w_pa_bf[l], w_pr_bf[l], w_o_bf[l], wr_hi[l], wr_lo[l], br[l])

        q, k, kb, v, xr, yr, ga, gr = _in_proj(xp, mods_p[l], tp, ATTN_SCALE * log2e, *lp)
        vt = jnp.swapaxes(v.reshape(bp, nblk, blk, N_KV_HEADS, V_DIM).astype(BF16), 2, 4)
        vt = jnp.transpose(vt, (0, 3, 1, 2, 4)).reshape(bp * N_KV_HEADS, nblk, V_DIM, blk)
        ones = jnp.zeros((bp * N_KV_HEADS, nblk, ONES_ROWS, blk), BF16).at[:, :, 0].set(1.0)
        vt_ext = jnp.concatenate([vt, ones], axis=2)
        attn = _attn_prompt(scal_p, q, kb, vt_ext, bias_diag, bias_prev, sg_col, bp, tp)
        rnn, h_last = _rglru_prompt(xr, yr, *rg, bp, tp)
        x1p, u2p, recp, cnt = _merge(xp, attn, rnn, ga, gr, mods_p[l], tp,
                                     jnp.zeros((1, ROUTER_LANES), F32), *mg)
        outs[0].append(k.reshape(bp, tp, N_KV_HEADS, 2 * HEAD_DIM))
        outs[1].append(v.reshape(bp, tp, N_KV_HEADS, V_DIM))
        outs[2].append(h_last.reshape(bp, D_MODEL))
        outs[3].append(xr.reshape(bp, tp, D_MODEL)[:, tp - (CONV_W - 1):])

        q, k, kb, v, xr, yr, ga, gr = _in_proj(xs, mods_s[l], ts, ATTN_SCALE, *lp)
        q6 = q.reshape(ts, bs, N_KV_HEADS, GQA_GROUP, 2, HEAD_DIM)
        q6 = jnp.transpose(q6, (1, 4, 2, 0, 3, 5))
        zq = jnp.zeros_like(q6[:, 0])
        q_rows = jnp.stack([jnp.concatenate([q6[:, 0], zq], axis=-1),
                            jnp.concatenate([zq, q6[:, 1]], axis=-1)], axis=1)
        q_rows = q_rows.reshape(bs, 2 * N_KV_HEADS * ts * GQA_GROUP, 2 * HEAD_DIM)
        new_rows = lambda z: jnp.pad(jnp.swapaxes(z.reshape(ts, bs, N_KV_HEADS, LANES), 0, 1)
                                     .reshape(bs, ts * N_KV_HEADS, LANES),
                                     ((0, 0), (0, n_new - ts * N_KV_HEADS), (0, 0)))
        o_s = _attn_sample(pt_flat, jnp.full((1,), l, jnp.int32), scal_s, q_rows, new_rows(kb),
                           new_rows(v.astype(BF16)), bias_s, sg_row, cache_k_rows, cache_v_rows, n_pages)
        attn = jnp.transpose(o_s.reshape(bs, N_KV_HEADS, ts, GQA_GROUP, V_DIM), (2, 0, 1, 3, 4))
        attn = attn.reshape(t_s, N_HEADS * V_DIM).astype(BF16)
        buf = jnp.swapaxes(state_conv[l], 0, 1)
        rnn, h_last = _rglru_sample(xr, yr, buf, state_rnn[l], *rg, ts, past_len == 0)
        x1s, u2s, recs, cnt = _merge(xs, attn, rnn, ga, gr, mods_s[l], ts, cnt, *mg)
        to_bt = lambda z, n: jnp.swapaxes(z.reshape(ts, bs, n), 0, 1)
        outs[4].append(to_bt(k, K_COLS).reshape(bs, ts, N_KV_HEADS, 2 * HEAD_DIM))
        outs[5].append(to_bt(v, V_COLS).reshape(bs, ts, N_KV_HEADS, V_DIM))
        outs[6].append(h_last)
        xin = jnp.concatenate([state_conv[l], to_bt(xr, D_MODEL)], axis=1)
        outs[7].append(xin[:, -(CONV_W - 1):])

        pos_flat, te, valid, last_tile, n_sorted = _dispatch_plan(jnp.concatenate([recp, recs], axis=0), cnt)
        u_sorted = _dispatch(pos_flat, jnp.concatenate([u2p, u2s], axis=0), n_sorted)
        y_sorted = _experts(te, valid, last_tile, u_sorted, w_e_gate[l], w_e_up[l], w_e_down[l])
        xp = _combine(pos_flat, x1p, recp, y_sorted, mods_p[l], tp, 0, t_all)
        xs = _combine(pos_flat, x1s, recs, y_sorted, mods_s[l], ts, t_p, t_all)

    y_sample = jnp.swapaxes(xs.reshape(ts, bs, D_MODEL), 0, 1)
    stack = lambda i: jnp.stack(outs[i])
    return (xp.reshape(bp, tp, D_MODEL), y_sample, stack(0), stack(1), stack(2), stack(3),
            stack(4), stack(5), stack(6), stack(7))
```

```python
import functools
import math

import jax
import jax.numpy as jnp
from jax import lax
from jax.experimental import pallas as pl
from jax.experimental.pallas import tpu as pltpu

F32 = jnp.float32
BF16 = jnp.bfloat16

D_MODEL = 1024
HEAD_DIM = 64
N_HEADS = 8
N_KV_HEADS = 4
GQA_GROUP = 2
V_DIM = 128
ATTN_SCALE = HEAD_DIM ** -0.5
MASK_VALUE = -1e30
NUM_BUCKETS = 32
MAX_EXACT = 16
MAX_DISTANCE = 128
N_RNN_BLOCKS = 16
RNN_BLOCK = 64
CONV_W = 4
LRU_C = 8.0
N_GROUPS = 4
EXPERTS_PER_GROUP = 8
N_EXPERTS = 32
D_EXPERT = 512
NORM_EPS = 1e-6
PAGE_SIZE = 128
Q_COLS = 1024
K_COLS = 512
V_COLS = 512
D_IN = 6144

LANES = 128
SUBLANES = 8
MXU_DIM = 256
VMEM_LIMIT_BYTES = 56 * 1024 * 1024

TOKEN_TILE = 256
ATTN_BLOCK = 512
TIME_TILE = 256
EXPERT_TILE = 256
ROUTER_LANES = 128
ONES_ROWS = 16

_NT = (((1,), (1,)), ((), ()))


def _cparams(sem):
    return pltpu.CompilerParams(dimension_semantics=sem, vmem_limit_bytes=VMEM_LIMIT_BYTES)


def _dot(a, b):
    return jnp.dot(a, b, preferred_element_type=F32)


def _resident(shape):
    n = len(shape)
    return pl.BlockSpec(shape, lambda *_: (0,) * n)


_MODS_COLS = 1536


def _mods_body(c_ref, w_ref, b_ref, o_ref):
    c = c_ref[...]
    s = c * jax.nn.sigmoid(c)
    o_ref[...] = jnp.dot(s, w_ref[...], precision=lax.Precision.HIGHEST,
                         preferred_element_type=F32) + b_ref[...]


def _mods(c_all, w_mod, b_mod):
    rows = c_all.shape[0]
    depth, _, n = w_mod.shape
    return pl.pallas_call(
        _mods_body,
        grid=(depth, n // _MODS_COLS),
        in_specs=[pl.BlockSpec((rows, D_MODEL), lambda l, j: (0, 0)),
                  pl.BlockSpec((None, D_MODEL, _MODS_COLS), lambda l, j: (l, 0, j)),
                  pl.BlockSpec((None, 1, _MODS_COLS), lambda l, j: (l, 0, j))],
        out_specs=pl.BlockSpec((None, rows, _MODS_COLS), lambda l, j: (l, 0, j)),
        out_shape=jax.ShapeDtypeStruct((depth, rows, n), F32),
        compiler_params=_cparams(("arbitrary", "arbitrary")),
        name="mods",
    )(c_all, w_mod, b_mod.reshape(depth, 1, n))


def _rms_rows(x, g):
    ms = jnp.mean(x * x, axis=-1, keepdims=True)
    return x * lax.rsqrt(ms + NORM_EPS) * g


def _chunk_mean_square(z, p):
    parts = []
    for j in range(z.shape[1] // MXU_DIM):
        zz = z[:, j * MXU_DIM:(j + 1) * MXU_DIM]
        parts.append(_dot((zz * zz).astype(BF16), p))
    return jnp.concatenate(parts, axis=1)


def _in_proj_body(q_scale, x_ref, sc_ref, sh_ref, g_ref, w_ref, qg_ref, kg_ref, p_ref,
                  q_ref, k_ref, kb_ref, v_ref, xr_ref, yr_ref, ga_ref, gr_ref):
    u = (_rms_rows(x_ref[...], g_ref[...]) * (1.0 + sc_ref[...]) + sh_ref[...]).astype(BF16)
    p = p_ref[...]
    zq = _dot(u, w_ref[:, 0:Q_COLS])
    q = zq * lax.rsqrt(_chunk_mean_square(zq, p) + NORM_EPS) * qg_ref[...]
    q_ref[...] = (q * q_scale).astype(BF16)
    zk = _dot(u, w_ref[:, Q_COLS:Q_COLS + K_COLS])
    k = zk * lax.rsqrt(_chunk_mean_square(zk, p) + NORM_EPS) * kg_ref[...]
    k_ref[...] = k
    kb_ref[...] = k.astype(BF16)
    c0 = Q_COLS + K_COLS
    v_ref[...] = _dot(u, w_ref[:, c0:c0 + V_COLS])
    c0 += V_COLS
    xr_ref[...] = _dot(u, w_ref[:, c0:c0 + D_MODEL])
    c0 += D_MODEL
    yr_ref[...] = _dot(u, w_ref[:, c0:c0 + D_MODEL])
    c0 += D_MODEL
    ga_ref[...] = _dot(u, w_ref[:, c0:c0 + D_MODEL])
    c0 += D_MODEL
    gr_ref[...] = _dot(u, w_ref[:, c0:c0 + D_MODEL])


def _mod_spec(mods, tile, rows_per_group, col_block):
    if mods.ndim == 3:
        tiles_per_group = rows_per_group // tile
        return pl.BlockSpec((None, 1, D_MODEL), lambda i: (i // tiles_per_group, 0, col_block))
    return pl.BlockSpec((tile, D_MODEL), lambda i: (i, col_block))


def _in_proj(x, mods, rows_per_group, q_scale, norm_g, w_in_bf, qg_row, kg_row, chunk_p):
    t = x.shape[0]
    tile = min(TOKEN_TILE, t)
    row = lambda n: pl.BlockSpec((tile, n), lambda i: (i, 0))
    outs = [(Q_COLS, BF16), (K_COLS, F32), (K_COLS, BF16), (V_COLS, F32),
            (D_MODEL, F32), (D_MODEL, F32), (D_MODEL, F32), (D_MODEL, F32)]
    return pl.pallas_call(
        functools.partial(_in_proj_body, q_scale),
        grid=(t // tile,),
        in_specs=[row(D_MODEL),
                  _mod_spec(mods, tile, rows_per_group, 1),
                  _mod_spec(mods, tile, rows_per_group, 0),
                  _resident((1, D_MODEL)),
                  _resident((D_MODEL, D_IN)),
                  _resident((1, Q_COLS)),
                  _resident((1, K_COLS)),
                  _resident((MXU_DIM, MXU_DIM))],
        out_specs=[row(n) for n, _ in outs],
        out_shape=[jax.ShapeDtypeStruct((t, n), dt) for n, dt in outs],
        compiler_params=_cparams(("arbitrary",)),
        name="in_proj",
    )(x, mods, mods, norm_g, w_in_bf, qg_row, kg_row, chunk_p)


def _bucket_of_distance(n):
    n = jnp.maximum(n, 0)
    nf = jnp.maximum(n, 1).astype(F32)
    large = MAX_EXACT + (jnp.log(nf / MAX_EXACT) / math.log(MAX_DISTANCE / MAX_EXACT)
                         * (NUM_BUCKETS - MAX_EXACT)).astype(jnp.int32)
    return jnp.where(n < MAX_EXACT, n, jnp.minimum(large, NUM_BUCKETS - 1))


def _bias_of_distance(rel_bias, n):
    b = rel_bias.astype(F32)[_bucket_of_distance(n)]
    b = jnp.where((n >= 0)[..., None], b, MASK_VALUE)
    return jnp.moveaxis(b, -1, 0)


def _bias_tile(rel_bias, blk, offset):
    n_dist = 2 * blk - 1
    v = _bias_of_distance(rel_bias, jnp.arange(n_dist, dtype=jnp.int32) - (blk - 1) + offset)
    flat = jnp.tile(v, (1, blk + 1))[:, :blk * (n_dist + 1)]
    m = flat.reshape(v.shape[0], blk, n_dist + 1)[:, :, :blk]
    return m[:, ::-1, :]


_S_LAM, _S_OUT = 0, 1
_S_FAR = 2
_S_BMAX = _S_FAR + N_HEADS
_S_BRANGE = _S_BMAX + N_HEADS
BOUND_MARGIN = 1.01
SAFE_EXPONENT_SPAN = 100.0


def _attn_prompt_body(scal_ref, q_ref, k_ref, vt_ref, bdiag_ref, bprev_ref, sg_ref,
                      o_ref, qpad_sc, kmax_sc, bq_sc, m_sc, acc_sc):
    h = pl.program_id(1)
    qi = pl.program_id(2)
    blk = q_ref.shape[0]
    ncombo = 2 * GQA_GROUP
    lam = scal_ref[_S_LAM]
    out_scale = scal_ref[_S_OUT]

    @pl.when(qi == 0)
    def _():
        kk = k_ref[...].astype(F32)
        r = lax.broadcasted_iota(jnp.int32, (LANES, LANES), 0) // HEAD_DIM
        c = lax.broadcasted_iota(jnp.int32, (LANES, LANES), 1) // HEAD_DIM
        same = jnp.where(r == c, 1.0, 0.0).astype(BF16)
        ksq = _dot((kk * kk).astype(BF16), same)
        kmax_sc[...] = jnp.sqrt(jnp.max(ksq, axis=0, keepdims=True))

    lane = lax.broadcasted_iota(jnp.int32, (blk, LANES), 1)
    ones = jnp.ones((SUBLANES, LANES), BF16)
    span = jnp.zeros((1, 1), F32)
    for g in range(GQA_GROUP):
        q2 = q_ref[:, g * LANES:(g + 1) * LANES]
        for c in range(2):
            idx = 2 * g + c
            qp = jnp.where((lane < HEAD_DIM) == (c == 0), q2, jnp.zeros_like(q2))
            qpad_sc[idx] = qp
            qf = qp.astype(F32)
            qsq = lax.dot_general(ones, (qf * qf).astype(BF16), _NT, preferred_element_type=F32)[0:1]
            kmax = jnp.max(kmax_sc[:, c * HEAD_DIM:(c + 1) * HEAD_DIM], axis=1, keepdims=True)
            bqk = BOUND_MARGIN * jnp.sqrt(qsq) * kmax
            bq_sc[idx] = bqk + scal_ref[_S_BMAX + h * GQA_GROUP + g]
            span = jnp.maximum(span, 2.0 * jnp.max(bqk, axis=1, keepdims=True)
                               + scal_ref[_S_BRANGE + h * GQA_GROUP + g])
    acc_sc[...] = jnp.zeros(acc_sc.shape, F32)
    far_bias = lambda g: scal_ref[_S_FAR + h * GQA_GROUP + g]
    n_far = jnp.maximum(qi - 1, 0)

    def k_block(kj):
        return k_ref[pl.ds(pl.multiple_of(kj * blk, blk), blk), :]

    def fast_step(kj, shift_of_group, tile_of_group):
        k_blk = k_block(kj)
        vt = vt_ref[kj]
        qk = lambda idx: lax.dot_general(k_blk, qpad_sc[idx], _NT, preferred_element_type=F32)
        st_next = qk(0)
        for idx in range(ncombo):
            st = st_next
            if idx + 1 < ncombo:
                st_next = qk(idx + 1)
            g = idx // 2
            if tile_of_group is not None:
                st = st + tile_of_group(g)
            bound = bq_sc[idx] if shift_of_group is None else bq_sc[idx] - shift_of_group(g)
            acc_sc[idx] += _dot(vt, jnp.exp2(st - bound).astype(BF16))

    def exact_step(kj, shift_of_group, tile_of_group):
        k_blk = k_block(kj)
        vt = vt_ref[kj]
        for idx in range(ncombo):
            g = idx // 2
            st = lax.dot_general(k_blk, qpad_sc[idx], _NT, preferred_element_type=F32)
            if tile_of_group is not None:
                st = st + tile_of_group(g)
            shift = 0.0 if shift_of_group is None else shift_of_group(g)
            m_old = m_sc[idx]
            m_new = jnp.maximum(m_old, jnp.max(st, axis=0, keepdims=True) + shift)
            p = jnp.exp2(st - (m_new - shift)).astype(BF16)
            acc_sc[idx] = jnp.exp2(m_old - m_new) * acc_sc[idx] + _dot(vt, p)
            m_sc[idx] = m_new

    def run(step):
        def far_step(kj, carry):
            step(kj, far_bias, None)
            return carry
        lax.fori_loop(0, n_far, far_step, 0)

        @pl.when(qi >= 1)
        def _():
            step(qi - 1, None, lambda g: bprev_ref[g])
        step(qi, None, lambda g: bdiag_ref[g])

    def run_exact():
        m_sc[...] = jnp.full(m_sc.shape, -jnp.inf, F32)
        run(exact_step)

    lax.cond(span[0, 0] <= SAFE_EXPONENT_SPAN, lambda: run(fast_step), run_exact)

    for g in range(GQA_GROUP):
        a0 = acc_sc[2 * g]
        a1 = acc_sc[2 * g + 1]
        o = a0[:V_DIM] / a0[V_DIM:V_DIM + 1] - lam * (a1[:V_DIM] / a1[V_DIM:V_DIM + 1])
        o = o * lax.rsqrt(jnp.mean(o * o, axis=0, keepdims=True) + NORM_EPS)
        o = o * sg_ref[...] * out_scale
        o_ref[:, g * V_DIM:(g + 1) * V_DIM] = o.T.astype(BF16)


def _attn_prompt(scal, q_bf, k_bf, vt_ext, bias_diag, bias_prev, sg_col, batch, seq):
    blk = ATTN_BLOCK
    nblk = seq // blk
    vrows = V_DIM + ONES_ROWS
    ncombo = 2 * GQA_GROUP
    grid_spec = pltpu.PrefetchScalarGridSpec(
        num_scalar_prefetch=1,
        grid=(batch, N_KV_HEADS, nblk),
        in_specs=[pl.BlockSpec((blk, GQA_GROUP * 2 * HEAD_DIM), lambda b, h, i, s: (b * nblk + i, h)),
                  pl.BlockSpec((seq, 2 * HEAD_DIM), lambda b, h, i, s: (b, h)),
                  pl.BlockSpec((None, nblk, vrows, blk), lambda b, h, i, s: (b * N_KV_HEADS + h, 0, 0, 0)),
                  pl.BlockSpec((GQA_GROUP, blk, blk), lambda b, h, i, s: (h, 0, 0)),
                  pl.BlockSpec((GQA_GROUP, blk, blk), lambda b, h, i, s: (h, 0, 0)),
                  pl.BlockSpec((V_DIM, 1), lambda b, h, i, s: (0, 0))],
        out_specs=pl.BlockSpec((blk, GQA_GROUP * V_DIM), lambda b, h, i, s: (b * nblk + i, h)),
        scratch_shapes=[pltpu.VMEM((ncombo, blk, LANES), BF16),
                        pltpu.VMEM((1, LANES), F32),
                        pltpu.VMEM((ncombo, 1, blk), F32),
                        pltpu.VMEM((ncombo, 1, blk), F32),
                        pltpu.VMEM((ncombo, vrows, blk), F32)],
    )
    return pl.pallas_call(
        _attn_prompt_body,
        grid_spec=grid_spec,
        out_shape=jax.ShapeDtypeStruct((batch * seq, N_HEADS * V_DIM), BF16),
        compiler_params=_cparams(("arbitrary", "arbitrary", "arbitrary")),
        name="attn_prompt",
    )(scal, q_bf, k_bf, vt_ext, bias_diag, bias_prev, sg_col)


def _attn_sample_body(n_pages, pt_ref, layer_ref, scal_ref, q_ref, knew_ref, vnew_ref, bias_ref, sg_ref, *refs):
    k_pages = refs[:n_pages]
    v_pages = refs[n_pages:2 * n_pages]
    o_ref, kb_sc, vb_sc = refs[2 * n_pages:]
    page_rows = k_pages[0].shape[2]
    past = n_pages * page_rows
    n_new = knew_ref.shape[1]
    lam = scal_ref[0]
    out_scale = scal_ref[1]
    half = q_ref.shape[1] // 2

    @pl.when(pl.program_id(0) == 0)
    def _():
        tail = kb_sc.shape[0] - past - n_new
        kb_sc[past + n_new:, :] = jnp.zeros((tail, LANES), BF16)
        vb_sc[past + n_new:, :] = jnp.zeros((tail, LANES), BF16)

    for j in range(n_pages):
        kb_sc[j * page_rows:(j + 1) * page_rows, :] = k_pages[j][0, 0].astype(BF16)
        vb_sc[j * page_rows:(j + 1) * page_rows, :] = v_pages[j][0, 0].astype(BF16)
    kb_sc[past:past + n_new, :] = knew_ref[0]
    vb_sc[past:past + n_new, :] = vnew_ref[0]
    s = lax.dot_general(q_ref[0], kb_sc[...], _NT, preferred_element_type=F32) + bias_ref[...]
    p = jnp.exp(s - jnp.max(s, axis=-1, keepdims=True))
    p = p / jnp.sum(p, axis=-1, keepdims=True)
    a = p[:half] - lam * p[half:]
    o = _dot(a.astype(BF16), vb_sc[...])
    o = o * lax.rsqrt(jnp.mean(o * o, axis=-1, keepdims=True) + NORM_EPS)
    o_ref[0] = o * sg_ref[...] * out_scale


def _attn_sample(page_table_flat, layer_idx, scal, q_rows, knew, vnew, bias_s, sg_row, cache_k, cache_v, n_pages):
    nb, rows, _ = q_rows.shape
    n_new = knew.shape[1]
    page_rows = cache_k.shape[2]
    keys = bias_s.shape[1]

    def page_spec(j):
        return pl.BlockSpec((1, 1, page_rows, LANES), lambda b, pt, l: (l[0], pt[b * n_pages + j], 0, 0))

    grid_spec = pltpu.PrefetchScalarGridSpec(
        num_scalar_prefetch=2,
        grid=(nb,),
        in_specs=[pl.BlockSpec(memory_space=pltpu.SMEM),
                  pl.BlockSpec((1, rows, LANES), lambda b, pt, l: (b, 0, 0)),
                  pl.BlockSpec((1, n_new, LANES), lambda b, pt, l: (b, 0, 0)),
                  pl.BlockSpec((1, n_new, LANES), lambda b, pt, l: (b, 0, 0)),
                  pl.BlockSpec((rows, keys), lambda b, pt, l: (0, 0)),
                  pl.BlockSpec((1, V_DIM), lambda b, pt, l: (0, 0))]
                 + [page_spec(j) for j in range(n_pages)] * 2,
        out_specs=pl.BlockSpec((1, rows // 2, V_DIM), lambda b, pt, l: (b, 0, 0)),
        scratch_shapes=[pltpu.VMEM((keys, LANES), BF16),
                        pltpu.VMEM((keys, LANES), BF16)],
    )
    return pl.pallas_call(
        functools.partial(_attn_sample_body, n_pages),
        grid_spec=grid_spec,
        out_shape=jax.ShapeDtypeStruct((nb, rows // 2, V_DIM), F32),
        compiler_params=_cparams(("arbitrary",)),
        name="attn_sample",
    )(page_table_flat, layer_idx, scal, q_rows, knew, vnew, bias_s, sg_row,
      *([cache_k] * n_pages), *([cache_v] * n_pages))


def _lru_coeffs(xc, wa_ref, wi_ref, ba_ref, bi_ref, sp_ref):
    xb = xc.astype(BF16)
    ra, ri = [], []
    for j in range(xc.shape[1] // MXU_DIM):
        cols = slice(j * MXU_DIM, (j + 1) * MXU_DIM)
        ra.append(_dot(xb[:, cols], wa_ref[j]))
        ri.append(_dot(xb[:, cols], wi_ref[j]))
    r = jax.nn.sigmoid(jnp.concatenate(ra, axis=1) + ba_ref[...])
    i = jax.nn.sigmoid(jnp.concatenate(ri, axis=1) + bi_ref[...])
    log_a = -LRU_C * r * sp_ref[...]
    a = jnp.exp(log_a)
    mult = jnp.sqrt(-jnp.tanh(log_a) * (a * a + 1.0))
    return a, mult, i


def _rglru_prompt_body(xr_ref, yr_ref, cw_ref, cb_ref, wa_ref, wi_ref, ba_ref, bi_ref, sp_ref,
                       rnn_ref, hl_ref, xe_sc, a_sc, b_sc, h_sc):
    i = pl.program_id(1)
    tt = xr_ref.shape[0]

    @pl.when(i == 0)
    def _():
        xe_sc[0:SUBLANES, :] = jnp.zeros((SUBLANES, D_MODEL), F32)
        h_sc[...] = jnp.zeros(h_sc.shape, F32)

    xe_sc[SUBLANES:SUBLANES + tt, :] = xr_ref[...]
    xc = cb_ref[...]
    for j in range(CONV_W):
        off = SUBLANES - (CONV_W - 1) + j
        xc = xc + xe_sc[off:off + tt, :] * cw_ref[j:j + 1, :]
    a, mult, gate_i = _lru_coeffs(xc, wa_ref, wi_ref, ba_ref, bi_ref, sp_ref)
    row = lax.broadcasted_iota(jnp.int32, (tt, 1), 0)
    mult = jnp.where((row == 0) & (i == 0), 1.0, mult)
    a_sc[...] = a
    b_sc[...] = mult * gate_i * xc

    def step(t, h):
        h = a_sc[pl.ds(t, 1), :] * h + b_sc[pl.ds(t, 1), :]
        b_sc[pl.ds(t, 1), :] = h
        return h

    h = lax.fori_loop(0, tt, step, h_sc[...], unroll=8)
    h_sc[...] = h
    hl_ref[...] = h
    rnn_ref[...] = (b_sc[...] * jax.nn.gelu(yr_ref[...])).astype(BF16)
    xe_sc[0:SUBLANES, :] = xe_sc[tt:tt + SUBLANES, :]


def _rglru_prompt(xr, yr, cw, cb, wa_bd, wi_bd, ba, bi, sp, batch, seq):
    tt = min(TIME_TILE, seq)
    nt = seq // tt
    row = pl.BlockSpec((tt, D_MODEL), lambda b, i: (b * nt + i, 0))
    nbd = D_MODEL // MXU_DIM
    return pl.pallas_call(
        _rglru_prompt_body,
        grid=(batch, nt),
        in_specs=[row, row,
                  pl.BlockSpec((CONV_W, D_MODEL), lambda b, i: (0, 0)),
                  pl.BlockSpec((1, D_MODEL), lambda b, i: (0, 0)),
                  pl.BlockSpec((nbd, MXU_DIM, MXU_DIM), lambda b, i: (0, 0, 0)),
                  pl.BlockSpec((nbd, MXU_DIM, MXU_DIM), lambda b, i: (0, 0, 0)),
                  pl.BlockSpec((1, D_MODEL), lambda b, i: (0, 0)),
                  pl.BlockSpec((1, D_MODEL), lambda b, i: (0, 0)),
                  pl.BlockSpec((1, D_MODEL), lambda b, i: (0, 0))],
        out_specs=[row, pl.BlockSpec((None, 1, D_MODEL), lambda b, i: (b, 0, 0))],
        out_shape=[jax.ShapeDtypeStruct((batch * seq, D_MODEL), BF16),
                   jax.ShapeDtypeStruct((batch, 1, D_MODEL), F32)],
        scratch_shapes=[pltpu.VMEM((tt + SUBLANES, D_MODEL), F32),
                        pltpu.VMEM((tt, D_MODEL), F32),
                        pltpu.VMEM((tt, D_MODEL), F32),
                        pltpu.VMEM((1, D_MODEL), F32)],
        compiler_params=_cparams(("arbitrary", "arbitrary")),
        name="rglru_prompt",
    )(xr, yr, cw, cb, wa_bd, wi_bd, ba, bi, sp)


def _rglru_sample_body(n_steps, first_is_start, xr_ref, yr_ref, buf_ref, h0_ref, cw_ref, cb_ref, wa_ref, wi_ref, ba_ref, bi_ref,
                       sp_ref, rnn_ref, hl_ref):
    nb = h0_ref.shape[0]
    taps = [buf_ref[j] for j in range(CONV_W - 1)] + [xr_ref[t * nb:(t + 1) * nb, :] for t in range(n_steps)]
    h = h0_ref[...]
    for t in range(n_steps):
        xc = cb_ref[...]
        for j in range(CONV_W):
            xc = xc + taps[t + j] * cw_ref[j:j + 1, :]
        a, mult, gate_i = _lru_coeffs(xc, wa_ref, wi_ref, ba_ref, bi_ref, sp_ref)
        if first_is_start and t == 0:
            mult = jnp.ones_like(mult)
        h = a * h + mult * gate_i * xc
        rnn_ref[t * nb:(t + 1) * nb, :] = (h * jax.nn.gelu(yr_ref[t * nb:(t + 1) * nb, :])).astype(BF16)
    hl_ref[...] = h


def _rglru_sample(xr, yr, buf, h0, cw, cb, wa_bd, wi_bd, ba, bi, sp, n_steps, first_is_start):
    t = xr.shape[0]
    return pl.pallas_call(
        functools.partial(_rglru_sample_body, n_steps, first_is_start),
        out_shape=[jax.ShapeDtypeStruct((t, D_MODEL), BF16),
                   jax.ShapeDtypeStruct(h0.shape, F32)],
        compiler_params=pltpu.CompilerParams(vmem_limit_bytes=VMEM_LIMIT_BYTES),
        name="rglru_sample",
    )(xr, yr, buf, h0, cw, cb, wa_bd, wi_bd, ba, bi, sp)


def _route(logits):
    lane = lax.broadcasted_iota(jnp.int32, logits.shape, 1).astype(F32)
    none = float(ROUTER_LANES)
    is_group = lane < N_GROUPS
    lg = jnp.where(is_group, logits, -jnp.inf)
    eg = jnp.exp(lg - jnp.max(lg, axis=-1, keepdims=True))
    gp = eg / jnp.sum(eg, axis=-1, keepdims=True)
    gw = jnp.max(gp, axis=-1, keepdims=True)
    gi = jnp.min(jnp.where(is_group & (gp == gw), lane, none), axis=-1, keepdims=True)
    first = N_GROUPS + gi * EXPERTS_PER_GROUP
    in_group = (lane >= first) & (lane < first + EXPERTS_PER_GROUP)
    le = jnp.where(in_group, logits, -jnp.inf)
    ee = jnp.exp(le - jnp.max(le, axis=-1, keepdims=True))
    pe = ee / jnp.sum(ee, axis=-1, keepdims=True)
    p1 = jnp.max(pe, axis=-1, keepdims=True)
    i1 = jnp.min(jnp.where(in_group & (pe == p1), lane, none), axis=-1, keepdims=True)
    rest = in_group & (lane != i1)
    pr = jnp.where(rest, pe, -1.0)
    p2 = jnp.max(pr, axis=-1, keepdims=True)
    i2 = jnp.min(jnp.where(rest & (pr == p2), lane, none), axis=-1, keepdims=True)
    den = p1 + p2
    w1 = gw * (p1 / den)
    w2 = gw * (p2 / den)
    rec = jnp.where(lane == 0.0, i1 - N_GROUPS, 0.0)
    rec = jnp.where(lane == 1.0, i2 - N_GROUPS, rec)
    rec = jnp.where(lane == 2.0, w1, rec)
    rec = jnp.where(lane == 3.0, w2, rec)
    return rec, jnp.where(lane == i1, 1.0, 0.0), jnp.where(lane == i2, 1.0, 0.0)


def _merge_body(x_ref, attn_ref, rnn_ref, ga_ref, gr_ref, g1_ref, sc2_ref, sh2_ref, n2g_ref,
                wpa_ref, wpr_ref, wo_ref, wrh_ref, wrl_ref, br_ref, ltri_ref, cnt_in_ref,
                x1_ref, u2_ref, rec_ref, cnt_ref, cnt_sc):
    pa = _dot(attn_ref[...], wpa_ref[...])
    pr = _dot(rnn_ref[...], wpr_ref[...])
    merged = jax.nn.sigmoid(ga_ref[...]) * pa + jax.nn.sigmoid(gr_ref[...]) * pr
    x1 = x_ref[...] + g1_ref[...] * _dot(merged.astype(BF16), wo_ref[...])
    x1_ref[...] = x1
    u2 = _rms_rows(x1, n2g_ref[...]) * (1.0 + sc2_ref[...]) + sh2_ref[...]
    u2_ref[...] = u2
    u_hi = u2.astype(BF16)
    u_lo = (u2 - u_hi.astype(F32)).astype(BF16)
    logits = (_dot(u_hi, wrh_ref[...]) + _dot(u_lo, wrh_ref[...]) + _dot(u_hi, wrl_ref[...])) + br_ref[...]
    rec, hot1, hot2 = _route(logits)

    @pl.when(pl.program_id(0) == 0)
    def _():
        cnt_sc[...] = cnt_in_ref[...]
    cnt = cnt_sc[...]
    tot1 = jnp.sum(hot1, axis=0, keepdims=True)
    tot2 = jnp.sum(hot2, axis=0, keepdims=True)
    before1 = _dot(ltri_ref[...], hot1.astype(BF16)) + cnt
    before2 = _dot(ltri_ref[...], hot2.astype(BF16)) + (cnt + tot1)
    rank1 = jnp.sum(hot1 * before1, axis=-1, keepdims=True)
    rank2 = jnp.sum(hot2 * before2, axis=-1, keepdims=True)
    cnt_sc[...] = cnt + tot1 + tot2
    cnt_ref[...] = cnt_sc[...]
    lane = lax.broadcasted_iota(jnp.int32, rec.shape, 1)
    rec = jnp.where(lane == 4, rank1, rec)
    rec_ref[...] = jnp.where(lane == 5, rank2, rec)


def _merge(x, attn, rnn, ga, gr, mods, rows_per_group, cnt_in, n2g, wpa, wpr, wo, wr_hi, wr_lo, br):
    t = x.shape[0]
    tile = min(TOKEN_TILE, t)
    row = lambda n: pl.BlockSpec((tile, n), lambda i: (i, 0))
    sq = _resident((D_MODEL, D_MODEL))
    r = jnp.arange(tile)
    ltri = (r[None, :] < r[:, None]).astype(BF16)
    return pl.pallas_call(
        _merge_body,
        grid=(t // tile,),
        in_specs=[row(D_MODEL), row(D_MODEL), row(D_MODEL), row(D_MODEL), row(D_MODEL),
                  _mod_spec(mods, tile, rows_per_group, 2),
                  _mod_spec(mods, tile, rows_per_group, 4),
                  _mod_spec(mods, tile, rows_per_group, 3),
                  _resident((1, D_MODEL)), sq, sq, sq,
                  _resident((D_MODEL, ROUTER_LANES)), _resident((D_MODEL, ROUTER_LANES)),
                  _resident((1, ROUTER_LANES)), _resident((tile, tile)), _resident((1, ROUTER_LANES))],
        out_specs=[row(D_MODEL), row(D_MODEL), row(ROUTER_LANES), _resident((1, ROUTER_LANES))],
        out_shape=[jax.ShapeDtypeStruct((t, D_MODEL), F32),
                   jax.ShapeDtypeStruct((t, D_MODEL), F32),
                   jax.ShapeDtypeStruct((t, ROUTER_LANES), F32),
                   jax.ShapeDtypeStruct((1, ROUTER_LANES), F32)],
        scratch_shapes=[pltpu.VMEM((1, ROUTER_LANES), F32)],
        compiler_params=_cparams(("arbitrary",)),
        name="merge",
    )(x, attn, rnn, ga, gr, mods, mods, mods, n2g, wpa, wpr, wo, wr_hi, wr_lo, br, ltri, cnt_in)


ROW_UNROLL = 8


def _for_rows(n_rows, fn):
    def body(j, carry):
        for s in range(ROW_UNROLL):
            fn(j * ROW_UNROLL + s)
        return carry
    lax.fori_loop(0, n_rows // ROW_UNROLL, body, 0)


def _dispatch_body(t_all, n_first, pos_ref, ua_ref, ub_ref, init_hbm, us_hbm, sem):
    del init_hbm
    i = pl.program_id(0)

    def scatter_tile(u_ref, base):
        tile = u_ref.shape[0]

        def issue(r):
            for slot in range(2):
                pltpu.make_async_copy(u_ref.at[pl.ds(r, 1)],
                                      us_hbm.at[pl.ds(pos_ref[slot * t_all + base + r], 1)], sem.at[0]).start()
        _for_rows(tile, issue)
        for _ in range(2):
            pltpu.make_async_copy(u_ref, us_hbm.at[pl.ds(0, tile)], sem.at[0]).wait()

    @pl.when(i < n_first)
    def _():
        scatter_tile(ua_ref, i * ua_ref.shape[0])

    @pl.when(i >= n_first)
    def _():
        scatter_tile(ub_ref, n_first * ua_ref.shape[0] + (i - n_first) * ub_ref.shape[0])


def _dispatch(pos_flat, u_first, u_second, n_sorted_rows):
    t_all = u_first.shape[0] + u_second.shape[0]
    tile_a = min(TOKEN_TILE, u_first.shape[0])
    tile_b = min(TOKEN_TILE, u_second.shape[0])
    n_a = u_first.shape[0] // tile_a
    n_b = u_second.shape[0] // tile_b
    assert n_a * tile_a == u_first.shape[0] and n_b * tile_b == u_second.shape[0]
    assert tile_a % ROW_UNROLL == 0 and tile_b % ROW_UNROLL == 0
    grid_spec = pltpu.PrefetchScalarGridSpec(
        num_scalar_prefetch=1,
        grid=(n_a + n_b,),
        in_specs=[pl.BlockSpec((tile_a, D_MODEL), lambda i, pos: (jnp.minimum(i, n_a - 1), 0)),
                  pl.BlockSpec((tile_b, D_MODEL), lambda i, pos: (jnp.maximum(i - n_a, 0), 0)),
                  pl.BlockSpec(memory_space=pl.ANY)],
        out_specs=pl.BlockSpec(memory_space=pl.ANY),
        scratch_shapes=[pltpu.SemaphoreType.DMA((1,))],
    )
    return pl.pallas_call(
        functools.partial(_dispatch_body, t_all, n_a),
        grid_spec=grid_spec,
        out_shape=jax.ShapeDtypeStruct((n_sorted_rows, D_MODEL), F32),
        input_output_aliases={3: 0},
        compiler_params=_cparams(("arbitrary",)),
        name="dispatch",
    )(pos_flat, u_first, u_second, jnp.zeros((n_sorted_rows, D_MODEL), F32))


def _experts_body(te_ref, nv_ref, last_ref, layer_ref, u_ref, wg_ref, wu_ref, wd_ref, y_ref, wg_sc, wu_sc, wd_sc):
    del last_ref, layer_ref
    i = pl.program_id(0)

    @pl.when((i == 0) | (te_ref[i] != te_ref[jnp.maximum(i - 1, 0)]))
    def _():
        wg_sc[...] = wg_ref[...].astype(BF16)
        wu_sc[...] = wu_ref[...].astype(BF16)
        wd_sc[...] = wd_ref[...].astype(BF16)

    @pl.when(nv_ref[i] > 0)
    def _():
        u = u_ref[...].astype(BF16)
        hg = _dot(u, wg_sc[...])
        hid = (hg * jax.nn.sigmoid(hg)) * _dot(u, wu_sc[...])
        y_ref[...] = _dot(hid.astype(BF16), wd_sc[...])

    @pl.when(nv_ref[i] == 0)
    def _():
        y_ref[...] = jnp.zeros(y_ref.shape, F32)


def _experts(tile_expert, tile_valid, last_tile, layer_idx, u_sorted, w_gate, w_up, w_down):
    tile = EXPERT_TILE
    n_tiles = tile_expert.shape[0]
    wspec = lambda a, b: pl.BlockSpec((None, None, a, b), lambda i, te, nv, last, lyr: (lyr[0], te[i], 0, 0))
    grid_spec = pltpu.PrefetchScalarGridSpec(
        num_scalar_prefetch=4,
        grid=(n_tiles,),
        in_specs=[pl.BlockSpec((tile, D_MODEL), lambda i, te, nv, last, lyr: (jnp.minimum(i, last[0]), 0)),
                  wspec(D_MODEL, D_EXPERT), wspec(D_MODEL, D_EXPERT), wspec(D_EXPERT, D_MODEL)],
        out_specs=pl.BlockSpec((tile, D_MODEL), lambda i, te, nv, last, lyr: (i, 0)),
        scratch_shapes=[pltpu.VMEM((D_MODEL, D_EXPERT), BF16),
                        pltpu.VMEM((D_MODEL, D_EXPERT), BF16),
                        pltpu.VMEM((D_EXPERT, D_MODEL), BF16)],
    )
    return pl.pallas_call(
        _experts_body,
        grid_spec=grid_spec,
        out_shape=jax.ShapeDtypeStruct(u_sorted.shape, F32),
        compiler_params=_cparams(("arbitrary",)),
        name="experts",
    )(tile_expert, tile_valid, last_tile, layer_idx, u_sorted, w_gate, w_up, w_down)


def _dispatch_plan(rec_all, counts_row):
    t_all = rec_all.shape[0]
    tile = EXPERT_TILE
    n_tiles = -(-2 * t_all // tile) + N_EXPERTS
    counts = counts_row[0, N_GROUPS:N_GROUPS + N_EXPERTS].astype(jnp.int32)
    tiles_per_e = (counts + tile - 1) // tile
    tile_end = jnp.cumsum(tiles_per_e)
    tile_start = tile_end - tiles_per_e
    experts = jnp.arange(N_EXPERTS, dtype=jnp.int32)
    e = rec_all[:, 0:2].astype(jnp.int32)
    rank = rec_all[:, 4:6].astype(jnp.int32)
    row_start = jnp.sum(jnp.where(e[:, :, None] == experts, tile_start * tile, 0), axis=-1)
    pos_flat = (row_start + rank).T.reshape(-1)
    n_used = tile_end[-1]
    tile_ids = jnp.arange(n_tiles, dtype=jnp.int32)
    clamped = jnp.minimum(tile_ids, n_used - 1)
    te = jnp.sum((clamped[:, None] >= tile_end[None, :]).astype(jnp.int32), axis=-1)
    te = jnp.minimum(te, N_EXPERTS - 1)
    first = jnp.sum(jnp.where(te[:, None] == experts, tile_start, 0), axis=-1)
    cnt_t = jnp.sum(jnp.where(te[:, None] == experts, counts, 0), axis=-1)
    valid = jnp.clip(cnt_t - (tile_ids - first) * tile, 0, tile)
    valid = jnp.where(tile_ids < n_used, valid, 0).astype(jnp.int32)
    return pos_flat, te, valid, (n_used - 1).reshape(1).astype(jnp.int32), n_tiles * tile


def _combine_body(t_all, row_offset, pos_ref, x_ref, rec_ref, g2_ref, ys_hbm, o_ref, ybuf, sem):
    i = pl.program_id(0)
    n = pl.num_programs(0)
    tile = x_ref.shape[0]
    slot = i % 2

    def start_gather(tile_idx, buf):
        base = row_offset + tile_idx * tile

        def issue(r):
            for k in range(2):
                pltpu.make_async_copy(ys_hbm.at[pl.ds(pos_ref[k * t_all + base + r], 1)],
                                      ybuf.at[buf, k, pl.ds(r, 1)], sem.at[buf]).start()
        _for_rows(tile, issue)

    @pl.when(i == 0)
    def _():
        start_gather(0, 0)

    @pl.when(i + 1 < n)
    def _():
        start_gather(i + 1, 1 - slot)

    for k in range(2):
        pltpu.make_async_copy(ys_hbm.at[pl.ds(0, tile)], ybuf.at[slot, k], sem.at[slot]).wait()
    w1 = rec_ref[:, 2:3]
    w2 = rec_ref[:, 3:4]
    o_ref[...] = x_ref[...] + g2_ref[...] * (w1 * ybuf[slot, 0] + w2 * ybuf[slot, 1])


def _combine(pos_flat, x1, rec, y_sorted, mods, rows_per_group, row_offset, t_all):
    t = x1.shape[0]
    tile = min(TOKEN_TILE, t)
    assert tile % ROW_UNROLL == 0
    if mods.ndim == 3:
        tiles_per_group = rows_per_group // tile
        g2_spec = pl.BlockSpec((None, 1, D_MODEL), lambda i, pos: (i // tiles_per_group, 0, 5))
    else:
        g2_spec = pl.BlockSpec((tile, D_MODEL), lambda i, pos: (i, 5))
    grid_spec = pltpu.PrefetchScalarGridSpec(
        num_scalar_prefetch=1,
        grid=(t // tile,),
        in_specs=[pl.BlockSpec((tile, D_MODEL), lambda i, pos: (i, 0)),
                  pl.BlockSpec((tile, ROUTER_LANES), lambda i, pos: (i, 0)),
                  g2_spec,
                  pl.BlockSpec(memory_space=pl.ANY)],
        out_specs=pl.BlockSpec((tile, D_MODEL), lambda i, pos: (i, 0)),
        scratch_shapes=[pltpu.VMEM((2, 2, tile, D_MODEL), F32),
                        pltpu.SemaphoreType.DMA((2,))],
    )
    return pl.pallas_call(
        functools.partial(_combine_body, t_all, row_offset),
        grid_spec=grid_spec,
        out_shape=jax.ShapeDtypeStruct((t, D_MODEL), F32),
        compiler_params=_cparams(("arbitrary",)),
        name="combine",
    )(pos_flat, x1, rec, mods, y_sorted)


def _block_diag(w):
    per = MXU_DIM // RNN_BLOCK
    w4 = w.reshape(N_RNN_BLOCKS // per, per, RNN_BLOCK, RNN_BLOCK)
    eye = jnp.eye(per, dtype=w.dtype)
    return jnp.einsum("jmcd,mn->jmcnd", w4, eye).reshape(N_RNN_BLOCKS // per, MXU_DIM, MXU_DIM)


def kernel(x_prompt, x_sample, cache_k, cache_v, state_rnn, state_conv, page_table, c_prompt, c_sample, rel_bias, w_mod, b_mod, norm1_g, norm2_g, w_in, q_norm_g, k_norm_g, lam_q1, lam_k1, lam_q2, lam_k2, subln_g, conv_w, conv_b, lru_wa, lru_ba, lru_wi, lru_bi, lru_lambda, w_pa, w_pr, w_o, w_rg, b_rg, w_re, b_re, w_e_gate, w_e_up, w_e_down):
    bp, tp, _ = x_prompt.shape
    bs, ts, _ = x_sample.shape
    depth = w_in.shape[0]
    n_pool = cache_k.shape[1]
    n_pages = page_table.shape[1]
    past_len = n_pages * PAGE_SIZE
    t_p = bp * tp
    t_s = bs * ts
    t_all = t_p + t_s
    blk = ATTN_BLOCK
    nblk = tp // blk
    assert tp % blk == 0 and tp % TIME_TILE == 0 and t_p % TOKEN_TILE == 0
    assert t_s % SUBLANES == 0 and t_p % min(TOKEN_TILE, t_s) == 0
    log2e = math.log2(math.e)

    rows_c = -(-(bp + bs) // SUBLANES) * SUBLANES
    c_all = jnp.zeros((rows_c, D_MODEL), F32).at[:bp].set(c_prompt).at[bp:bp + bs].set(c_sample)
    mods = _mods(c_all, w_mod, b_mod)
    mods_p = mods[:, :bp].reshape(depth, bp, 1, 6 * D_MODEL)
    mods_s = jnp.tile(mods[:, bp:bp + bs], (1, ts, 1))

    w_in_bf = w_in.astype(BF16)
    w_pa_bf, w_pr_bf, w_o_bf = w_pa.astype(BF16), w_pr.astype(BF16), w_o.astype(BF16)
    qg_row = jnp.tile(q_norm_g, (1, Q_COLS // HEAD_DIM)).reshape(depth, 1, Q_COLS)
    kg_row = jnp.tile(k_norm_g, (1, K_COLS // HEAD_DIM)).reshape(depth, 1, K_COLS)
    seg = jnp.arange(MXU_DIM) // HEAD_DIM
    chunk_p = ((seg[:, None] == seg[None, :]).astype(F32) / HEAD_DIM).astype(BF16)
    wr = jnp.concatenate([w_rg, w_re], axis=-1)
    wr = jnp.pad(wr, ((0, 0), (0, 0), (0, ROUTER_LANES - wr.shape[-1])))
    wr_hi = wr.astype(BF16)
    wr_lo = (wr - wr_hi.astype(F32)).astype(BF16)
    br = jnp.pad(jnp.concatenate([b_rg, b_re], axis=-1), ((0, 0), (0, ROUTER_LANES - N_GROUPS - N_EXPERTS)))
    br = br.reshape(depth, 1, ROUTER_LANES)
    wa_bd = jax.vmap(_block_diag)(lru_wa).astype(BF16)
    wi_bd = jax.vmap(_block_diag)(lru_wi).astype(BF16)
    ba = lru_ba.reshape(depth, 1, D_MODEL)
    bi = lru_bi.reshape(depth, 1, D_MODEL)
    sp = jax.nn.softplus(-lru_lambda.astype(F32)).reshape(depth, 1, D_MODEL)
    cb = conv_b.reshape(depth, 1, D_MODEL)
    lam_f = lambda a, b: jnp.exp(jnp.sum(a.astype(F32) * b.astype(F32), axis=-1))
    lam_base = lam_f(lam_q1, lam_k1) - lam_f(lam_q2, lam_k2)

    bias_diag = _bias_tile(rel_bias, blk, 0) * log2e
    bias_prev = _bias_tile(rel_bias, blk, blk) * log2e
    rb = rel_bias.astype(F32) * log2e
    bias_scal = jnp.concatenate([rb[NUM_BUCKETS - 1],
                                 jnp.max(rb, axis=0), jnp.max(rb, axis=0) - jnp.min(rb, axis=0)])

    page_rows = PAGE_SIZE * N_KV_HEADS
    n_new = -(-ts * N_KV_HEADS // (2 * SUBLANES)) * (2 * SUBLANES)
    keys_s = n_pages * page_rows + -(-n_new // LANES) * LANES
    n_pos = past_len + ts
    dist = jnp.arange(-(ts - 1), n_pos, dtype=jnp.int32)
    flipped = _bias_of_distance(rel_bias, dist)[:, ::-1]
    by_token = jnp.stack([flipped[:, ts - 1 - t:ts - 1 - t + n_pos] for t in range(ts)], axis=1)
    by_row = jnp.transpose(by_token.reshape(N_KV_HEADS, GQA_GROUP, ts, n_pos), (0, 2, 1, 3))
    own_head = (jnp.arange(N_KV_HEADS)[:, None, None, None, None]
                == jnp.arange(N_KV_HEADS)[None, None, None, None, :])
    bias_rows = jnp.where(own_head, by_row[..., None], MASK_VALUE)
    bias_rows = bias_rows.reshape(N_KV_HEADS * ts * GQA_GROUP, n_pos * N_KV_HEADS)
    bias_rows = jnp.pad(bias_rows, ((0, 0), (0, keys_s - n_pos * N_KV_HEADS)), constant_values=MASK_VALUE)
    bias_s = jnp.concatenate([bias_rows, bias_rows], axis=0)
    pt_flat = page_table.reshape(-1).astype(jnp.int32)
    cache_k_rows = cache_k.reshape(depth, n_pool, page_rows, LANES)
    cache_v_rows = cache_v.reshape(depth, n_pool, page_rows, LANES)

    xp = x_prompt.reshape(t_p, D_MODEL)
    xs = jnp.swapaxes(x_sample, 0, 1).reshape(t_s, D_MODEL)
    outs = [[] for _ in range(8)]
    for l in range(depth):
        lam_init = 0.8 - 0.6 * math.exp(-0.3 * l)
        lam = lam_base[l] + lam_init
        sg_col = subln_g[l].astype(F32).reshape(V_DIM, 1)
        sg_row = subln_g[l].astype(F32).reshape(1, V_DIM)
        scal_s = jnp.stack([lam, jnp.asarray(1.0 - lam_init, F32)]).astype(F32)
        scal_p = jnp.concatenate([scal_s, bias_scal])
        lp = (norm1_g[l].reshape(1, D_MODEL), w_in_bf[l], qg_row[l], kg_row[l], chunk_p)
        rg = (conv_w[l], cb[l], wa_bd[l], wi_bd[l], ba[l], bi[l], sp[l])
        mg = (norm2_g[l].reshape(1, D_MODEL), w_pa_bf[l], w_pr_bf[l], w_o_bf[l], wr_hi[l], wr_lo[l], br[l])

        q, k, kb, v, xr, yr, ga, gr = _in_proj(xp, mods_p[l], tp, ATTN_SCALE * log2e, *lp)
        vt = jnp.swapaxes(v.reshape(bp, nblk, blk, N_KV_HEADS, V_DIM).astype(BF16), 2, 4)
        vt = jnp.transpose(vt, (0, 3, 1, 2, 4)).reshape(bp * N_KV_HEADS, nblk, V_DIM, blk)
        ones = jnp.zeros((bp * N_KV_HEADS, nblk, ONES_ROWS, blk), BF16).at[:, :, 0].set(1.0)
        vt_ext = jnp.concatenate([vt, ones], axis=2)
        attn = _attn_prompt(scal_p, q, kb, vt_ext, bias_diag, bias_prev, sg_col, bp, tp)
        rnn, h_last = _rglru_prompt(xr, yr, *rg, bp, tp)
        x1p, u2p, recp, cnt = _merge(xp, attn, rnn, ga, gr, mods_p[l], tp,
                                     jnp.zeros((1, ROUTER_LANES), F32), *mg)
        outs[0].append(k.reshape(bp, tp, N_KV_HEADS, 2 * HEAD_DIM))
        outs[1].append(v.reshape(bp, tp, N_KV_HEADS, V_DIM))
        outs[2].append(h_last.reshape(bp, D_MODEL))
        outs[3].append(xr.reshape(bp, tp, D_MODEL)[:, tp - (CONV_W - 1):])

        q, k, kb, v, xr, yr, ga, gr = _in_proj(xs, mods_s[l], ts, ATTN_SCALE, *lp)
        q6 = q.reshape(ts, bs, N_KV_HEADS, GQA_GROUP, 2, HEAD_DIM)
        q6 = jnp.transpose(q6, (1, 4, 2, 0, 3, 5))
        zq = jnp.zeros_like(q6[:, 0])
        q_rows = jnp.stack([jnp.concatenate([q6[:, 0], zq], axis=-1),
                            jnp.concatenate([zq, q6[:, 1]], axis=-1)], axis=1)
        q_rows = q_rows.reshape(bs, 2 * N_KV_HEADS * ts * GQA_GROUP, 2 * HEAD_DIM)
        new_rows = lambda z: jnp.pad(jnp.swapaxes(z.reshape(ts, bs, N_KV_HEADS, LANES), 0, 1)
                                     .reshape(bs, ts * N_KV_HEADS, LANES),
                                     ((0, 0), (0, n_new - ts * N_KV_HEADS), (0, 0)))
        o_s = _attn_sample(pt_flat, jnp.full((1,), l, jnp.int32), scal_s, q_rows, new_rows(kb),
                           new_rows(v.astype(BF16)), bias_s, sg_row, cache_k_rows, cache_v_rows, n_pages)
        attn = jnp.transpose(o_s.reshape(bs, N_KV_HEADS, ts, GQA_GROUP, V_DIM), (2, 0, 1, 3, 4))
        attn = attn.reshape(t_s, N_HEADS * V_DIM).astype(BF16)
        buf = jnp.swapaxes(state_conv[l], 0, 1)
        rnn, h_last = _rglru_sample(xr, yr, buf, state_rnn[l], *rg, ts, past_len == 0)
        x1s, u2s, recs, cnt = _merge(xs, attn, rnn, ga, gr, mods_s[l], ts, cnt, *mg)
        to_bt = lambda z, n: jnp.swapaxes(z.reshape(ts, bs, n), 0, 1)
        outs[4].append(to_bt(k, K_COLS).reshape(bs, ts, N_KV_HEADS, 2 * HEAD_DIM))
        outs[5].append(to_bt(v, V_COLS).reshape(bs, ts, N_KV_HEADS, V_DIM))
        outs[6].append(h_last)
        xin = jnp.concatenate([state_conv[l], to_bt(xr, D_MODEL)], axis=1)
        outs[7].append(xin[:, -(CONV_W - 1):])

        pos_flat, te, valid, last_tile, n_sorted = _dispatch_plan(jnp.concatenate([recp, recs], axis=0), cnt)
        u_sorted = _dispatch(pos_flat, u2p, u2s, n_sorted)
        y_sorted = _experts(te, valid, last_tile, jnp.full((1,), l, jnp.int32), u_sorted,
                            w_e_gate, w_e_up, w_e_down)
        xp = _combine(pos_flat, x1p, recp, y_sorted, mods_p[l], tp, 0, t_all)
        xs = _combine(pos_flat, x1s, recs, y_sorted, mods_s[l], ts, t_p, t_all)

    y_sample = jnp.swapaxes(xs.reshape(ts, bs, D_MODEL), 0, 1)
    stack = lambda i: jnp.stack(outs[i])
    return (xp.reshape(bp, tp, D_MODEL), y_sample, stack(0), stack(1), stack(2), stack(3),
            stack(4), stack(5), stack(6), stack(7))
```

```python
import functools
import math

import jax
import jax.numpy as jnp
from jax import lax
from jax.experimental import pallas as pl
from jax.experimental.pallas import tpu as pltpu

F32 = jnp.float32
BF16 = jnp.bfloat16

D_MODEL = 1024
HEAD_DIM = 64
N_HEADS = 8
N_KV_HEADS = 4
GQA_GROUP = 2
V_DIM = 128
ATTN_SCALE = HEAD_DIM ** -0.5
MASK_VALUE = -1e30
NUM_BUCKETS = 32
MAX_EXACT = 16
MAX_DISTANCE = 128
N_RNN_BLOCKS = 16
RNN_BLOCK = 64
CONV_W = 4
LRU_C = 8.0
N_GROUPS = 4
EXPERTS_PER_GROUP = 8
N_EXPERTS = 32
D_EXPERT = 512
NORM_EPS = 1e-6
PAGE_SIZE = 128
Q_COLS = 1024
K_COLS = 512
V_COLS = 512
D_IN = 6144

LANES = 128
SUBLANES = 8
MXU_DIM = 256
VMEM_LIMIT_BYTES = 56 * 1024 * 1024

TOKEN_TILE = 256
ATTN_BLOCK = 512
TIME_TILE = 256
EXPERT_TILE = 256
ROUTER_LANES = 128
ONES_ROWS = 16

_NT = (((1,), (1,)), ((), ()))


def _cparams(sem):
    return pltpu.CompilerParams(dimension_semantics=sem, vmem_limit_bytes=VMEM_LIMIT_BYTES)


def _dot(a, b):
    return jnp.dot(a, b, preferred_element_type=F32)


def _resident(shape):
    n = len(shape)
    return pl.BlockSpec(shape, lambda *_: (0,) * n)


_MODS_COLS = 1536


def _mods_body(c_ref, w_ref, b_ref, o_ref):
    c = c_ref[...]
    s = c * jax.nn.sigmoid(c)
    o_ref[...] = jnp.dot(s, w_ref[...], precision=lax.Precision.HIGHEST,
                         preferred_element_type=F32) + b_ref[...]


def _mods(c_all, w_mod, b_mod):
    rows = c_all.shape[0]
    depth, _, n = w_mod.shape
    return pl.pallas_call(
        _mods_body,
        grid=(depth, n // _MODS_COLS),
        in_specs=[pl.BlockSpec((rows, D_MODEL), lambda l, j: (0, 0)),
                  pl.BlockSpec((None, D_MODEL, _MODS_COLS), lambda l, j: (l, 0, j)),
                  pl.BlockSpec((None, 1, _MODS_COLS), lambda l, j: (l, 0, j))],
        out_specs=pl.BlockSpec((None, rows, _MODS_COLS), lambda l, j: (l, 0, j)),
        out_shape=jax.ShapeDtypeStruct((depth, rows, n), F32),
        compiler_params=_cparams(("arbitrary", "arbitrary")),
        name="mods",
    )(c_all, w_mod, b_mod.reshape(depth, 1, n))


def _rms_rows(x, g):
    ms = jnp.mean(x * x, axis=-1, keepdims=True)
    return x * lax.rsqrt(ms + NORM_EPS) * g


def _chunk_mean_square(z, p):
    parts = []
    for j in range(z.shape[1] // MXU_DIM):
        zz = z[:, j * MXU_DIM:(j + 1) * MXU_DIM]
        parts.append(_dot((zz * zz).astype(BF16), p))
    return jnp.concatenate(parts, axis=1)


def _in_proj_body(q_scale, x_ref, sc_ref, sh_ref, g_ref, w_ref, qg_ref, kg_ref, p_ref,
                  q_ref, k_ref, kb_ref, v_ref, xr_ref, yr_ref, ga_ref, gr_ref, vt_ref=None):
    u = (_rms_rows(x_ref[...], g_ref[...]) * (1.0 + sc_ref[...]) + sh_ref[...]).astype(BF16)
    p = p_ref[...]
    zq = _dot(u, w_ref[:, 0:Q_COLS])
    q = zq * lax.rsqrt(_chunk_mean_square(zq, p) + NORM_EPS) * qg_ref[...]
    q_ref[...] = (q * q_scale).astype(BF16)
    zk = _dot(u, w_ref[:, Q_COLS:Q_COLS + K_COLS])
    k = zk * lax.rsqrt(_chunk_mean_square(zk, p) + NORM_EPS) * kg_ref[...]
    k_ref[...] = k
    kb_ref[...] = k.astype(BF16)
    c0 = Q_COLS + K_COLS
    v = _dot(u, w_ref[:, c0:c0 + V_COLS])
    v_ref[...] = v
    if vt_ref is not None:
        row = lax.broadcasted_iota(jnp.int32, (ONES_ROWS, v.shape[0]), 0)
        ones_rows = jnp.where(row == 0, 1.0, 0.0).astype(BF16)
        for h in range(N_KV_HEADS):
            vt_ref[h, 0:V_DIM, :] = v[:, h * V_DIM:(h + 1) * V_DIM].T.astype(BF16)
            vt_ref[h, V_DIM:V_DIM + ONES_ROWS, :] = ones_rows
    c0 += V_COLS
    xr_ref[...] = _dot(u, w_ref[:, c0:c0 + D_MODEL])
    c0 += D_MODEL
    yr_ref[...] = _dot(u, w_ref[:, c0:c0 + D_MODEL])
    c0 += D_MODEL
    ga_ref[...] = _dot(u, w_ref[:, c0:c0 + D_MODEL])
    c0 += D_MODEL
    gr_ref[...] = _dot(u, w_ref[:, c0:c0 + D_MODEL])


def _mod_spec(mods, tile, rows_per_group, col_block):
    if mods.ndim == 3:
        tiles_per_group = rows_per_group // tile
        return pl.BlockSpec((None, 1, D_MODEL), lambda i: (i // tiles_per_group, 0, col_block))
    return pl.BlockSpec((tile, D_MODEL), lambda i: (i, col_block))


def _in_proj(x, mods, rows_per_group, q_scale, vt_block, norm_g, w_in_bf, qg_row, kg_row, chunk_p):
    t = x.shape[0]
    tile = min(TOKEN_TILE, t)
    row = lambda n: pl.BlockSpec((tile, n), lambda i: (i, 0))
    outs = [(Q_COLS, BF16), (K_COLS, F32), (K_COLS, BF16), (V_COLS, F32),
            (D_MODEL, F32), (D_MODEL, F32), (D_MODEL, F32), (D_MODEL, F32)]
    out_specs = [row(n) for n, _ in outs]
    out_shape = [jax.ShapeDtypeStruct((t, n), dt) for n, dt in outs]
    if vt_block is not None:
        tiles_per_seq = rows_per_group // tile
        per_block = vt_block // tile
        vrows = V_DIM + ONES_ROWS
        out_specs.append(pl.BlockSpec(
            (N_KV_HEADS, None, vrows, tile),
            lambda i: (i // tiles_per_seq, (i % tiles_per_seq) // per_block, 0, (i % tiles_per_seq) % per_block)))
        out_shape.append(jax.ShapeDtypeStruct(
            (t // rows_per_group * N_KV_HEADS, rows_per_group // vt_block, vrows, vt_block), BF16))
    return pl.pallas_call(
        functools.partial(_in_proj_body, q_scale),
        grid=(t // tile,),
        in_specs=[row(D_MODEL),
                  _mod_spec(mods, tile, rows_per_group, 1),
                  _mod_spec(mods, tile, rows_per_group, 0),
                  _resident((1, D_MODEL)),
                  _resident((D_MODEL, D_IN)),
                  _resident((1, Q_COLS)),
                  _resident((1, K_COLS)),
                  _resident((MXU_DIM, MXU_DIM))],
        out_specs=out_specs,
        out_shape=out_shape,
        compiler_params=_cparams(("arbitrary",)),
        name="in_proj",
    )(x, mods, mods, norm_g, w_in_bf, qg_row, kg_row, chunk_p)


def _bucket_of_distance(n):
    n = jnp.maximum(n, 0)
    nf = jnp.maximum(n, 1).astype(F32)
    large = MAX_EXACT + (jnp.log(nf / MAX_EXACT) / math.log(MAX_DISTANCE / MAX_EXACT)
                         * (NUM_BUCKETS - MAX_EXACT)).astype(jnp.int32)
    return jnp.where(n < MAX_EXACT, n, jnp.minimum(large, NUM_BUCKETS - 1))


def _bias_of_distance(rel_bias, n):
    b = rel_bias.astype(F32)[_bucket_of_distance(n)]
    b = jnp.where((n >= 0)[..., None], b, MASK_VALUE)
    return jnp.moveaxis(b, -1, 0)


def _bias_table(rel_bias, blk, offset):
    j = jnp.arange(2 * blk, dtype=jnp.int32)
    diff = jnp.where(j < blk, j, j - 2 * blk)
    return _bias_of_distance(rel_bias, diff + offset)[:, None, :]


_S_LAM, _S_OUT = 0, 1
_S_FAR = 2
_S_BMAX = _S_FAR + N_HEADS
_S_BRANGE = _S_BMAX + N_HEADS
BOUND_MARGIN = 1.01
SAFE_EXPONENT_SPAN = 100.0


def _attn_prompt_body(scal_ref, q_ref, k_ref, vt_ref, tdiag_ref, tprev_ref, sg_ref,
                      o_ref, qpad_sc, kmax_sc, bq_sc, m_sc, acc_sc, bdiag_ref, bprev_ref):
    h = pl.program_id(1)
    qi = pl.program_id(2)
    blk = q_ref.shape[0]
    ncombo = 2 * GQA_GROUP
    lam = scal_ref[_S_LAM]
    out_scale = scal_ref[_S_OUT]

    @pl.when(qi == 0)
    def _():
        kk = k_ref[...].astype(F32)
        r = lax.broadcasted_iota(jnp.int32, (LANES, LANES), 0) // HEAD_DIM
        c = lax.broadcasted_iota(jnp.int32, (LANES, LANES), 1) // HEAD_DIM
        same = jnp.where(r == c, 1.0, 0.0).astype(BF16)
        ksq = _dot((kk * kk).astype(BF16), same)
        kmax_sc[...] = jnp.sqrt(jnp.max(ksq, axis=0, keepdims=True))
        for table_ref, tile_ref in ((tdiag_ref, bdiag_ref), (tprev_ref, bprev_ref)):
            for g in range(GQA_GROUP):
                rows = jnp.broadcast_to(table_ref[g], (blk, 2 * blk))
                tile_ref[g] = pltpu.roll(rows, 0, 1, stride=1, stride_axis=0)[:, :blk]

    lane = lax.broadcasted_iota(jnp.int32, (blk, LANES), 1)
    ones = jnp.ones((SUBLANES, LANES), BF16)
    span = jnp.zeros((1, 1), F32)
    for g in range(GQA_GROUP):
        q2 = q_ref[:, g * LANES:(g + 1) * LANES]
        for c in range(2):
            idx = 2 * g + c
            qp = jnp.where((lane < HEAD_DIM) == (c == 0), q2, jnp.zeros_like(q2))
            qpad_sc[idx] = qp
            qf = qp.astype(F32)
            qsq = lax.dot_general(ones, (qf * qf).astype(BF16), _NT, preferred_element_type=F32)[0:1]
            kmax = jnp.max(kmax_sc[:, c * HEAD_DIM:(c + 1) * HEAD_DIM], axis=1, keepdims=True)
            bqk = BOUND_MARGIN * jnp.sqrt(qsq) * kmax
            bq_sc[idx] = bqk + scal_ref[_S_BMAX + h * GQA_GROUP + g]
            span = jnp.maximum(span, 2.0 * jnp.max(bqk, axis=1, keepdims=True)
                               + scal_ref[_S_BRANGE + h * GQA_GROUP + g])
    acc_sc[...] = jnp.zeros(acc_sc.shape, F32)
    far_bias = lambda g: scal_ref[_S_FAR + h * GQA_GROUP + g]
    n_far = jnp.maximum(qi - 1, 0)

    def k_block(kj):
        return k_ref[pl.ds(pl.multiple_of(kj * blk, blk), blk), :]

    def fast_step(kj, shift_of_group, tile_of_group):
        k_blk = k_block(kj)
        vt = vt_ref[kj]
        qk = lambda idx: lax.dot_general(k_blk, qpad_sc[idx], _NT, preferred_element_type=F32)
        st_next = qk(0)
        for idx in range(ncombo):
            st = st_next
            if idx + 1 < ncombo:
                st_next = qk(idx + 1)
            g = idx // 2
            if tile_of_group is not None:
                st = st + tile_of_group(g)
            bound = bq_sc[idx] if shift_of_group is None else bq_sc[idx] - shift_of_group(g)
            acc_sc[idx] += _dot(vt, jnp.exp2(st - bound).astype(BF16))

    def exact_step(kj, shift_of_group, tile_of_group):
        k_blk = k_block(kj)
        vt = vt_ref[kj]
        for idx in range(ncombo):
            g = idx // 2
            st = lax.dot_general(k_blk, qpad_sc[idx], _NT, preferred_element_type=F32)
            if tile_of_group is not None:
                st = st + tile_of_group(g)
            shift = 0.0 if shift_of_group is None else shift_of_group(g)
            m_old = m_sc[idx]
            m_new = jnp.maximum(m_old, jnp.max(st, axis=0, keepdims=True) + shift)
            p = jnp.exp2(st - (m_new - shift)).astype(BF16)
            acc_sc[idx] = jnp.exp2(m_old - m_new) * acc_sc[idx] + _dot(vt, p)
            m_sc[idx] = m_new

    def run(step):
        def far_step(kj, carry):
            step(kj, far_bias, None)
            return carry
        lax.fori_loop(0, n_far, far_step, 0)

        @pl.when(qi >= 1)
        def _():
            step(qi - 1, None, lambda g: bprev_ref[g])
        step(qi, None, lambda g: bdiag_ref[g])

    def run_exact():
        m_sc[...] = jnp.full(m_sc.shape, -jnp.inf, F32)
        run(exact_step)

    lax.cond(span[0, 0] <= SAFE_EXPONENT_SPAN, lambda: run(fast_step), run_exact)

    for g in range(GQA_GROUP):
        a0 = acc_sc[2 * g]
        a1 = acc_sc[2 * g + 1]
        o = a0[:V_DIM] / a0[V_DIM:V_DIM + 1] - lam * (a1[:V_DIM] / a1[V_DIM:V_DIM + 1])
        o = o * lax.rsqrt(jnp.mean(o * o, axis=0, keepdims=True) + NORM_EPS)
        o = o * sg_ref[...] * out_scale
        o_ref[:, g * V_DIM:(g + 1) * V_DIM] = o.T.astype(BF16)


def _attn_prompt(scal, q_bf, k_bf, vt_ext, bias_diag, bias_prev, sg_col, batch, seq):
    blk = ATTN_BLOCK
    nblk = seq // blk
    vrows = V_DIM + ONES_ROWS
    ncombo = 2 * GQA_GROUP
    grid_spec = pltpu.PrefetchScalarGridSpec(
        num_scalar_prefetch=1,
        grid=(batch, N_KV_HEADS, nblk),
        in_specs=[pl.BlockSpec((blk, GQA_GROUP * 2 * HEAD_DIM), lambda b, h, i, s: (b * nblk + i, h)),
                  pl.BlockSpec((seq, 2 * HEAD_DIM), lambda b, h, i, s: (b, h)),
                  pl.BlockSpec((None, nblk, vrows, blk), lambda b, h, i, s: (b * N_KV_HEADS + h, 0, 0, 0)),
                  pl.BlockSpec((GQA_GROUP, 1, 2 * blk), lambda b, h, i, s: (h, 0, 0)),
                  pl.BlockSpec((GQA_GROUP, 1, 2 * blk), lambda b, h, i, s: (h, 0, 0)),
                  pl.BlockSpec((V_DIM, 1), lambda b, h, i, s: (0, 0))],
        out_specs=pl.BlockSpec((blk, GQA_GROUP * V_DIM), lambda b, h, i, s: (b * nblk + i, h)),
        scratch_shapes=[pltpu.VMEM((ncombo, blk, LANES), BF16),
                        pltpu.VMEM((1, LANES), F32),
                        pltpu.VMEM((ncombo, 1, blk), F32),
                        pltpu.VMEM((ncombo, 1, blk), F32),
                        pltpu.VMEM((ncombo, vrows, blk), F32),
                        pltpu.VMEM((GQA_GROUP, blk, blk), F32),
                        pltpu.VMEM((GQA_GROUP, blk, blk), F32)],
    )
    return pl.pallas_call(
        _attn_prompt_body,
        grid_spec=grid_spec,
        out_shape=jax.ShapeDtypeStruct((batch * seq, N_HEADS * V_DIM), BF16),
        compiler_params=_cparams(("arbitrary", "arbitrary", "arbitrary")),
        name="attn_prompt",
    )(scal, q_bf, k_bf, vt_ext, bias_diag, bias_prev, sg_col)


def _attn_sample_body(n_pages, pt_ref, layer_ref, scal_ref, q_ref, knew_ref, vnew_ref, bias_ref, sg_ref, *refs):
    k_pages = refs[:n_pages]
    v_pages = refs[n_pages:2 * n_pages]
    o_ref, kb_sc, vb_sc = refs[2 * n_pages:]
    page_rows = k_pages[0].shape[2]
    past = n_pages * page_rows
    n_new = knew_ref.shape[1]
    lam = scal_ref[0]
    out_scale = scal_ref[1]
    half = q_ref.shape[1] // 2

    @pl.when(pl.program_id(0) == 0)
    def _():
        tail = kb_sc.shape[0] - past - n_new
        kb_sc[past + n_new:, :] = jnp.zeros((tail, LANES), BF16)
        vb_sc[past + n_new:, :] = jnp.zeros((tail, LANES), BF16)

    for j in range(n_pages):
        kb_sc[j * page_rows:(j + 1) * page_rows, :] = k_pages[j][0, 0].astype(BF16)
        vb_sc[j * page_rows:(j + 1) * page_rows, :] = v_pages[j][0, 0].astype(BF16)
    kb_sc[past:past + n_new, :] = knew_ref[0]
    vb_sc[past:past + n_new, :] = vnew_ref[0]
    s = lax.dot_general(q_ref[0], kb_sc[...], _NT, preferred_element_type=F32) + bias_ref[...]
    p = jnp.exp(s - jnp.max(s, axis=-1, keepdims=True))
    p = p / jnp.sum(p, axis=-1, keepdims=True)
    a = p[:half] - lam * p[half:]
    o = _dot(a.astype(BF16), vb_sc[...])
    o = o * lax.rsqrt(jnp.mean(o * o, axis=-1, keepdims=True) + NORM_EPS)
    o_ref[0] = o * sg_ref[...] * out_scale


def _attn_sample(page_table_flat, layer_idx, scal, q_rows, knew, vnew, bias_s, sg_row, cache_k, cache_v, n_pages):
    nb, rows, _ = q_rows.shape
    n_new = knew.shape[1]
    page_rows = cache_k.shape[2]
    keys = bias_s.shape[1]

    def page_spec(j):
        return pl.BlockSpec((1, 1, page_rows, LANES), lambda b, pt, l: (l[0], pt[b * n_pages + j], 0, 0))

    grid_spec = pltpu.PrefetchScalarGridSpec(
        num_scalar_prefetch=2,
        grid=(nb,),
        in_specs=[pl.BlockSpec(memory_space=pltpu.SMEM),
                  pl.BlockSpec((1, rows, LANES), lambda b, pt, l: (b, 0, 0)),
                  pl.BlockSpec((1, n_new, LANES), lambda b, pt, l: (b, 0, 0)),
                  pl.BlockSpec((1, n_new, LANES), lambda b, pt, l: (b, 0, 0)),
                  pl.BlockSpec((rows, keys), lambda b, pt, l: (0, 0)),
                  pl.BlockSpec((1, V_DIM), lambda b, pt, l: (0, 0))]
                 + [page_spec(j) for j in range(n_pages)] * 2,
        out_specs=pl.BlockSpec((1, rows // 2, V_DIM), lambda b, pt, l: (b, 0, 0)),
        scratch_shapes=[pltpu.VMEM((keys, LANES), BF16),
                        pltpu.VMEM((keys, LANES), BF16)],
    )
    return pl.pallas_call(
        functools.partial(_attn_sample_body, n_pages),
        grid_spec=grid_spec,
        out_shape=jax.ShapeDtypeStruct((nb, rows // 2, V_DIM), F32),
        compiler_params=_cparams(("arbitrary",)),
        name="attn_sample",
    )(page_table_flat, layer_idx, scal, q_rows, knew, vnew, bias_s, sg_row,
      *([cache_k] * n_pages), *([cache_v] * n_pages))


def _lru_coeffs(xc, wa_ref, wi_ref, ba_ref, bi_ref, sp_ref):
    xb = xc.astype(BF16)
    ra, ri = [], []
    for j in range(xc.shape[1] // MXU_DIM):
        cols = slice(j * MXU_DIM, (j + 1) * MXU_DIM)
        ra.append(_dot(xb[:, cols], wa_ref[j]))
        ri.append(_dot(xb[:, cols], wi_ref[j]))
    r = jax.nn.sigmoid(jnp.concatenate(ra, axis=1) + ba_ref[...])
    i = jax.nn.sigmoid(jnp.concatenate(ri, axis=1) + bi_ref[...])
    log_a = -LRU_C * r * sp_ref[...]
    a = jnp.exp(log_a)
    mult = jnp.sqrt(-jnp.tanh(log_a) * (a * a + 1.0))
    return a, mult, i


def _rglru_prompt_body(xr_ref, yr_ref, cw_ref, cb_ref, wa_ref, wi_ref, ba_ref, bi_ref, sp_ref,
                       rnn_ref, hl_ref, xe_sc, a_sc, b_sc, h_sc):
    i = pl.program_id(1)
    tt = xr_ref.shape[0]

    @pl.when(i == 0)
    def _():
        xe_sc[0:SUBLANES, :] = jnp.zeros((SUBLANES, D_MODEL), F32)
        h_sc[...] = jnp.zeros(h_sc.shape, F32)

    xe_sc[SUBLANES:SUBLANES + tt, :] = xr_ref[...]
    xc = cb_ref[...]
    for j in range(CONV_W):
        off = SUBLANES - (CONV_W - 1) + j
        xc = xc + xe_sc[off:off + tt, :] * cw_ref[j:j + 1, :]
    a, mult, gate_i = _lru_coeffs(xc, wa_ref, wi_ref, ba_ref, bi_ref, sp_ref)
    row = lax.broadcasted_iota(jnp.int32, (tt, 1), 0)
    mult = jnp.where((row == 0) & (i == 0), 1.0, mult)
    a_sc[...] = a
    b_sc[...] = mult * gate_i * xc

    def step(t, h):
        h = a_sc[pl.ds(t, 1), :] * h + b_sc[pl.ds(t, 1), :]
        b_sc[pl.ds(t, 1), :] = h
        return h

    h = lax.fori_loop(0, tt, step, h_sc[...], unroll=8)
    h_sc[...] = h
    hl_ref[...] = h
    rnn_ref[...] = (b_sc[...] * jax.nn.gelu(yr_ref[...])).astype(BF16)
    xe_sc[0:SUBLANES, :] = xe_sc[tt:tt + SUBLANES, :]


def _rglru_prompt(xr, yr, cw, cb, wa_bd, wi_bd, ba, bi, sp, batch, seq):
    tt = min(TIME_TILE, seq)
    nt = seq // tt
    row = pl.BlockSpec((tt, D_MODEL), lambda b, i: (b * nt + i, 0))
    nbd = D_MODEL // MXU_DIM
    return pl.pallas_call(
        _rglru_prompt_body,
        grid=(batch, nt),
        in_specs=[row, row,
                  pl.BlockSpec((CONV_W, D_MODEL), lambda b, i: (0, 0)),
                  pl.BlockSpec((1, D_MODEL), lambda b, i: (0, 0)),
                  pl.BlockSpec((nbd, MXU_DIM, MXU_DIM), lambda b, i: (0, 0, 0)),
                  pl.BlockSpec((nbd, MXU_DIM, MXU_DIM), lambda b, i: (0, 0, 0)),
                  pl.BlockSpec((1, D_MODEL), lambda b, i: (0, 0)),
                  pl.BlockSpec((1, D_MODEL), lambda b, i: (0, 0)),
                  pl.BlockSpec((1, D_MODEL), lambda b, i: (0, 0))],
        out_specs=[row, pl.BlockSpec((None, 1, D_MODEL), lambda b, i: (b, 0, 0))],
        out_shape=[jax.ShapeDtypeStruct((batch * seq, D_MODEL), BF16),
                   jax.ShapeDtypeStruct((batch, 1, D_MODEL), F32)],
        scratch_shapes=[pltpu.VMEM((tt + SUBLANES, D_MODEL), F32),
                        pltpu.VMEM((tt, D_MODEL), F32),
                        pltpu.VMEM((tt, D_MODEL), F32),
                        pltpu.VMEM((1, D_MODEL), F32)],
        compiler_params=_cparams(("arbitrary", "arbitrary")),
        name="rglru_prompt",
    )(xr, yr, cw, cb, wa_bd, wi_bd, ba, bi, sp)


def _rglru_sample_body(n_steps, first_is_start, xr_ref, yr_ref, buf_ref, h0_ref, cw_ref, cb_ref, wa_ref, wi_ref, ba_ref, bi_ref,
                       sp_ref, rnn_ref, hl_ref):
    nb = h0_ref.shape[0]
    taps = [buf_ref[j] for j in range(CONV_W - 1)] + [xr_ref[t * nb:(t + 1) * nb, :] for t in range(n_steps)]
    h = h0_ref[...]
    for t in range(n_steps):
        xc = cb_ref[...]
        for j in range(CONV_W):
            xc = xc + taps[t + j] * cw_ref[j:j + 1, :]
        a, mult, gate_i = _lru_coeffs(xc, wa_ref, wi_ref, ba_ref, bi_ref, sp_ref)
        if first_is_start and t == 0:
            mult = jnp.ones_like(mult)
        h = a * h + mult * gate_i * xc
        rnn_ref[t * nb:(t + 1) * nb, :] = (h * jax.nn.gelu(yr_ref[t * nb:(t + 1) * nb, :])).astype(BF16)
    hl_ref[...] = h


def _rglru_sample(xr, yr, buf, h0, cw, cb, wa_bd, wi_bd, ba, bi, sp, n_steps, first_is_start):
    t = xr.shape[0]
    return pl.pallas_call(
        functools.partial(_rglru_sample_body, n_steps, first_is_start),
        out_shape=[jax.ShapeDtypeStruct((t, D_MODEL), BF16),
                   jax.ShapeDtypeStruct(h0.shape, F32)],
        compiler_params=pltpu.CompilerParams(vmem_limit_bytes=VMEM_LIMIT_BYTES),
        name="rglru_sample",
    )(xr, yr, buf, h0, cw, cb, wa_bd, wi_bd, ba, bi, sp)


def _route(logits):
    lane = lax.broadcasted_iota(jnp.int32, logits.shape, 1).astype(F32)
    none = float(ROUTER_LANES)
    is_group = lane < N_GROUPS
    lg = jnp.where(is_group, logits, -jnp.inf)
    eg = jnp.exp(lg - jnp.max(lg, axis=-1, keepdims=True))
    gp = eg / jnp.sum(eg, axis=-1, keepdims=True)
    gw = jnp.max(gp, axis=-1, keepdims=True)
    gi = jnp.min(jnp.where(is_group & (gp == gw), lane, none), axis=-1, keepdims=True)
    first = N_GROUPS + gi * EXPERTS_PER_GROUP
    in_group = (lane >= first) & (lane < first + EXPERTS_PER_GROUP)
    le = jnp.where(in_group, logits, -jnp.inf)
    ee = jnp.exp(le - jnp.max(le, axis=-1, keepdims=True))
    pe = ee / jnp.sum(ee, axis=-1, keepdims=True)
    p1 = jnp.max(pe, axis=-1, keepdims=True)
    i1 = jnp.min(jnp.where(in_group & (pe == p1), lane, none), axis=-1, keepdims=True)
    rest = in_group & (lane != i1)
    pr = jnp.where(rest, pe, -1.0)
    p2 = jnp.max(pr, axis=-1, keepdims=True)
    i2 = jnp.min(jnp.where(rest & (pr == p2), lane, none), axis=-1, keepdims=True)
    den = p1 + p2
    w1 = gw * (p1 / den)
    w2 = gw * (p2 / den)
    rec = jnp.where(lane == 0.0, i1 - N_GROUPS, 0.0)
    rec = jnp.where(lane == 1.0, i2 - N_GROUPS, rec)
    rec = jnp.where(lane == 2.0, w1, rec)
    rec = jnp.where(lane == 3.0, w2, rec)
    return rec, jnp.where(lane == i1, 1.0, 0.0), jnp.where(lane == i2, 1.0, 0.0)


def _merge_body(x_ref, attn_ref, rnn_ref, ga_ref, gr_ref, g1_ref, sc2_ref, sh2_ref, n2g_ref,
                wpa_ref, wpr_ref, wo_ref, wrh_ref, wrl_ref, br_ref, ltri_ref, cnt_in_ref,
                x1_ref, u2_ref, rec_ref, cnt_ref, cnt_sc):
    pa = _dot(attn_ref[...], wpa_ref[...])
    pr = _dot(rnn_ref[...], wpr_ref[...])
    merged = jax.nn.sigmoid(ga_ref[...]) * pa + jax.nn.sigmoid(gr_ref[...]) * pr
    x1 = x_ref[...] + g1_ref[...] * _dot(merged.astype(BF16), wo_ref[...])
    x1_ref[...] = x1
    u2 = _rms_rows(x1, n2g_ref[...]) * (1.0 + sc2_ref[...]) + sh2_ref[...]
    u2_ref[...] = u2
    u_hi = u2.astype(BF16)
    u_lo = (u2 - u_hi.astype(F32)).astype(BF16)
    logits = (_dot(u_hi, wrh_ref[...]) + _dot(u_lo, wrh_ref[...]) + _dot(u_hi, wrl_ref[...])) + br_ref[...]
    rec, hot1, hot2 = _route(logits)

    @pl.when(pl.program_id(0) == 0)
    def _():
        cnt_sc[...] = cnt_in_ref[...]
    cnt = cnt_sc[...]
    tot1 = jnp.sum(hot1, axis=0, keepdims=True)
    tot2 = jnp.sum(hot2, axis=0, keepdims=True)
    before1 = _dot(ltri_ref[...], hot1.astype(BF16)) + cnt
    before2 = _dot(ltri_ref[...], hot2.astype(BF16)) + (cnt + tot1)
    rank1 = jnp.sum(hot1 * before1, axis=-1, keepdims=True)
    rank2 = jnp.sum(hot2 * before2, axis=-1, keepdims=True)
    cnt_sc[...] = cnt + tot1 + tot2
    cnt_ref[...] = cnt_sc[...]
    lane = lax.broadcasted_iota(jnp.int32, rec.shape, 1)
    rec = jnp.where(lane == 4, rank1, rec)
    rec_ref[...] = jnp.where(lane == 5, rank2, rec)


def _merge(x, attn, rnn, ga, gr, mods, rows_per_group, cnt_in, n2g, wpa, wpr, wo, wr_hi, wr_lo, br):
    t = x.shape[0]
    tile = min(TOKEN_TILE, t)
    row = lambda n: pl.BlockSpec((tile, n), lambda i: (i, 0))
    sq = _resident((D_MODEL, D_MODEL))
    r = jnp.arange(tile)
    ltri = (r[None, :] < r[:, None]).astype(BF16)
    return pl.pallas_call(
        _merge_body,
        grid=(t // tile,),
        in_specs=[row(D_MODEL), row(D_MODEL), row(D_MODEL), row(D_MODEL), row(D_MODEL),
                  _mod_spec(mods, tile, rows_per_group, 2),
                  _mod_spec(mods, tile, rows_per_group, 4),
                  _mod_spec(mods, tile, rows_per_group, 3),
                  _resident((1, D_MODEL)), sq, sq, sq,
                  _resident((D_MODEL, ROUTER_LANES)), _resident((D_MODEL, ROUTER_LANES)),
                  _resident((1, ROUTER_LANES)), _resident((tile, tile)), _resident((1, ROUTER_LANES))],
        out_specs=[row(D_MODEL), row(D_MODEL), row(ROUTER_LANES), _resident((1, ROUTER_LANES))],
        out_shape=[jax.ShapeDtypeStruct((t, D_MODEL), F32),
                   jax.ShapeDtypeStruct((t, D_MODEL), F32),
                   jax.ShapeDtypeStruct((t, ROUTER_LANES), F32),
                   jax.ShapeDtypeStruct((1, ROUTER_LANES), F32)],
        scratch_shapes=[pltpu.VMEM((1, ROUTER_LANES), F32)],
        compiler_params=_cparams(("arbitrary",)),
        name="merge",
    )(x, attn, rnn, ga, gr, mods, mods, mods, n2g, wpa, wpr, wo, wr_hi, wr_lo, br, ltri, cnt_in)


ROW_UNROLL = 8


def _for_rows(n_rows, fn):
    def body(j, carry):
        for s in range(ROW_UNROLL):
            fn(j * ROW_UNROLL + s)
        return carry
    lax.fori_loop(0, n_rows // ROW_UNROLL, body, 0)


def _dispatch_body(t_all, n_first, pos_ref, ua_ref, ub_ref, init_hbm, us_hbm, sem):
    del init_hbm
    i = pl.program_id(0)

    def scatter_tile(u_ref, base):
        tile = u_ref.shape[0]

        def issue(r):
            for slot in range(2):
                pltpu.make_async_copy(u_ref.at[pl.ds(r, 1)],
                                      us_hbm.at[pl.ds(pos_ref[slot * t_all + base + r], 1)], sem.at[0]).start()
        _for_rows(tile, issue)
        for _ in range(2):
            pltpu.make_async_copy(u_ref, us_hbm.at[pl.ds(0, tile)], sem.at[0]).wait()

    @pl.when(i < n_first)
    def _():
        scatter_tile(ua_ref, i * ua_ref.shape[0])

    @pl.when(i >= n_first)
    def _():
        scatter_tile(ub_ref, n_first * ua_ref.shape[0] + (i - n_first) * ub_ref.shape[0])


def _dispatch(pos_flat, u_first, u_second, n_sorted_rows):
    t_all = u_first.shape[0] + u_second.shape[0]
    tile_a = min(TOKEN_TILE, u_first.shape[0])
    tile_b = min(TOKEN_TILE, u_second.shape[0])
    n_a = u_first.shape[0] // tile_a
    n_b = u_second.shape[0] // tile_b
    assert n_a * tile_a == u_first.shape[0] and n_b * tile_b == u_second.shape[0]
    assert tile_a % ROW_UNROLL == 0 and tile_b % ROW_UNROLL == 0
    grid_spec = pltpu.PrefetchScalarGridSpec(
        num_scalar_prefetch=1,
        grid=(n_a + n_b,),
        in_specs=[pl.BlockSpec((tile_a, D_MODEL), lambda i, pos: (jnp.minimum(i, n_a - 1), 0)),
                  pl.BlockSpec((tile_b, D_MODEL), lambda i, pos: (jnp.maximum(i - n_a, 0), 0)),
                  pl.BlockSpec(memory_space=pl.ANY)],
        out_specs=pl.BlockSpec(memory_space=pl.ANY),
        scratch_shapes=[pltpu.SemaphoreType.DMA((1,))],
    )
    return pl.pallas_call(
        functools.partial(_dispatch_body, t_all, n_a),
        grid_spec=grid_spec,
        out_shape=jax.ShapeDtypeStruct((n_sorted_rows, D_MODEL), F32),
        input_output_aliases={3: 0},
        compiler_params=_cparams(("arbitrary",)),
        name="dispatch",
    )(pos_flat, u_first, u_second, jnp.zeros((n_sorted_rows, D_MODEL), F32))


def _experts_body(te_ref, nv_ref, last_ref, layer_ref, u_ref, wg_ref, wu_ref, wd_ref, y_ref, wg_sc, wu_sc, wd_sc):
    del last_ref, layer_ref
    i = pl.program_id(0)

    @pl.when((i == 0) | (te_ref[i] != te_ref[jnp.maximum(i - 1, 0)]))
    def _():
        wg_sc[...] = wg_ref[...].astype(BF16)
        wu_sc[...] = wu_ref[...].astype(BF16)
        wd_sc[...] = wd_ref[...].astype(BF16)

    @pl.when(nv_ref[i] > 0)
    def _():
        u = u_ref[...].astype(BF16)
        hg = _dot(u, wg_sc[...])
        hid = (hg * jax.nn.sigmoid(hg)) * _dot(u, wu_sc[...])
        y_ref[...] = _dot(hid.astype(BF16), wd_sc[...])

    @pl.when(nv_ref[i] == 0)
    def _():
        y_ref[...] = jnp.zeros(y_ref.shape, F32)


def _experts(tile_expert, tile_valid, last_tile, layer_idx, u_sorted, w_gate, w_up, w_down):
    tile = EXPERT_TILE
    n_tiles = tile_expert.shape[0]
    wspec = lambda a, b: pl.BlockSpec((None, None, a, b), lambda i, te, nv, last, lyr: (lyr[0], te[i], 0, 0))
    grid_spec = pltpu.PrefetchScalarGridSpec(
        num_scalar_prefetch=4,
        grid=(n_tiles,),
        in_specs=[pl.BlockSpec((tile, D_MODEL), lambda i, te, nv, last, lyr: (jnp.minimum(i, last[0]), 0)),
                  wspec(D_MODEL, D_EXPERT), wspec(D_MODEL, D_EXPERT), wspec(D_EXPERT, D_MODEL)],
        out_specs=pl.BlockSpec((tile, D_MODEL), lambda i, te, nv, last, lyr: (i, 0)),
        scratch_shapes=[pltpu.VMEM((D_MODEL, D_EXPERT), BF16),
                        pltpu.VMEM((D_MODEL, D_EXPERT), BF16),
                        pltpu.VMEM((D_EXPERT, D_MODEL), BF16)],
    )
    return pl.pallas_call(
        _experts_body,
        grid_spec=grid_spec,
        out_shape=jax.ShapeDtypeStruct(u_sorted.shape, F32),
        compiler_params=_cparams(("arbitrary",)),
        name="experts",
    )(tile_expert, tile_valid, last_tile, layer_idx, u_sorted, w_gate, w_up, w_down)


def _dispatch_plan(rec_all, counts_row):
    t_all = rec_all.shape[0]
    tile = EXPERT_TILE
    n_tiles = -(-2 * t_all // tile) + N_EXPERTS
    counts = counts_row[0, N_GROUPS:N_GROUPS + N_EXPERTS].astype(jnp.int32)
    tiles_per_e = (counts + tile - 1) // tile
    tile_end = jnp.cumsum(tiles_per_e)
    tile_start = tile_end - tiles_per_e
    experts = jnp.arange(N_EXPERTS, dtype=jnp.int32)
    e = rec_all[:, 0:2].astype(jnp.int32)
    rank = rec_all[:, 4:6].astype(jnp.int32)
    row_start = jnp.sum(jnp.where(e[:, :, None] == experts, tile_start * tile, 0), axis=-1)
    pos_flat = (row_start + rank).T.reshape(-1)
    n_used = tile_end[-1]
    tile_ids = jnp.arange(n_tiles, dtype=jnp.int32)
    clamped = jnp.minimum(tile_ids, n_used - 1)
    te = jnp.sum((clamped[:, None] >= tile_end[None, :]).astype(jnp.int32), axis=-1)
    te = jnp.minimum(te, N_EXPERTS - 1)
    first = jnp.sum(jnp.where(te[:, None] == experts, tile_start, 0), axis=-1)
    cnt_t = jnp.sum(jnp.where(te[:, None] == experts, counts, 0), axis=-1)
    valid = jnp.clip(cnt_t - (tile_ids - first) * tile, 0, tile)
    valid = jnp.where(tile_ids < n_used, valid, 0).astype(jnp.int32)
    return pos_flat, te, valid, (n_used - 1).reshape(1).astype(jnp.int32), n_tiles * tile


def _combine_body(t_all, row_offset, pos_ref, x_ref, rec_ref, g2_ref, ys_hbm, o_ref, ybuf, sem):
    i = pl.program_id(0)
    n = pl.num_programs(0)
    tile = x_ref.shape[0]
    slot = i % 2

    def start_gather(tile_idx, buf):
        base = row_offset + tile_idx * tile

        def issue(r):
            for k in range(2):
                pltpu.make_async_copy(ys_hbm.at[pl.ds(pos_ref[k * t_all + base + r], 1)],
                                      ybuf.at[buf, k, pl.ds(r, 1)], sem.at[buf]).start()
        _for_rows(tile, issue)

    @pl.when(i == 0)
    def _():
        start_gather(0, 0)

    @pl.when(i + 1 < n)
    def _():
        start_gather(i + 1, 1 - slot)

    for k in range(2):
        pltpu.make_async_copy(ys_hbm.at[pl.ds(0, tile)], ybuf.at[slot, k], sem.at[slot]).wait()
    w1 = rec_ref[:, 2:3]
    w2 = rec_ref[:, 3:4]
    o_ref[...] = x_ref[...] + g2_ref[...] * (w1 * ybuf[slot, 0] + w2 * ybuf[slot, 1])


def _combine(pos_flat, x1, rec, y_sorted, mods, rows_per_group, row_offset, t_all):
    t = x1.shape[0]
    tile = min(TOKEN_TILE, t)
    assert tile % ROW_UNROLL == 0
    if mods.ndim == 3:
        tiles_per_group = rows_per_group // tile
        g2_spec = pl.BlockSpec((None, 1, D_MODEL), lambda i, pos: (i // tiles_per_group, 0, 5))
    else:
        g2_spec = pl.BlockSpec((tile, D_MODEL), lambda i, pos: (i, 5))
    grid_spec = pltpu.PrefetchScalarGridSpec(
        num_scalar_prefetch=1,
        grid=(t // tile,),
        in_specs=[pl.BlockSpec((tile, D_MODEL), lambda i, pos: (i, 0)),
                  pl.BlockSpec((tile, ROUTER_LANES), lambda i, pos: (i, 0)),
                  g2_spec,
                  pl.BlockSpec(memory_space=pl.ANY)],
        out_specs=pl.BlockSpec((tile, D_MODEL), lambda i, pos: (i, 0)),
        scratch_shapes=[pltpu.VMEM((2, 2, tile, D_MODEL), F32),
                        pltpu.SemaphoreType.DMA((2,))],
    )
    return pl.pallas_call(
        functools.partial(_combine_body, t_all, row_offset),
        grid_spec=grid_spec,
        out_shape=jax.ShapeDtypeStruct((t, D_MODEL), F32),
        compiler_params=_cparams(("arbitrary",)),
        name="combine",
    )(pos_flat, x1, rec, mods, y_sorted)


def _block_diag(w):
    per = MXU_DIM // RNN_BLOCK
    w4 = w.reshape(N_RNN_BLOCKS // per, per, RNN_BLOCK, RNN_BLOCK)
    eye = jnp.eye(per, dtype=w.dtype)
    return jnp.einsum("jmcd,mn->jmcnd", w4, eye).reshape(N_RNN_BLOCKS // per, MXU_DIM, MXU_DIM)


def kernel(x_prompt, x_sample, cache_k, cache_v, state_rnn, state_conv, page_table, c_prompt, c_sample, rel_bias, w_mod, b_mod, norm1_g, norm2_g, w_in, q_norm_g, k_norm_g, lam_q1, lam_k1, lam_q2, lam_k2, subln_g, conv_w, conv_b, lru_wa, lru_ba, lru_wi, lru_bi, lru_lambda, w_pa, w_pr, w_o, w_rg, b_rg, w_re, b_re, w_e_gate, w_e_up, w_e_down):
    bp, tp, _ = x_prompt.shape
    bs, ts, _ = x_sample.shape
    depth = w_in.shape[0]
    n_pool = cache_k.shape[1]
    n_pages = page_table.shape[1]
    past_len = n_pages * PAGE_SIZE
    t_p = bp * tp
    t_s = bs * ts
    t_all = t_p + t_s
    blk = ATTN_BLOCK
    nblk = tp // blk
    assert tp % blk == 0 and tp % TIME_TILE == 0 and t_p % TOKEN_TILE == 0
    assert t_s % SUBLANES == 0 and t_p % min(TOKEN_TILE, t_s) == 0
    log2e = math.log2(math.e)

    rows_c = -(-(bp + bs) // SUBLANES) * SUBLANES
    c_all = jnp.zeros((rows_c, D_MODEL), F32).at[:bp].set(c_prompt).at[bp:bp + bs].set(c_sample)
    mods = _mods(c_all, w_mod, b_mod)
    mods_p = mods[:, :bp].reshape(depth, bp, 1, 6 * D_MODEL)
    mods_s = jnp.tile(mods[:, bp:bp + bs], (1, ts, 1))

    w_in_bf = w_in.astype(BF16)
    w_pa_bf, w_pr_bf, w_o_bf = w_pa.astype(BF16), w_pr.astype(BF16), w_o.astype(BF16)
    qg_row = jnp.tile(q_norm_g, (1, Q_COLS // HEAD_DIM)).reshape(depth, 1, Q_COLS)
    kg_row = jnp.tile(k_norm_g, (1, K_COLS // HEAD_DIM)).reshape(depth, 1, K_COLS)
    seg = jnp.arange(MXU_DIM) // HEAD_DIM
    chunk_p = ((seg[:, None] == seg[None, :]).astype(F32) / HEAD_DIM).astype(BF16)
    wr = jnp.concatenate([w_rg, w_re], axis=-1)
    wr = jnp.pad(wr, ((0, 0), (0, 0), (0, ROUTER_LANES - wr.shape[-1])))
    wr_hi = wr.astype(BF16)
    wr_lo = (wr - wr_hi.astype(F32)).astype(BF16)
    br = jnp.pad(jnp.concatenate([b_rg, b_re], axis=-1), ((0, 0), (0, ROUTER_LANES - N_GROUPS - N_EXPERTS)))
    br = br.reshape(depth, 1, ROUTER_LANES)
    wa_bd = jax.vmap(_block_diag)(lru_wa).astype(BF16)
    wi_bd = jax.vmap(_block_diag)(lru_wi).astype(BF16)
    ba = lru_ba.reshape(depth, 1, D_MODEL)
    bi = lru_bi.reshape(depth, 1, D_MODEL)
    sp = jax.nn.softplus(-lru_lambda.astype(F32)).reshape(depth, 1, D_MODEL)
    cb = conv_b.reshape(depth, 1, D_MODEL)
    lam_f = lambda a, b: jnp.exp(jnp.sum(a.astype(F32) * b.astype(F32), axis=-1))
    lam_base = lam_f(lam_q1, lam_k1) - lam_f(lam_q2, lam_k2)

    bias_diag = _bias_table(rel_bias, blk, 0) * log2e
    bias_prev = _bias_table(rel_bias, blk, blk) * log2e
    rb = rel_bias.astype(F32) * log2e
    bias_scal = jnp.concatenate([rb[NUM_BUCKETS - 1],
                                 jnp.max(rb, axis=0), jnp.max(rb, axis=0) - jnp.min(rb, axis=0)])

    page_rows = PAGE_SIZE * N_KV_HEADS
    n_new = -(-ts * N_KV_HEADS // (2 * SUBLANES)) * (2 * SUBLANES)
    keys_s = n_pages * page_rows + -(-n_new // LANES) * LANES
    n_pos = past_len + ts
    dist = jnp.arange(-(ts - 1), n_pos, dtype=jnp.int32)
    flipped = _bias_of_distance(rel_bias, dist)[:, ::-1]
    by_token = jnp.stack([flipped[:, ts - 1 - t:ts - 1 - t + n_pos] for t in range(ts)], axis=1)
    by_row = jnp.transpose(by_token.reshape(N_KV_HEADS, GQA_GROUP, ts, n_pos), (0, 2, 1, 3))
    own_head = (jnp.arange(N_KV_HEADS)[:, None, None, None, None]
                == jnp.arange(N_KV_HEADS)[None, None, None, None, :])
    bias_rows = jnp.where(own_head, by_row[..., None], MASK_VALUE)
    bias_rows = bias_rows.reshape(N_KV_HEADS * ts * GQA_GROUP, n_pos * N_KV_HEADS)
    bias_rows = jnp.pad(bias_rows, ((0, 0), (0, keys_s - n_pos * N_KV_HEADS)), constant_values=MASK_VALUE)
    bias_s = jnp.concatenate([bias_rows, bias_rows], axis=0)
    pt_flat = page_table.reshape(-1).astype(jnp.int32)
    cache_k_rows = cache_k.reshape(depth, n_pool, page_rows, LANES)
    cache_v_rows = cache_v.reshape(depth, n_pool, page_rows, LANES)

    xp = x_prompt.reshape(t_p, D_MODEL)
    xs = jnp.swapaxes(x_sample, 0, 1).reshape(t_s, D_MODEL)
    outs = [[] for _ in range(8)]
    for l in range(depth):
        lam_init = 0.8 - 0.6 * math.exp(-0.3 * l)
        lam = lam_base[l] + lam_init
        sg_col = subln_g[l].astype(F32).reshape(V_DIM, 1)
        sg_row = subln_g[l].astype(F32).reshape(1, V_DIM)
        scal_s = jnp.stack([lam, jnp.asarray(1.0 - lam_init, F32)]).astype(F32)
        scal_p = jnp.concatenate([scal_s, bias_scal])
        lp = (norm1_g[l].reshape(1, D_MODEL), w_in_bf[l], qg_row[l], kg_row[l], chunk_p)
        rg = (conv_w[l], cb[l], wa_bd[l], wi_bd[l], ba[l], bi[l], sp[l])
        mg = (norm2_g[l].reshape(1, D_MODEL), w_pa_bf[l], w_pr_bf[l], w_o_bf[l], wr_hi[l], wr_lo[l], br[l])

        q, k, kb, v, xr, yr, ga, gr, vt_ext = _in_proj(xp, mods_p[l], tp, ATTN_SCALE * log2e, blk, *lp)
        attn = _attn_prompt(scal_p, q, kb, vt_ext, bias_diag, bias_prev, sg_col, bp, tp)
        rnn, h_last = _rglru_prompt(xr, yr, *rg, bp, tp)
        x1p, u2p, recp, cnt = _merge(xp, attn, rnn, ga, gr, mods_p[l], tp,
                                     jnp.zeros((1, ROUTER_LANES), F32), *mg)
        outs[0].append(k.reshape(bp, tp, N_KV_HEADS, 2 * HEAD_DIM))
        outs[1].append(v.reshape(bp, tp, N_KV_HEADS, V_DIM))
        outs[2].append(h_last.reshape(bp, D_MODEL))
        outs[3].append(xr.reshape(bp, tp, D_MODEL)[:, tp - (CONV_W - 1):])

        q, k, kb, v, xr, yr, ga, gr = _in_proj(xs, mods_s[l], ts, ATTN_SCALE, None, *lp)
        q6 = q.reshape(ts, bs, N_KV_HEADS, GQA_GROUP, 2, HEAD_DIM)
        q6 = jnp.transpose(q6, (1, 4, 2, 0, 3, 5))
        zq = jnp.zeros_like(q6[:, 0])
        q_rows = jnp.stack([jnp.concatenate([q6[:, 0], zq], axis=-1),
                            jnp.concatenate([zq, q6[:, 1]], axis=-1)], axis=1)
        q_rows = q_rows.reshape(bs, 2 * N_KV_HEADS * ts * GQA_GROUP, 2 * HEAD_DIM)
        new_rows = lambda z: jnp.pad(jnp.swapaxes(z.reshape(ts, bs, N_KV_HEADS, LANES), 0, 1)
                                     .reshape(bs, ts * N_KV_HEADS, LANES),
                                     ((0, 0), (0, n_new - ts * N_KV_HEADS), (0, 0)))
        o_s = _attn_sample(pt_flat, jnp.full((1,), l, jnp.int32), scal_s, q_rows, new_rows(kb),
                           new_rows(v.astype(BF16)), bias_s, sg_row, cache_k_rows, cache_v_rows, n_pages)
        attn = jnp.transpose(o_s.reshape(bs, N_KV_HEADS, ts, GQA_GROUP, V_DIM), (2, 0, 1, 3, 4))
        attn = attn.reshape(t_s, N_HEADS * V_DIM).astype(BF16)
        buf = jnp.swapaxes(state_conv[l], 0, 1)
        rnn, h_last = _rglru_sample(xr, yr, buf, state_rnn[l], *rg, ts, past_len == 0)
        x1s, u2s, recs, cnt = _merge(xs, attn, rnn, ga, gr, mods_s[l], ts, cnt, *mg)
        to_bt = lambda z, n: jnp.swapaxes(z.reshape(ts, bs, n), 0, 1)
        outs[4].append(to_bt(k, K_COLS).reshape(bs, ts, N_KV_HEADS, 2 * HEAD_DIM))
        outs[5].append(to_bt(v, V_COLS).reshape(bs, ts, N_KV_HEADS, V_DIM))
        outs[6].append(h_last)
        xin = jnp.concatenate([state_conv[l], to_bt(xr, D_MODEL)], axis=1)
        outs[7].append(xin[:, -(CONV_W - 1):])

        pos_flat, te, valid, last_tile, n_sorted = _dispatch_plan(jnp.concatenate([recp, recs], axis=0), cnt)
        u_sorted = _dispatch(pos_flat, u2p, u2s, n_sorted)
        y_sorted = _experts(te, valid, last_tile, jnp.full((1,), l, jnp.int32), u_sorted,
                            w_e_gate, w_e_up, w_e_down)
        xp = _combine(pos_flat, x1p, recp, y_sorted, mods_p[l], tp, 0, t_all)
        xs = _combine(pos_flat, x1s, recs, y_sorted, mods_s[l], ts, t_p, t_all)

    y_sample = jnp.swapaxes(xs.reshape(ts, bs, D_MODEL), 0, 1)
    stack = lambda i: jnp.stack(outs[i])
    return (xp.reshape(bp, tp, D_MODEL), y_sample, stack(0), stack(1), stack(2), stack(3),
            stack(4), stack(5), stack(6), stack(7))
```

```python
import functools
import math

import jax
import jax.numpy as jnp
from jax import lax
from jax.experimental import pallas as pl
from jax.experimental.pallas import tpu as pltpu

F32 = jnp.float32
BF16 = jnp.bfloat16

D_MODEL = 1024
HEAD_DIM = 64
N_HEADS = 8
N_KV_HEADS = 4
GQA_GROUP = 2
V_DIM = 128
ATTN_SCALE = HEAD_DIM ** -0.5
MASK_VALUE = -1e30
NUM_BUCKETS = 32
MAX_EXACT = 16
MAX_DISTANCE = 128
N_RNN_BLOCKS = 16
RNN_BLOCK = 64
CONV_W = 4
LRU_C = 8.0
N_GROUPS = 4
EXPERTS_PER_GROUP = 8
N_EXPERTS = 32
D_EXPERT = 512
NORM_EPS = 1e-6
PAGE_SIZE = 128
Q_COLS = 1024
K_COLS = 512
V_COLS = 512
D_IN = 6144

LANES = 128
SUBLANES = 8
MXU_DIM = 256
VMEM_LIMIT_BYTES = 56 * 1024 * 1024

TOKEN_TILE = 256
ATTN_BLOCK = 512
TIME_TILE = 256
EXPERT_TILE = 256
ROUTER_LANES = 128
ONES_ROWS = 16

_NT = (((1,), (1,)), ((), ()))


def _cparams(sem):
    return pltpu.CompilerParams(dimension_semantics=sem, vmem_limit_bytes=VMEM_LIMIT_BYTES)


def _dot(a, b):
    return jnp.dot(a, b, preferred_element_type=F32)


def _resident(shape):
    n = len(shape)
    return pl.BlockSpec(shape, lambda *_: (0,) * n)


_MODS_COLS = 1536


def _mods_body(c_ref, w_ref, b_ref, o_ref):
    c = c_ref[...]
    s = c * jax.nn.sigmoid(c)
    o_ref[...] = jnp.dot(s, w_ref[...], precision=lax.Precision.HIGHEST,
                         preferred_element_type=F32) + b_ref[...]


def _mods(c_all, w_mod, b_mod):
    rows = c_all.shape[0]
    depth, _, n = w_mod.shape
    return pl.pallas_call(
        _mods_body,
        grid=(depth, n // _MODS_COLS),
        in_specs=[pl.BlockSpec((rows, D_MODEL), lambda l, j: (0, 0)),
                  pl.BlockSpec((None, D_MODEL, _MODS_COLS), lambda l, j: (l, 0, j)),
                  pl.BlockSpec((None, 1, _MODS_COLS), lambda l, j: (l, 0, j))],
        out_specs=pl.BlockSpec((None, rows, _MODS_COLS), lambda l, j: (l, 0, j)),
        out_shape=jax.ShapeDtypeStruct((depth, rows, n), F32),
        compiler_params=_cparams(("arbitrary", "arbitrary")),
        name="mods",
    )(c_all, w_mod, b_mod.reshape(depth, 1, n))


def _rms_rows(x, g):
    ms = jnp.mean(x * x, axis=-1, keepdims=True)
    return x * lax.rsqrt(ms + NORM_EPS) * g


def _chunk_mean_square(z, p):
    parts = []
    for j in range(z.shape[1] // MXU_DIM):
        zz = z[:, j * MXU_DIM:(j + 1) * MXU_DIM]
        parts.append(_dot((zz * zz).astype(BF16), p))
    return jnp.concatenate(parts, axis=1)


def _in_proj_body(q_scale, x_ref, sc_ref, sh_ref, g_ref, w_ref, qg_ref, kg_ref, p_ref,
                  q_ref, k_ref, kb_ref, v_ref, xr_ref, yr_ref, ga_ref, gr_ref, vt_ref=None):
    u = (_rms_rows(x_ref[...], g_ref[...]) * (1.0 + sc_ref[...]) + sh_ref[...]).astype(BF16)
    p = p_ref[...]
    zq = _dot(u, w_ref[:, 0:Q_COLS])
    q = zq * lax.rsqrt(_chunk_mean_square(zq, p) + NORM_EPS) * qg_ref[...]
    q_ref[...] = (q * q_scale).astype(BF16)
    zk = _dot(u, w_ref[:, Q_COLS:Q_COLS + K_COLS])
    k = zk * lax.rsqrt(_chunk_mean_square(zk, p) + NORM_EPS) * kg_ref[...]
    k_ref[...] = k
    kb_ref[...] = k.astype(BF16)
    c0 = Q_COLS + K_COLS
    v = _dot(u, w_ref[:, c0:c0 + V_COLS])
    v_ref[...] = v
    if vt_ref is not None:
        row = lax.broadcasted_iota(jnp.int32, (ONES_ROWS, v.shape[0]), 0)
        ones_rows = jnp.where(row == 0, 1.0, 0.0).astype(BF16)
        for h in range(N_KV_HEADS):
            vt_ref[h, 0:V_DIM, :] = v[:, h * V_DIM:(h + 1) * V_DIM].T.astype(BF16)
            vt_ref[h, V_DIM:V_DIM + ONES_ROWS, :] = ones_rows
    c0 += V_COLS
    xr_ref[...] = _dot(u, w_ref[:, c0:c0 + D_MODEL])
    c0 += D_MODEL
    yr_ref[...] = _dot(u, w_ref[:, c0:c0 + D_MODEL])
    c0 += D_MODEL
    ga_ref[...] = _dot(u, w_ref[:, c0:c0 + D_MODEL])
    c0 += D_MODEL
    gr_ref[...] = _dot(u, w_ref[:, c0:c0 + D_MODEL])


def _mod_spec(mods, tile, rows_per_group, col_block):
    if mods.ndim == 3:
        tiles_per_group = rows_per_group // tile
        return pl.BlockSpec((None, 1, D_MODEL), lambda i: (i // tiles_per_group, 0, col_block))
    return pl.BlockSpec((tile, D_MODEL), lambda i: (i, col_block))


def _in_proj(x, mods, rows_per_group, q_scale, vt_block, layer, norm_g, w_in_bf, qg_row, kg_row, chunk_p):
    t = x.shape[0]
    tile = min(TOKEN_TILE, t)
    row = lambda n: pl.BlockSpec((tile, n), lambda i: (i, 0))
    outs = [(Q_COLS, BF16), (K_COLS, F32), (K_COLS, BF16), (V_COLS, F32),
            (D_MODEL, F32), (D_MODEL, F32), (D_MODEL, F32), (D_MODEL, F32)]
    out_specs = [row(n) for n, _ in outs]
    out_shape = [jax.ShapeDtypeStruct((t, n), dt) for n, dt in outs]
    if vt_block is not None:
        tiles_per_seq = rows_per_group // tile
        per_block = vt_block // tile
        vrows = V_DIM + ONES_ROWS
        out_specs.append(pl.BlockSpec(
            (N_KV_HEADS, None, vrows, tile),
            lambda i: (i // tiles_per_seq, (i % tiles_per_seq) // per_block, 0, (i % tiles_per_seq) % per_block)))
        out_shape.append(jax.ShapeDtypeStruct(
            (t // rows_per_group * N_KV_HEADS, rows_per_group // vt_block, vrows, vt_block), BF16))
    return pl.pallas_call(
        functools.partial(_in_proj_body, q_scale),
        grid=(t // tile,),
        in_specs=[row(D_MODEL),
                  _mod_spec(mods, tile, rows_per_group, 1),
                  _mod_spec(mods, tile, rows_per_group, 0),
                  _resident((1, D_MODEL)),
                  pl.BlockSpec((None, D_MODEL, D_IN), lambda i: (layer, 0, 0)),
                  _resident((1, Q_COLS)),
                  _resident((1, K_COLS)),
                  _resident((MXU_DIM, MXU_DIM))],
        out_specs=out_specs,
        out_shape=out_shape,
        compiler_params=_cparams(("arbitrary",)),
        name="in_proj",
    )(x, mods, mods, norm_g, w_in_bf, qg_row, kg_row, chunk_p)


def _bucket_of_distance(n):
    n = jnp.maximum(n, 0)
    nf = jnp.maximum(n, 1).astype(F32)
    large = MAX_EXACT + (jnp.log(nf / MAX_EXACT) / math.log(MAX_DISTANCE / MAX_EXACT)
                         * (NUM_BUCKETS - MAX_EXACT)).astype(jnp.int32)
    return jnp.where(n < MAX_EXACT, n, jnp.minimum(large, NUM_BUCKETS - 1))


def _bias_of_distance(rel_bias, n):
    b = rel_bias.astype(F32)[_bucket_of_distance(n)]
    b = jnp.where((n >= 0)[..., None], b, MASK_VALUE)
    return jnp.moveaxis(b, -1, 0)


def _bias_table(rel_bias, blk, offset):
    j = jnp.arange(2 * blk, dtype=jnp.int32)
    diff = jnp.where(j < blk, j, j - 2 * blk)
    return _bias_of_distance(rel_bias, diff + offset)[:, None, :]


_S_LAM, _S_OUT = 0, 1
_S_FAR = 2
_S_BMAX = _S_FAR + N_HEADS
_S_BRANGE = _S_BMAX + N_HEADS
BOUND_MARGIN = 1.01
SAFE_EXPONENT_SPAN = 100.0


def _attn_prompt_body(scal_ref, q_ref, k_ref, vt_ref, tdiag_ref, tprev_ref, sg_ref,
                      o_ref, qpad_sc, kmax_sc, bq_sc, m_sc, acc_sc, bdiag_ref, bprev_ref):
    h = pl.program_id(1)
    qi = pl.program_id(2)
    blk = q_ref.shape[0]
    ncombo = 2 * GQA_GROUP
    lam = scal_ref[_S_LAM]
    out_scale = scal_ref[_S_OUT]

    @pl.when(qi == 0)
    def _():
        kk = k_ref[...].astype(F32)
        r = lax.broadcasted_iota(jnp.int32, (LANES, LANES), 0) // HEAD_DIM
        c = lax.broadcasted_iota(jnp.int32, (LANES, LANES), 1) // HEAD_DIM
        same = jnp.where(r == c, 1.0, 0.0).astype(BF16)
        ksq = _dot((kk * kk).astype(BF16), same)
        kmax_sc[...] = jnp.sqrt(jnp.max(ksq, axis=0, keepdims=True))
        for table_ref, tile_ref in ((tdiag_ref, bdiag_ref), (tprev_ref, bprev_ref)):
            for g in range(GQA_GROUP):
                rows = jnp.broadcast_to(table_ref[g], (blk, 2 * blk))
                tile_ref[g] = pltpu.roll(rows, 0, 1, stride=1, stride_axis=0)[:, :blk]

    lane = lax.broadcasted_iota(jnp.int32, (blk, LANES), 1)
    ones = jnp.ones((SUBLANES, LANES), BF16)
    span = jnp.zeros((1, 1), F32)
    for g in range(GQA_GROUP):
        q2 = q_ref[:, g * LANES:(g + 1) * LANES]
        for c in range(2):
            idx = 2 * g + c
            qp = jnp.where((lane < HEAD_DIM) == (c == 0), q2, jnp.zeros_like(q2))
            qpad_sc[idx] = qp
            qf = qp.astype(F32)
            qsq = lax.dot_general(ones, (qf * qf).astype(BF16), _NT, preferred_element_type=F32)[0:1]
            kmax = jnp.max(kmax_sc[:, c * HEAD_DIM:(c + 1) * HEAD_DIM], axis=1, keepdims=True)
            bqk = BOUND_MARGIN * jnp.sqrt(qsq) * kmax
            bq_sc[idx] = bqk + scal_ref[_S_BMAX + h * GQA_GROUP + g]
            span = jnp.maximum(span, 2.0 * jnp.max(bqk, axis=1, keepdims=True)
                               + scal_ref[_S_BRANGE + h * GQA_GROUP + g])
    acc_sc[...] = jnp.zeros(acc_sc.shape, F32)
    far_bias = lambda g: scal_ref[_S_FAR + h * GQA_GROUP + g]
    n_far = jnp.maximum(qi - 1, 0)

    def k_block(kj):
        return k_ref[pl.ds(pl.multiple_of(kj * blk, blk), blk), :]

    def fast_chain(steps):
        work = [(s, idx) for s in range(len(steps)) for idx in range(ncombo)]
        k_blks = [k_block(kj) for kj, _, _ in steps]
        vts = [vt_ref[kj] for kj, _, _ in steps]
        qk = lambda w: lax.dot_general(k_blks[w[0]], qpad_sc[w[1]], _NT, preferred_element_type=F32)
        st_next = qk(work[0])
        for n, (s, idx) in enumerate(work):
            st = st_next
            if n + 1 < len(work):
                st_next = qk(work[n + 1])
            _, shift_of_group, tile_of_group = steps[s]
            g = idx // 2
            if tile_of_group is not None:
                st = st + tile_of_group(g)
            bound = bq_sc[idx] if shift_of_group is None else bq_sc[idx] - shift_of_group(g)
            acc_sc[idx] += _dot(vts[s], jnp.exp2(st - bound).astype(BF16))

    def exact_step(kj, shift_of_group, tile_of_group):
        k_blk = k_block(kj)
        vt = vt_ref[kj]
        for idx in range(ncombo):
            g = idx // 2
            st = lax.dot_general(k_blk, qpad_sc[idx], _NT, preferred_element_type=F32)
            if tile_of_group is not None:
                st = st + tile_of_group(g)
            shift = 0.0 if shift_of_group is None else shift_of_group(g)
            m_old = m_sc[idx]
            m_new = jnp.maximum(m_old, jnp.max(st, axis=0, keepdims=True) + shift)
            p = jnp.exp2(st - (m_new - shift)).astype(BF16)
            acc_sc[idx] = jnp.exp2(m_old - m_new) * acc_sc[idx] + _dot(vt, p)
            m_sc[idx] = m_new

    prev_tile = lambda g: bprev_ref[g]
    diag_tile = lambda g: bdiag_ref[g]

    def run_fast():
        def far_pair(j, carry):
            fast_chain([(2 * j, far_bias, None), (2 * j + 1, far_bias, None)])
            return carry
        lax.fori_loop(0, n_far // 2, far_pair, 0)

        @pl.when(n_far % 2 == 1)
        def _():
            fast_chain([(n_far - 1, far_bias, None)])

        @pl.when(qi >= 1)
        def _():
            fast_chain([(qi - 1, None, prev_tile), (qi, None, diag_tile)])

        @pl.when(qi == 0)
        def _():
            fast_chain([(qi, None, diag_tile)])

    def run_exact():
        m_sc[...] = jnp.full(m_sc.shape, -jnp.inf, F32)

        def far_step(kj, carry):
            exact_step(kj, far_bias, None)
            return carry
        lax.fori_loop(0, n_far, far_step, 0)

        @pl.when(qi >= 1)
        def _():
            exact_step(qi - 1, None, prev_tile)
        exact_step(qi, None, diag_tile)

    lax.cond(span[0, 0] <= SAFE_EXPONENT_SPAN, run_fast, run_exact)

    for g in range(GQA_GROUP):
        a0 = acc_sc[2 * g]
        a1 = acc_sc[2 * g + 1]
        o = a0[:V_DIM] / a0[V_DIM:V_DIM + 1] - lam * (a1[:V_DIM] / a1[V_DIM:V_DIM + 1])
        o = o * lax.rsqrt(jnp.mean(o * o, axis=0, keepdims=True) + NORM_EPS)
        o = o * sg_ref[...] * out_scale
        o_ref[:, g * V_DIM:(g + 1) * V_DIM] = o.T.astype(BF16)


def _attn_prompt(scal, q_bf, k_bf, vt_ext, bias_diag, bias_prev, sg_col, batch, seq):
    blk = ATTN_BLOCK
    nblk = seq // blk
    vrows = V_DIM + ONES_ROWS
    ncombo = 2 * GQA_GROUP
    grid_spec = pltpu.PrefetchScalarGridSpec(
        num_scalar_prefetch=1,
        grid=(batch, N_KV_HEADS, nblk),
        in_specs=[pl.BlockSpec((blk, GQA_GROUP * 2 * HEAD_DIM), lambda b, h, i, s: (b * nblk + i, h)),
                  pl.BlockSpec((seq, 2 * HEAD_DIM), lambda b, h, i, s: (b, h)),
                  pl.BlockSpec((None, nblk, vrows, blk), lambda b, h, i, s: (b * N_KV_HEADS + h, 0, 0, 0)),
                  pl.BlockSpec((GQA_GROUP, 1, 2 * blk), lambda b, h, i, s: (h, 0, 0)),
                  pl.BlockSpec((GQA_GROUP, 1, 2 * blk), lambda b, h, i, s: (h, 0, 0)),
                  pl.BlockSpec((V_DIM, 1), lambda b, h, i, s: (0, 0))],
        out_specs=pl.BlockSpec((blk, GQA_GROUP * V_DIM), lambda b, h, i, s: (b * nblk + i, h)),
        scratch_shapes=[pltpu.VMEM((ncombo, blk, LANES), BF16),
                        pltpu.VMEM((1, LANES), F32),
                        pltpu.VMEM((ncombo, 1, blk), F32),
                        pltpu.VMEM((ncombo, 1, blk), F32),
                        pltpu.VMEM((ncombo, vrows, blk), F32),
                        pltpu.VMEM((GQA_GROUP, blk, blk), F32),
                        pltpu.VMEM((GQA_GROUP, blk, blk), F32)],
    )
    return pl.pallas_call(
        _attn_prompt_body,
        grid_spec=grid_spec,
        out_shape=jax.ShapeDtypeStruct((batch * seq, N_HEADS * V_DIM), BF16),
        compiler_params=_cparams(("arbitrary", "arbitrary", "arbitrary")),
        name="attn_prompt",
    )(scal, q_bf, k_bf, vt_ext, bias_diag, bias_prev, sg_col)


def _attn_sample_body(n_pages, pt_ref, layer_ref, scal_ref, q_ref, knew_ref, vnew_ref, bias_ref, sg_ref, *refs):
    k_pages = refs[:n_pages]
    v_pages = refs[n_pages:2 * n_pages]
    o_ref, kb_sc, vb_sc = refs[2 * n_pages:]
    page_rows = k_pages[0].shape[2]
    past = n_pages * page_rows
    n_new = knew_ref.shape[1]
    lam = scal_ref[0]
    out_scale = scal_ref[1]
    half = q_ref.shape[1] // 2

    @pl.when(pl.program_id(0) == 0)
    def _():
        tail = kb_sc.shape[0] - past - n_new
        kb_sc[past + n_new:, :] = jnp.zeros((tail, LANES), BF16)
        vb_sc[past + n_new:, :] = jnp.zeros((tail, LANES), BF16)

    for j in range(n_pages):
        kb_sc[j * page_rows:(j + 1) * page_rows, :] = k_pages[j][0, 0].astype(BF16)
        vb_sc[j * page_rows:(j + 1) * page_rows, :] = v_pages[j][0, 0].astype(BF16)
    kb_sc[past:past + n_new, :] = knew_ref[0]
    vb_sc[past:past + n_new, :] = vnew_ref[0]
    s = lax.dot_general(q_ref[0], kb_sc[...], _NT, preferred_element_type=F32) + bias_ref[...]
    p = jnp.exp(s - jnp.max(s, axis=-1, keepdims=True))
    p = p / jnp.sum(p, axis=-1, keepdims=True)
    a = p[:half] - lam * p[half:]
    o = _dot(a.astype(BF16), vb_sc[...])
    o = o * lax.rsqrt(jnp.mean(o * o, axis=-1, keepdims=True) + NORM_EPS)
    o_ref[0] = o * sg_ref[...] * out_scale


def _attn_sample(page_table_flat, layer_idx, scal, q_rows, knew, vnew, bias_s, sg_row, cache_k, cache_v, n_pages):
    nb, rows, _ = q_rows.shape
    n_new = knew.shape[1]
    page_rows = cache_k.shape[2]
    keys = bias_s.shape[1]

    def page_spec(j):
        return pl.BlockSpec((1, 1, page_rows, LANES), lambda b, pt, l: (l[0], pt[b * n_pages + j], 0, 0))

    grid_spec = pltpu.PrefetchScalarGridSpec(
        num_scalar_prefetch=2,
        grid=(nb,),
        in_specs=[pl.BlockSpec(memory_space=pltpu.SMEM),
                  pl.BlockSpec((1, rows, LANES), lambda b, pt, l: (b, 0, 0)),
                  pl.BlockSpec((1, n_new, LANES), lambda b, pt, l: (b, 0, 0)),
                  pl.BlockSpec((1, n_new, LANES), lambda b, pt, l: (b, 0, 0)),
                  pl.BlockSpec((rows, keys), lambda b, pt, l: (0, 0)),
                  pl.BlockSpec((1, V_DIM), lambda b, pt, l: (0, 0))]
                 + [page_spec(j) for j in range(n_pages)] * 2,
        out_specs=pl.BlockSpec((1, rows // 2, V_DIM), lambda b, pt, l: (b, 0, 0)),
        scratch_shapes=[pltpu.VMEM((keys, LANES), BF16),
                        pltpu.VMEM((keys, LANES), BF16)],
    )
    return pl.pallas_call(
        functools.partial(_attn_sample_body, n_pages),
        grid_spec=grid_spec,
        out_shape=jax.ShapeDtypeStruct((nb, rows // 2, V_DIM), F32),
        compiler_params=_cparams(("arbitrary",)),
        name="attn_sample",
    )(page_table_flat, layer_idx, scal, q_rows, knew, vnew, bias_s, sg_row,
      *([cache_k] * n_pages), *([cache_v] * n_pages))


def _lru_coeffs(xc, wa_ref, wi_ref, ba_ref, bi_ref, sp_ref):
    xb = xc.astype(BF16)
    ra, ri = [], []
    for j in range(xc.shape[1] // MXU_DIM):
        cols = slice(j * MXU_DIM, (j + 1) * MXU_DIM)
        ra.append(_dot(xb[:, cols], wa_ref[j]))
        ri.append(_dot(xb[:, cols], wi_ref[j]))
    r = jax.nn.sigmoid(jnp.concatenate(ra, axis=1) + ba_ref[...])
    i = jax.nn.sigmoid(jnp.concatenate(ri, axis=1) + bi_ref[...])
    log_a = -LRU_C * r * sp_ref[...]
    a = jnp.exp(log_a)
    mult = jnp.sqrt(-jnp.tanh(log_a) * (a * a + 1.0))
    return a, mult, i


def _rglru_prompt_body(xr_ref, yr_ref, cw_ref, cb_ref, wa_ref, wi_ref, ba_ref, bi_ref, sp_ref,
                       rnn_ref, hl_ref, xe_sc, a_sc, b_sc, h_sc):
    i = pl.program_id(1)
    tt = xr_ref.shape[0]

    @pl.when(i == 0)
    def _():
        xe_sc[0:SUBLANES, :] = jnp.zeros((SUBLANES, D_MODEL), F32)
        h_sc[...] = jnp.zeros(h_sc.shape, F32)

    xe_sc[SUBLANES:SUBLANES + tt, :] = xr_ref[...]
    xc = cb_ref[...]
    for j in range(CONV_W):
        off = SUBLANES - (CONV_W - 1) + j
        xc = xc + xe_sc[off:off + tt, :] * cw_ref[j:j + 1, :]
    a, mult, gate_i = _lru_coeffs(xc, wa_ref, wi_ref, ba_ref, bi_ref, sp_ref)
    row = lax.broadcasted_iota(jnp.int32, (tt, 1), 0)
    mult = jnp.where((row == 0) & (i == 0), 1.0, mult)
    a_sc[...] = a
    b_sc[...] = mult * gate_i * xc

    def step(t, h):
        h = a_sc[pl.ds(t, 1), :] * h + b_sc[pl.ds(t, 1), :]
        b_sc[pl.ds(t, 1), :] = h
        return h

    h = lax.fori_loop(0, tt, step, h_sc[...], unroll=8)
    h_sc[...] = h
    hl_ref[...] = h
    rnn_ref[...] = (b_sc[...] * jax.nn.gelu(yr_ref[...])).astype(BF16)
    xe_sc[0:SUBLANES, :] = xe_sc[tt:tt + SUBLANES, :]


def _rglru_prompt(xr, yr, cw, cb, wa_bd, wi_bd, ba, bi, sp, batch, seq):
    tt = min(TIME_TILE, seq)
    nt = seq // tt
    row = pl.BlockSpec((tt, D_MODEL), lambda b, i: (b * nt + i, 0))
    nbd = D_MODEL // MXU_DIM
    return pl.pallas_call(
        _rglru_prompt_body,
        grid=(batch, nt),
        in_specs=[row, row,
                  pl.BlockSpec((CONV_W, D_MODEL), lambda b, i: (0, 0)),
                  pl.BlockSpec((1, D_MODEL), lambda b, i: (0, 0)),
                  pl.BlockSpec((nbd, MXU_DIM, MXU_DIM), lambda b, i: (0, 0, 0)),
                  pl.BlockSpec((nbd, MXU_DIM, MXU_DIM), lambda b, i: (0, 0, 0)),
                  pl.BlockSpec((1, D_MODEL), lambda b, i: (0, 0)),
                  pl.BlockSpec((1, D_MODEL), lambda b, i: (0, 0)),
                  pl.BlockSpec((1, D_MODEL), lambda b, i: (0, 0))],
        out_specs=[row, pl.BlockSpec((None, 1, D_MODEL), lambda b, i: (b, 0, 0))],
        out_shape=[jax.ShapeDtypeStruct((batch * seq, D_MODEL), BF16),
                   jax.ShapeDtypeStruct((batch, 1, D_MODEL), F32)],
        scratch_shapes=[pltpu.VMEM((tt + SUBLANES, D_MODEL), F32),
                        pltpu.VMEM((tt, D_MODEL), F32),
                        pltpu.VMEM((tt, D_MODEL), F32),
                        pltpu.VMEM((1, D_MODEL), F32)],
        compiler_params=_cparams(("arbitrary", "arbitrary")),
        name="rglru_prompt",
    )(xr, yr, cw, cb, wa_bd, wi_bd, ba, bi, sp)


def _rglru_sample_body(n_steps, first_is_start, xr_ref, yr_ref, buf_ref, h0_ref, cw_ref, cb_ref, wa_ref, wi_ref, ba_ref, bi_ref,
                       sp_ref, rnn_ref, hl_ref):
    nb = h0_ref.shape[0]
    taps = [buf_ref[j] for j in range(CONV_W - 1)] + [xr_ref[t * nb:(t + 1) * nb, :] for t in range(n_steps)]
    h = h0_ref[...]
    for t in range(n_steps):
        xc = cb_ref[...]
        for j in range(CONV_W):
            xc = xc + taps[t + j] * cw_ref[j:j + 1, :]
        a, mult, gate_i = _lru_coeffs(xc, wa_ref, wi_ref, ba_ref, bi_ref, sp_ref)
        if first_is_start and t == 0:
            mult = jnp.ones_like(mult)
        h = a * h + mult * gate_i * xc
        rnn_ref[t * nb:(t + 1) * nb, :] = (h * jax.nn.gelu(yr_ref[t * nb:(t + 1) * nb, :])).astype(BF16)
    hl_ref[...] = h


def _rglru_sample(xr, yr, buf, h0, cw, cb, wa_bd, wi_bd, ba, bi, sp, n_steps, first_is_start):
    t = xr.shape[0]
    return pl.pallas_call(
        functools.partial(_rglru_sample_body, n_steps, first_is_start),
        out_shape=[jax.ShapeDtypeStruct((t, D_MODEL), BF16),
                   jax.ShapeDtypeStruct(h0.shape, F32)],
        compiler_params=pltpu.CompilerParams(vmem_limit_bytes=VMEM_LIMIT_BYTES),
        name="rglru_sample",
    )(xr, yr, buf, h0, cw, cb, wa_bd, wi_bd, ba, bi, sp)


def _route(logits):
    lane = lax.broadcasted_iota(jnp.int32, logits.shape, 1).astype(F32)
    none = float(ROUTER_LANES)
    is_group = lane < N_GROUPS
    lg = jnp.where(is_group, logits, -jnp.inf)
    eg = jnp.exp(lg - jnp.max(lg, axis=-1, keepdims=True))
    gp = eg / jnp.sum(eg, axis=-1, keepdims=True)
    gw = jnp.max(gp, axis=-1, keepdims=True)
    gi = jnp.min(jnp.where(is_group & (gp == gw), lane, none), axis=-1, keepdims=True)
    first = N_GROUPS + gi * EXPERTS_PER_GROUP
    in_group = (lane >= first) & (lane < first + EXPERTS_PER_GROUP)
    le = jnp.where(in_group, logits, -jnp.inf)
    ee = jnp.exp(le - jnp.max(le, axis=-1, keepdims=True))
    pe = ee / jnp.sum(ee, axis=-1, keepdims=True)
    p1 = jnp.max(pe, axis=-1, keepdims=True)
    i1 = jnp.min(jnp.where(in_group & (pe == p1), lane, none), axis=-1, keepdims=True)
    rest = in_group & (lane != i1)
    pr = jnp.where(rest, pe, -1.0)
    p2 = jnp.max(pr, axis=-1, keepdims=True)
    i2 = jnp.min(jnp.where(rest & (pr == p2), lane, none), axis=-1, keepdims=True)
    den = p1 + p2
    w1 = gw * (p1 / den)
    w2 = gw * (p2 / den)
    rec = jnp.where(lane == 0.0, i1 - N_GROUPS, 0.0)
    rec = jnp.where(lane == 1.0, i2 - N_GROUPS, rec)
    rec = jnp.where(lane == 2.0, w1, rec)
    rec = jnp.where(lane == 3.0, w2, rec)
    return rec, jnp.where(lane == i1, 1.0, 0.0), jnp.where(lane == i2, 1.0, 0.0)


def _merge_body(x_ref, attn_ref, rnn_ref, ga_ref, gr_ref, g1_ref, sc2_ref, sh2_ref, n2g_ref,
                wpa_ref, wpr_ref, wo_ref, wrh_ref, wrl_ref, br_ref, ltri_ref, cnt_in_ref,
                x1_ref, u2_ref, rec_ref, cnt_ref, cnt_sc):
    pa = _dot(attn_ref[...], wpa_ref[...])
    pr = _dot(rnn_ref[...], wpr_ref[...])
    merged = jax.nn.sigmoid(ga_ref[...]) * pa + jax.nn.sigmoid(gr_ref[...]) * pr
    x1 = x_ref[...] + g1_ref[...] * _dot(merged.astype(BF16), wo_ref[...])
    x1_ref[...] = x1
    u2 = _rms_rows(x1, n2g_ref[...]) * (1.0 + sc2_ref[...]) + sh2_ref[...]
    u2_ref[...] = u2
    u_hi = u2.astype(BF16)
    u_lo = (u2 - u_hi.astype(F32)).astype(BF16)
    logits = (_dot(u_hi, wrh_ref[...]) + _dot(u_lo, wrh_ref[...]) + _dot(u_hi, wrl_ref[...])) + br_ref[...]
    rec, hot1, hot2 = _route(logits)

    @pl.when(pl.program_id(0) == 0)
    def _():
        cnt_sc[...] = cnt_in_ref[...]
    cnt = cnt_sc[...]
    tot1 = jnp.sum(hot1, axis=0, keepdims=True)
    tot2 = jnp.sum(hot2, axis=0, keepdims=True)
    before1 = _dot(ltri_ref[...], hot1.astype(BF16)) + cnt
    before2 = _dot(ltri_ref[...], hot2.astype(BF16)) + (cnt + tot1)
    rank1 = jnp.sum(hot1 * before1, axis=-1, keepdims=True)
    rank2 = jnp.sum(hot2 * before2, axis=-1, keepdims=True)
    cnt_sc[...] = cnt + tot1 + tot2
    cnt_ref[...] = cnt_sc[...]
    lane = lax.broadcasted_iota(jnp.int32, rec.shape, 1)
    rec = jnp.where(lane == 4, rank1, rec)
    rec_ref[...] = jnp.where(lane == 5, rank2, rec)


def _merge(x, attn, rnn, ga, gr, mods, rows_per_group, cnt_in, n2g, wpa, wpr, wo, wr_hi, wr_lo, br):
    t = x.shape[0]
    tile = min(TOKEN_TILE, t)
    row = lambda n: pl.BlockSpec((tile, n), lambda i: (i, 0))
    sq = _resident((D_MODEL, D_MODEL))
    r = jnp.arange(tile)
    ltri = (r[None, :] < r[:, None]).astype(BF16)
    return pl.pallas_call(
        _merge_body,
        grid=(t // tile,),
        in_specs=[row(D_MODEL), row(D_MODEL), row(D_MODEL), row(D_MODEL), row(D_MODEL),
                  _mod_spec(mods, tile, rows_per_group, 2),
                  _mod_spec(mods, tile, rows_per_group, 4),
                  _mod_spec(mods, tile, rows_per_group, 3),
                  _resident((1, D_MODEL)), sq, sq, sq,
                  _resident((D_MODEL, ROUTER_LANES)), _resident((D_MODEL, ROUTER_LANES)),
                  _resident((1, ROUTER_LANES)), _resident((tile, tile)), _resident((1, ROUTER_LANES))],
        out_specs=[row(D_MODEL), row(D_MODEL), row(ROUTER_LANES), _resident((1, ROUTER_LANES))],
        out_shape=[jax.ShapeDtypeStruct((t, D_MODEL), F32),
                   jax.ShapeDtypeStruct((t, D_MODEL), F32),
                   jax.ShapeDtypeStruct((t, ROUTER_LANES), F32),
                   jax.ShapeDtypeStruct((1, ROUTER_LANES), F32)],
        scratch_shapes=[pltpu.VMEM((1, ROUTER_LANES), F32)],
        compiler_params=_cparams(("arbitrary",)),
        name="merge",
    )(x, attn, rnn, ga, gr, mods, mods, mods, n2g, wpa, wpr, wo, wr_hi, wr_lo, br, ltri, cnt_in)


ROW_UNROLL = 8


def _for_rows(n_rows, fn):
    def body(j, carry):
        for s in range(ROW_UNROLL):
            fn(j * ROW_UNROLL + s)
        return carry
    lax.fori_loop(0, n_rows // ROW_UNROLL, body, 0)


def _dispatch_body(t_all, n_first, pos_ref, ua_ref, ub_ref, init_hbm, us_hbm, sem):
    del init_hbm
    i = pl.program_id(0)

    def scatter_tile(u_ref, base):
        tile = u_ref.shape[0]

        def issue(r):
            for slot in range(2):
                pltpu.make_async_copy(u_ref.at[pl.ds(r, 1)],
                                      us_hbm.at[pl.ds(pos_ref[slot * t_all + base + r], 1)], sem.at[0]).start()
        _for_rows(tile, issue)
        for _ in range(2):
            pltpu.make_async_copy(u_ref, us_hbm.at[pl.ds(0, tile)], sem.at[0]).wait()

    @pl.when(i < n_first)
    def _():
        scatter_tile(ua_ref, i * ua_ref.shape[0])

    @pl.when(i >= n_first)
    def _():
        scatter_tile(ub_ref, n_first * ua_ref.shape[0] + (i - n_first) * ub_ref.shape[0])


def _dispatch(pos_flat, u_first, u_second, init):
    n_sorted_rows = init.shape[0]
    t_all = u_first.shape[0] + u_second.shape[0]
    tile_a = min(TOKEN_TILE, u_first.shape[0])
    tile_b = min(TOKEN_TILE, u_second.shape[0])
    n_a = u_first.shape[0] // tile_a
    n_b = u_second.shape[0] // tile_b
    assert n_a * tile_a == u_first.shape[0] and n_b * tile_b == u_second.shape[0]
    assert tile_a % ROW_UNROLL == 0 and tile_b % ROW_UNROLL == 0
    grid_spec = pltpu.PrefetchScalarGridSpec(
        num_scalar_prefetch=1,
        grid=(n_a + n_b,),
        in_specs=[pl.BlockSpec((tile_a, D_MODEL), lambda i, pos: (jnp.minimum(i, n_a - 1), 0)),
                  pl.BlockSpec((tile_b, D_MODEL), lambda i, pos: (jnp.maximum(i - n_a, 0), 0)),
                  pl.BlockSpec(memory_space=pl.ANY)],
        out_specs=pl.BlockSpec(memory_space=pl.ANY),
        scratch_shapes=[pltpu.SemaphoreType.DMA((1,))],
    )
    return pl.pallas_call(
        functools.partial(_dispatch_body, t_all, n_a),
        grid_spec=grid_spec,
        out_shape=jax.ShapeDtypeStruct((n_sorted_rows, D_MODEL), F32),
        input_output_aliases={3: 0},
        compiler_params=_cparams(("arbitrary",)),
        name="dispatch",
    )(pos_flat, u_first, u_second, init)


def _experts_body(te_ref, nv_ref, last_ref, layer_ref, u_ref, wg_ref, wu_ref, wd_ref, y_ref, wg_sc, wu_sc, wd_sc):
    del last_ref, layer_ref
    i = pl.program_id(0)

    @pl.when((i == 0) | (te_ref[i] != te_ref[jnp.maximum(i - 1, 0)]))
    def _():
        wg_sc[...] = wg_ref[...].astype(BF16)
        wu_sc[...] = wu_ref[...].astype(BF16)
        wd_sc[...] = wd_ref[...].astype(BF16)

    @pl.when(nv_ref[i] > 0)
    def _():
        u = u_ref[...].astype(BF16)
        hg = _dot(u, wg_sc[...])
        hid = (hg * jax.nn.sigmoid(hg)) * _dot(u, wu_sc[...])
        y_ref[...] = _dot(hid.astype(BF16), wd_sc[...])

    @pl.when(nv_ref[i] == 0)
    def _():
        y_ref[...] = jnp.zeros(y_ref.shape, F32)


def _experts(tile_expert, tile_valid, last_tile, layer_idx, u_sorted, w_gate, w_up, w_down):
    tile = EXPERT_TILE
    n_tiles = tile_expert.shape[0]
    wspec = lambda a, b: pl.BlockSpec((None, None, a, b), lambda i, te, nv, last, lyr: (lyr[0], te[i], 0, 0))
    grid_spec = pltpu.PrefetchScalarGridSpec(
        num_scalar_prefetch=4,
        grid=(n_tiles,),
        in_specs=[pl.BlockSpec((tile, D_MODEL), lambda i, te, nv, last, lyr: (jnp.minimum(i, last[0]), 0)),
                  wspec(D_MODEL, D_EXPERT), wspec(D_MODEL, D_EXPERT), wspec(D_EXPERT, D_MODEL)],
        out_specs=pl.BlockSpec((tile, D_MODEL), lambda i, te, nv, last, lyr: (i, 0)),
        scratch_shapes=[pltpu.VMEM((D_MODEL, D_EXPERT), BF16),
                        pltpu.VMEM((D_MODEL, D_EXPERT), BF16),
                        pltpu.VMEM((D_EXPERT, D_MODEL), BF16)],
    )
    return pl.pallas_call(
        _experts_body,
        grid_spec=grid_spec,
        out_shape=jax.ShapeDtypeStruct(u_sorted.shape, F32),
        compiler_params=_cparams(("arbitrary",)),
        name="experts",
    )(tile_expert, tile_valid, last_tile, layer_idx, u_sorted, w_gate, w_up, w_down)


def _dispatch_plan(rec_all, counts_row):
    t_all = rec_all.shape[0]
    tile = EXPERT_TILE
    n_tiles = -(-2 * t_all // tile) + N_EXPERTS
    counts = counts_row[0, N_GROUPS:N_GROUPS + N_EXPERTS].astype(jnp.int32)
    tiles_per_e = (counts + tile - 1) // tile
    tile_end = jnp.cumsum(tiles_per_e)
    tile_start = tile_end - tiles_per_e
    experts = jnp.arange(N_EXPERTS, dtype=jnp.int32)
    e = rec_all[:, 0:2].astype(jnp.int32)
    rank = rec_all[:, 4:6].astype(jnp.int32)
    row_start = jnp.sum(jnp.where(e[:, :, None] == experts, tile_start * tile, 0), axis=-1)
    pos_flat = (row_start + rank).T.reshape(-1)
    n_used = tile_end[-1]
    tile_ids = jnp.arange(n_tiles, dtype=jnp.int32)
    clamped = jnp.minimum(tile_ids, n_used - 1)
    te = jnp.sum((clamped[:, None] >= tile_end[None, :]).astype(jnp.int32), axis=-1)
    te = jnp.minimum(te, N_EXPERTS - 1)
    first = jnp.sum(jnp.where(te[:, None] == experts, tile_start, 0), axis=-1)
    cnt_t = jnp.sum(jnp.where(te[:, None] == experts, counts, 0), axis=-1)
    valid = jnp.clip(cnt_t - (tile_ids - first) * tile, 0, tile)
    valid = jnp.where(tile_ids < n_used, valid, 0).astype(jnp.int32)
    return pos_flat, te, valid, (n_used - 1).reshape(1).astype(jnp.int32), n_tiles * tile


def _combine_body(t_all, row_offset, pos_ref, x_ref, rec_ref, g2_ref, ys_hbm, o_ref, ybuf, sem):
    i = pl.program_id(0)
    n = pl.num_programs(0)
    tile = x_ref.shape[0]
    slot = i % 2

    def start_gather(tile_idx, buf):
        base = row_offset + tile_idx * tile

        def issue(r):
            for k in range(2):
                pltpu.make_async_copy(ys_hbm.at[pl.ds(pos_ref[k * t_all + base + r], 1)],
                                      ybuf.at[buf, k, pl.ds(r, 1)], sem.at[buf]).start()
        _for_rows(tile, issue)

    @pl.when(i == 0)
    def _():
        start_gather(0, 0)

    @pl.when(i + 1 < n)
    def _():
        start_gather(i + 1, 1 - slot)

    for k in range(2):
        pltpu.make_async_copy(ys_hbm.at[pl.ds(0, tile)], ybuf.at[slot, k], sem.at[slot]).wait()
    w1 = rec_ref[:, 2:3]
    w2 = rec_ref[:, 3:4]
    o_ref[...] = x_ref[...] + g2_ref[...] * (w1 * ybuf[slot, 0] + w2 * ybuf[slot, 1])


def _combine(pos_flat, x1, rec, y_sorted, mods, rows_per_group, row_offset, t_all):
    t = x1.shape[0]
    tile = min(TOKEN_TILE, t)
    assert tile % ROW_UNROLL == 0
    if mods.ndim == 3:
        tiles_per_group = rows_per_group // tile
        g2_spec = pl.BlockSpec((None, 1, D_MODEL), lambda i, pos: (i // tiles_per_group, 0, 5))
    else:
        g2_spec = pl.BlockSpec((tile, D_MODEL), lambda i, pos: (i, 5))
    grid_spec = pltpu.PrefetchScalarGridSpec(
        num_scalar_prefetch=1,
        grid=(t // tile,),
        in_specs=[pl.BlockSpec((tile, D_MODEL), lambda i, pos: (i, 0)),
                  pl.BlockSpec((tile, ROUTER_LANES), lambda i, pos: (i, 0)),
                  g2_spec,
                  pl.BlockSpec(memory_space=pl.ANY)],
        out_specs=pl.BlockSpec((tile, D_MODEL), lambda i, pos: (i, 0)),
        scratch_shapes=[pltpu.VMEM((2, 2, tile, D_MODEL), F32),
                        pltpu.SemaphoreType.DMA((2,))],
    )
    return pl.pallas_call(
        functools.partial(_combine_body, t_all, row_offset),
        grid_spec=grid_spec,
        out_shape=jax.ShapeDtypeStruct((t, D_MODEL), F32),
        compiler_params=_cparams(("arbitrary",)),
        name="combine",
    )(pos_flat, x1, rec, mods, y_sorted)


def _block_diag(w):
    per = MXU_DIM // RNN_BLOCK
    w4 = w.reshape(N_RNN_BLOCKS // per, per, RNN_BLOCK, RNN_BLOCK)
    eye = jnp.eye(per, dtype=w.dtype)
    return jnp.einsum("jmcd,mn->jmcnd", w4, eye).reshape(N_RNN_BLOCKS // per, MXU_DIM, MXU_DIM)


def kernel(x_prompt, x_sample, cache_k, cache_v, state_rnn, state_conv, page_table, c_prompt, c_sample, rel_bias, w_mod, b_mod, norm1_g, norm2_g, w_in, q_norm_g, k_norm_g, lam_q1, lam_k1, lam_q2, lam_k2, subln_g, conv_w, conv_b, lru_wa, lru_ba, lru_wi, lru_bi, lru_lambda, w_pa, w_pr, w_o, w_rg, b_rg, w_re, b_re, w_e_gate, w_e_up, w_e_down):
    bp, tp, _ = x_prompt.shape
    bs, ts, _ = x_sample.shape
    depth = w_in.shape[0]
    n_pool = cache_k.shape[1]
    n_pages = page_table.shape[1]
    past_len = n_pages * PAGE_SIZE
    t_p = bp * tp
    t_s = bs * ts
    t_all = t_p + t_s
    blk = ATTN_BLOCK
    nblk = tp // blk
    assert tp % blk == 0 and tp % TIME_TILE == 0 and t_p % TOKEN_TILE == 0
    assert t_s % SUBLANES == 0 and t_p % min(TOKEN_TILE, t_s) == 0
    log2e = math.log2(math.e)

    rows_c = -(-(bp + bs) // SUBLANES) * SUBLANES
    c_all = jnp.zeros((rows_c, D_MODEL), F32).at[:bp].set(c_prompt).at[bp:bp + bs].set(c_sample)
    mods = _mods(c_all, w_mod, b_mod)
    mods_p = mods[:, :bp].reshape(depth, bp, 1, 6 * D_MODEL)
    mods_s = jnp.tile(mods[:, bp:bp + bs], (1, ts, 1))

    w_in_bf = w_in.astype(BF16)
    w_pa_bf, w_pr_bf, w_o_bf = w_pa.astype(BF16), w_pr.astype(BF16), w_o.astype(BF16)
    qg_row = jnp.tile(q_norm_g, (1, Q_COLS // HEAD_DIM)).reshape(depth, 1, Q_COLS)
    kg_row = jnp.tile(k_norm_g, (1, K_COLS // HEAD_DIM)).reshape(depth, 1, K_COLS)
    seg = jnp.arange(MXU_DIM) // HEAD_DIM
    chunk_p = ((seg[:, None] == seg[None, :]).astype(F32) / HEAD_DIM).astype(BF16)
    wr = jnp.concatenate([w_rg, w_re], axis=-1)
    wr = jnp.pad(wr, ((0, 0), (0, 0), (0, ROUTER_LANES - wr.shape[-1])))
    wr_hi = wr.astype(BF16)
    wr_lo = (wr - wr_hi.astype(F32)).astype(BF16)
    br = jnp.pad(jnp.concatenate([b_rg, b_re], axis=-1), ((0, 0), (0, ROUTER_LANES - N_GROUPS - N_EXPERTS)))
    br = br.reshape(depth, 1, ROUTER_LANES)
    wa_bd = jax.vmap(_block_diag)(lru_wa).astype(BF16)
    wi_bd = jax.vmap(_block_diag)(lru_wi).astype(BF16)
    ba = lru_ba.reshape(depth, 1, D_MODEL)
    bi = lru_bi.reshape(depth, 1, D_MODEL)
    sp = jax.nn.softplus(-lru_lambda.astype(F32)).reshape(depth, 1, D_MODEL)
    cb = conv_b.reshape(depth, 1, D_MODEL)
    lam_f = lambda a, b: jnp.exp(jnp.sum(a.astype(F32) * b.astype(F32), axis=-1))
    lam_base = lam_f(lam_q1, lam_k1) - lam_f(lam_q2, lam_k2)

    bias_diag = _bias_table(rel_bias, blk, 0) * log2e
    bias_prev = _bias_table(rel_bias, blk, blk) * log2e
    rb = rel_bias.astype(F32) * log2e
    bias_scal = jnp.concatenate([rb[NUM_BUCKETS - 1],
                                 jnp.max(rb, axis=0), jnp.max(rb, axis=0) - jnp.min(rb, axis=0)])

    page_rows = PAGE_SIZE * N_KV_HEADS
    n_new = -(-ts * N_KV_HEADS // (2 * SUBLANES)) * (2 * SUBLANES)
    keys_s = n_pages * page_rows + -(-n_new // LANES) * LANES
    n_pos = past_len + ts
    dist = jnp.arange(-(ts - 1), n_pos, dtype=jnp.int32)
    flipped = _bias_of_distance(rel_bias, dist)[:, ::-1]
    by_token = jnp.stack([flipped[:, ts - 1 - t:ts - 1 - t + n_pos] for t in range(ts)], axis=1)
    by_row = jnp.transpose(by_token.reshape(N_KV_HEADS, GQA_GROUP, ts, n_pos), (0, 2, 1, 3))
    own_head = (jnp.arange(N_KV_HEADS)[:, None, None, None, None]
                == jnp.arange(N_KV_HEADS)[None, None, None, None, :])
    bias_rows = jnp.where(own_head, by_row[..., None], MASK_VALUE)
    bias_rows = bias_rows.reshape(N_KV_HEADS * ts * GQA_GROUP, n_pos * N_KV_HEADS)
    bias_rows = jnp.pad(bias_rows, ((0, 0), (0, keys_s - n_pos * N_KV_HEADS)), constant_values=MASK_VALUE)
    bias_s = jnp.concatenate([bias_rows, bias_rows], axis=0)
    pt_flat = page_table.reshape(-1).astype(jnp.int32)
    cache_k_rows = cache_k.reshape(depth, n_pool, page_rows, LANES)
    cache_v_rows = cache_v.reshape(depth, n_pool, page_rows, LANES)

    xp = x_prompt.reshape(t_p, D_MODEL)
    xs = jnp.swapaxes(x_sample, 0, 1).reshape(t_s, D_MODEL)
    outs = [[] for _ in range(8)]
    for l in range(depth):
        lam_init = 0.8 - 0.6 * math.exp(-0.3 * l)
        lam = lam_base[l] + lam_init
        sg_col = subln_g[l].astype(F32).reshape(V_DIM, 1)
        sg_row = subln_g[l].astype(F32).reshape(1, V_DIM)
        scal_s = jnp.stack([lam, jnp.asarray(1.0 - lam_init, F32)]).astype(F32)
        scal_p = jnp.concatenate([scal_s, bias_scal])
        lp = (norm1_g[l].reshape(1, D_MODEL), w_in_bf, qg_row[l], kg_row[l], chunk_p)
        rg = (conv_w[l], cb[l], wa_bd[l], wi_bd[l], ba[l], bi[l], sp[l])
        mg = (norm2_g[l].reshape(1, D_MODEL), w_pa_bf[l], w_pr_bf[l], w_o_bf[l], wr_hi[l], wr_lo[l], br[l])

        q, k, kb, v, xr, yr, ga, gr, vt_ext = _in_proj(xp, mods_p[l], tp, ATTN_SCALE * log2e, blk, l, *lp)
        attn = _attn_prompt(scal_p, q, kb, vt_ext, bias_diag, bias_prev, sg_col, bp, tp)
        rnn, h_last = _rglru_prompt(xr, yr, *rg, bp, tp)
        x1p, u2p, recp, cnt = _merge(xp, attn, rnn, ga, gr, mods_p[l], tp,
                                     jnp.zeros((1, ROUTER_LANES), F32), *mg)
        outs[0].append(k.reshape(bp, tp, N_KV_HEADS, 2 * HEAD_DIM))
        outs[1].append(v.reshape(bp, tp, N_KV_HEADS, V_DIM))
        outs[2].append(h_last.reshape(bp, D_MODEL))
        outs[3].append(xr.reshape(bp, tp, D_MODEL)[:, tp - (CONV_W - 1):])

        q, k, kb, v, xr, yr, ga, gr = _in_proj(xs, mods_s[l], ts, ATTN_SCALE, None, l, *lp)
        q6 = q.reshape(ts, bs, N_KV_HEADS, GQA_GROUP, 2, HEAD_DIM)
        q6 = jnp.transpose(q6, (1, 4, 2, 0, 3, 5))
        zq = jnp.zeros_like(q6[:, 0])
        q_rows = jnp.stack([jnp.concatenate([q6[:, 0], zq], axis=-1),
                            jnp.concatenate([zq, q6[:, 1]], axis=-1)], axis=1)
        q_rows = q_rows.reshape(bs, 2 * N_KV_HEADS * ts * GQA_GROUP, 2 * HEAD_DIM)
        new_rows = lambda z: jnp.pad(jnp.swapaxes(z.reshape(ts, bs, N_KV_HEADS, LANES), 0, 1)
                                     .reshape(bs, ts * N_KV_HEADS, LANES),
                                     ((0, 0), (0, n_new - ts * N_KV_HEADS), (0, 0)))
        o_s = _attn_sample(pt_flat, jnp.full((1,), l, jnp.int32), scal_s, q_rows, new_rows(kb),
                           new_rows(v.astype(BF16)), bias_s, sg_row, cache_k_rows, cache_v_rows, n_pages)
        attn = jnp.transpose(o_s.reshape(bs, N_KV_HEADS, ts, GQA_GROUP, V_DIM), (2, 0, 1, 3, 4))
        attn = attn.reshape(t_s, N_HEADS * V_DIM).astype(BF16)
        buf = jnp.swapaxes(state_conv[l], 0, 1)
        rnn, h_last = _rglru_sample(xr, yr, buf, state_rnn[l], *rg, ts, past_len == 0)
        x1s, u2s, recs, cnt = _merge(xs, attn, rnn, ga, gr, mods_s[l], ts, cnt, *mg)
        to_bt = lambda z, n: jnp.swapaxes(z.reshape(ts, bs, n), 0, 1)
        outs[4].append(to_bt(k, K_COLS).reshape(bs, ts, N_KV_HEADS, 2 * HEAD_DIM))
        outs[5].append(to_bt(v, V_COLS).reshape(bs, ts, N_KV_HEADS, V_DIM))
        outs[6].append(h_last)
        xin = jnp.concatenate([state_conv[l], to_bt(xr, D_MODEL)], axis=1)
        outs[7].append(xin[:, -(CONV_W - 1):])

        pos_flat, te, valid, last_tile, n_sorted = _dispatch_plan(jnp.concatenate([recp, recs], axis=0), cnt)
        u_sorted = _dispatch(pos_flat, u2p, u2s, jnp.zeros((n_sorted, D_MODEL), F32) if l == 0 else u_sorted)
        y_sorted = _experts(te, valid, last_tile, jnp.full((1,), l, jnp.int32), u_sorted,
                            w_e_gate, w_e_up, w_e_down)
        xp = _combine(pos_flat, x1p, recp, y_sorted, mods_p[l], tp, 0, t_all)
        xs = _combine(pos_flat, x1s, recs, y_sorted, mods_s[l], ts, t_p, t_all)

    y_sample = jnp.swapaxes(xs.reshape(ts, bs, D_MODEL), 0, 1)
    stack = lambda i: jnp.stack(outs[i])
    return (xp.reshape(bp, tp, D_MODEL), y_sample, stack(0), stack(1), stack(2), stack(3),
            stack(4), stack(5), stack(6), stack(7))
```

```python
import functools
import math

import jax
import jax.numpy as jnp
from jax import lax
from jax.experimental import pallas as pl
from jax.experimental.pallas import tpu as pltpu

F32 = jnp.float32
BF16 = jnp.bfloat16

D_MODEL = 1024
HEAD_DIM = 64
N_HEADS = 8
N_KV_HEADS = 4
GQA_GROUP = 2
V_DIM = 128
ATTN_SCALE = HEAD_DIM ** -0.5
MASK_VALUE = -1e30
NUM_BUCKETS = 32
MAX_EXACT = 16
MAX_DISTANCE = 128
N_RNN_BLOCKS = 16
RNN_BLOCK = 64
CONV_W = 4
LRU_C = 8.0
N_GROUPS = 4
EXPERTS_PER_GROUP = 8
N_EXPERTS = 32
D_EXPERT = 512
NORM_EPS = 1e-6
PAGE_SIZE = 128
Q_COLS = 1024
K_COLS = 512
V_COLS = 512
D_IN = 6144

LANES = 128
SUBLANES = 8
MXU_DIM = 256
VMEM_LIMIT_BYTES = 56 * 1024 * 1024

TOKEN_TILE = 256
MERGE_TILE = 512
ATTN_BLOCK = 512
TIME_TILE = 256
EXPERT_TILE = 512
ROUTER_LANES = 128
ONES_ROWS = 16

_NT = (((1,), (1,)), ((), ()))


def _cparams(sem):
    return pltpu.CompilerParams(dimension_semantics=sem, vmem_limit_bytes=VMEM_LIMIT_BYTES)


def _dot(a, b):
    return jnp.dot(a, b, preferred_element_type=F32)


def _resident(shape):
    n = len(shape)
    return pl.BlockSpec(shape, lambda *_: (0,) * n)


_MODS_COLS = 1536


def _mods_body(c_ref, w_ref, b_ref, o_ref):
    c = c_ref[...]
    s = c * jax.nn.sigmoid(c)
    o_ref[...] = jnp.dot(s, w_ref[...], precision=lax.Precision.HIGHEST,
                         preferred_element_type=F32) + b_ref[...]


def _mods(c_all, w_mod, b_mod):
    rows = c_all.shape[0]
    depth, _, n = w_mod.shape
    return pl.pallas_call(
        _mods_body,
        grid=(depth, n // _MODS_COLS),
        in_specs=[pl.BlockSpec((rows, D_MODEL), lambda l, j: (0, 0)),
                  pl.BlockSpec((None, D_MODEL, _MODS_COLS), lambda l, j: (l, 0, j)),
                  pl.BlockSpec((None, 1, _MODS_COLS), lambda l, j: (l, 0, j))],
        out_specs=pl.BlockSpec((None, rows, _MODS_COLS), lambda l, j: (l, 0, j)),
        out_shape=jax.ShapeDtypeStruct((depth, rows, n), F32),
        compiler_params=_cparams(("arbitrary", "arbitrary")),
        name="mods",
    )(c_all, w_mod, b_mod.reshape(depth, 1, n))


def _rms_rows(x, g):
    ms = jnp.mean(x * x, axis=-1, keepdims=True)
    return x * lax.rsqrt(ms + NORM_EPS) * g


def _chunk_mean_square(z, p):
    parts = []
    for j in range(z.shape[1] // MXU_DIM):
        zz = z[:, j * MXU_DIM:(j + 1) * MXU_DIM]
        parts.append(_dot((zz * zz).astype(BF16), p))
    return jnp.concatenate(parts, axis=1)


def _in_proj_body(q_scale, x_ref, sc_ref, sh_ref, g_ref, w_ref, qg_ref, kg_ref, p_ref,
                  q_ref, k_ref, kb_ref, v_ref, xr_ref, yr_ref, ga_ref, gr_ref, vt_ref=None):
    u = (_rms_rows(x_ref[...], g_ref[...]) * (1.0 + sc_ref[...]) + sh_ref[...]).astype(BF16)
    p = p_ref[...]
    zq = _dot(u, w_ref[:, 0:Q_COLS])
    q = zq * lax.rsqrt(_chunk_mean_square(zq, p) + NORM_EPS) * qg_ref[...]
    q_ref[...] = (q * q_scale).astype(BF16)
    zk = _dot(u, w_ref[:, Q_COLS:Q_COLS + K_COLS])
    k = zk * lax.rsqrt(_chunk_mean_square(zk, p) + NORM_EPS) * kg_ref[...]
    k_ref[...] = k
    kb_ref[...] = k.astype(BF16)
    c0 = Q_COLS + K_COLS
    v = _dot(u, w_ref[:, c0:c0 + V_COLS])
    v_ref[...] = v
    if vt_ref is not None:
        row = lax.broadcasted_iota(jnp.int32, (ONES_ROWS, v.shape[0]), 0)
        ones_rows = jnp.where(row == 0, 1.0, 0.0).astype(BF16)
        for h in range(N_KV_HEADS):
            vt_ref[h, 0:V_DIM, :] = v[:, h * V_DIM:(h + 1) * V_DIM].T.astype(BF16)
            vt_ref[h, V_DIM:V_DIM + ONES_ROWS, :] = ones_rows
    c0 += V_COLS
    xr_ref[...] = _dot(u, w_ref[:, c0:c0 + D_MODEL])
    c0 += D_MODEL
    yr_ref[...] = _dot(u, w_ref[:, c0:c0 + D_MODEL])
    c0 += D_MODEL
    ga_ref[...] = _dot(u, w_ref[:, c0:c0 + D_MODEL])
    c0 += D_MODEL
    gr_ref[...] = _dot(u, w_ref[:, c0:c0 + D_MODEL])


def _mod_spec(mods, tile, rows_per_group, col_block):
    if mods.ndim == 3:
        tiles_per_group = rows_per_group // tile
        return pl.BlockSpec((None, 1, D_MODEL), lambda i: (i // tiles_per_group, 0, col_block))
    return pl.BlockSpec((tile, D_MODEL), lambda i: (i, col_block))


def _in_proj(x, mods, rows_per_group, q_scale, vt_block, layer, norm_g, w_in_bf, qg_row, kg_row, chunk_p):
    t = x.shape[0]
    tile = min(TOKEN_TILE, t)
    row = lambda n: pl.BlockSpec((tile, n), lambda i: (i, 0))
    outs = [(Q_COLS, BF16), (K_COLS, F32), (K_COLS, BF16), (V_COLS, F32),
            (D_MODEL, F32), (D_MODEL, F32), (D_MODEL, F32), (D_MODEL, F32)]
    out_specs = [row(n) for n, _ in outs]
    out_shape = [jax.ShapeDtypeStruct((t, n), dt) for n, dt in outs]
    if vt_block is not None:
        tiles_per_seq = rows_per_group // tile
        per_block = vt_block // tile
        vrows = V_DIM + ONES_ROWS
        out_specs.append(pl.BlockSpec(
            (N_KV_HEADS, None, vrows, tile),
            lambda i: (i // tiles_per_seq, (i % tiles_per_seq) // per_block, 0, (i % tiles_per_seq) % per_block)))
        out_shape.append(jax.ShapeDtypeStruct(
            (t // rows_per_group * N_KV_HEADS, rows_per_group // vt_block, vrows, vt_block), BF16))
    return pl.pallas_call(
        functools.partial(_in_proj_body, q_scale),
        grid=(t // tile,),
        in_specs=[row(D_MODEL),
                  _mod_spec(mods, tile, rows_per_group, 1),
                  _mod_spec(mods, tile, rows_per_group, 0),
                  _resident((1, D_MODEL)),
                  pl.BlockSpec((None, D_MODEL, D_IN), lambda i: (layer, 0, 0)),
                  _resident((1, Q_COLS)),
                  _resident((1, K_COLS)),
                  _resident((MXU_DIM, MXU_DIM))],
        out_specs=out_specs,
        out_shape=out_shape,
        compiler_params=_cparams(("arbitrary",)),
        name="in_proj",
    )(x, mods, mods, norm_g, w_in_bf, qg_row, kg_row, chunk_p)


def _bucket_of_distance(n):
    n = jnp.maximum(n, 0)
    nf = jnp.maximum(n, 1).astype(F32)
    large = MAX_EXACT + (jnp.log(nf / MAX_EXACT) / math.log(MAX_DISTANCE / MAX_EXACT)
                         * (NUM_BUCKETS - MAX_EXACT)).astype(jnp.int32)
    return jnp.where(n < MAX_EXACT, n, jnp.minimum(large, NUM_BUCKETS - 1))


def _bias_of_distance(rel_bias, n):
    b = rel_bias.astype(F32)[_bucket_of_distance(n)]
    b = jnp.where((n >= 0)[..., None], b, MASK_VALUE)
    return jnp.moveaxis(b, -1, 0)


def _bias_table(rel_bias, blk, offset):
    j = jnp.arange(2 * blk, dtype=jnp.int32)
    diff = jnp.where(j < blk, j, j - 2 * blk)
    return _bias_of_distance(rel_bias, diff + offset)[:, None, :]


_S_LAM, _S_OUT = 0, 1
_S_FAR = 2
_S_BMAX = _S_FAR + N_HEADS
_S_BRANGE = _S_BMAX + N_HEADS
BOUND_MARGIN = 1.01
SAFE_EXPONENT_SPAN = 100.0


def _attn_prompt_body(scal_ref, q_ref, k_ref, vt_ref, tdiag_ref, tprev_ref, sg_ref,
                      o_ref, qpad_sc, kmax_sc, bq_sc, m_sc, acc_sc, bdiag_ref, bprev_ref):
    h = pl.program_id(1)
    qi = pl.program_id(2)
    blk = q_ref.shape[0]
    ncombo = 2 * GQA_GROUP
    lam = scal_ref[_S_LAM]
    out_scale = scal_ref[_S_OUT]

    @pl.when(qi == 0)
    def _():
        kk = k_ref[...].astype(F32)
        r = lax.broadcasted_iota(jnp.int32, (LANES, LANES), 0) // HEAD_DIM
        c = lax.broadcasted_iota(jnp.int32, (LANES, LANES), 1) // HEAD_DIM
        same = jnp.where(r == c, 1.0, 0.0).astype(BF16)
        ksq = _dot((kk * kk).astype(BF16), same)
        kmax_sc[...] = jnp.sqrt(jnp.max(ksq, axis=0, keepdims=True))
        for table_ref, tile_ref in ((tdiag_ref, bdiag_ref), (tprev_ref, bprev_ref)):
            for g in range(GQA_GROUP):
                rows = jnp.broadcast_to(table_ref[g], (blk, 2 * blk))
                tile_ref[g] = pltpu.roll(rows, 0, 1, stride=1, stride_axis=0)[:, :blk]

    lane = lax.broadcasted_iota(jnp.int32, (blk, LANES), 1)
    ones = jnp.ones((SUBLANES, LANES), BF16)
    span = jnp.zeros((1, 1), F32)
    for g in range(GQA_GROUP):
        q2 = q_ref[:, g * LANES:(g + 1) * LANES]
        for c in range(2):
            idx = 2 * g + c
            qp = jnp.where((lane < HEAD_DIM) == (c == 0), q2, jnp.zeros_like(q2))
            qpad_sc[idx] = qp
            qf = qp.astype(F32)
            qsq = lax.dot_general(ones, (qf * qf).astype(BF16), _NT, preferred_element_type=F32)[0:1]
            kmax = jnp.max(kmax_sc[:, c * HEAD_DIM:(c + 1) * HEAD_DIM], axis=1, keepdims=True)
            bqk = BOUND_MARGIN * jnp.sqrt(qsq) * kmax
            bq_sc[idx] = bqk + scal_ref[_S_BMAX + h * GQA_GROUP + g]
            span = jnp.maximum(span, 2.0 * jnp.max(bqk, axis=1, keepdims=True)
                               + scal_ref[_S_BRANGE + h * GQA_GROUP + g])
    acc_sc[...] = jnp.zeros(acc_sc.shape, F32)
    far_bias = lambda g: scal_ref[_S_FAR + h * GQA_GROUP + g]
    n_far = jnp.maximum(qi - 1, 0)

    def k_block(kj):
        return k_ref[pl.ds(pl.multiple_of(kj * blk, blk), blk), :]

    def fast_chain(steps):
        work = [(s, idx) for s in range(len(steps)) for idx in range(ncombo)]
        k_blks = [k_block(kj) for kj, _, _ in steps]
        vts = [vt_ref[kj] for kj, _, _ in steps]
        qk = lambda w: lax.dot_general(k_blks[w[0]], qpad_sc[w[1]], _NT, preferred_element_type=F32)
        st_next = qk(work[0])
        for n, (s, idx) in enumerate(work):
            st = st_next
            if n + 1 < len(work):
                st_next = qk(work[n + 1])
            _, shift_of_group, tile_of_group = steps[s]
            g = idx // 2
            if tile_of_group is not None:
                st = st + tile_of_group(g)
            bound = bq_sc[idx] if shift_of_group is None else bq_sc[idx] - shift_of_group(g)
            acc_sc[idx] += _dot(vts[s], jnp.exp2(st - bound).astype(BF16))

    def exact_step(kj, shift_of_group, tile_of_group):
        k_blk = k_block(kj)
        vt = vt_ref[kj]
        for idx in range(ncombo):
            g = idx // 2
            st = lax.dot_general(k_blk, qpad_sc[idx], _NT, preferred_element_type=F32)
            if tile_of_group is not None:
                st = st + tile_of_group(g)
            shift = 0.0 if shift_of_group is None else shift_of_group(g)
            m_old = m_sc[idx]
            m_new = jnp.maximum(m_old, jnp.max(st, axis=0, keepdims=True) + shift)
            p = jnp.exp2(st - (m_new - shift)).astype(BF16)
            acc_sc[idx] = jnp.exp2(m_old - m_new) * acc_sc[idx] + _dot(vt, p)
            m_sc[idx] = m_new

    prev_tile = lambda g: bprev_ref[g]
    diag_tile = lambda g: bdiag_ref[g]

    def run_fast():
        def far_pair(j, carry):
            fast_chain([(2 * j, far_bias, None), (2 * j + 1, far_bias, None)])
            return carry
        lax.fori_loop(0, n_far // 2, far_pair, 0)

        @pl.when(n_far % 2 == 1)
        def _():
            fast_chain([(n_far - 1, far_bias, None)])

        @pl.when(qi >= 1)
        def _():
            fast_chain([(qi - 1, None, prev_tile), (qi, None, diag_tile)])

        @pl.when(qi == 0)
        def _():
            fast_chain([(qi, None, diag_tile)])

    def run_exact():
        m_sc[...] = jnp.full(m_sc.shape, -jnp.inf, F32)

        def far_step(kj, carry):
            exact_step(kj, far_bias, None)
            return carry
        lax.fori_loop(0, n_far, far_step, 0)

        @pl.when(qi >= 1)
        def _():
            exact_step(qi - 1, None, prev_tile)
        exact_step(qi, None, diag_tile)

    lax.cond(span[0, 0] <= SAFE_EXPONENT_SPAN, run_fast, run_exact)

    for g in range(GQA_GROUP):
        a0 = acc_sc[2 * g]
        a1 = acc_sc[2 * g + 1]
        o = a0[:V_DIM] / a0[V_DIM:V_DIM + 1] - lam * (a1[:V_DIM] / a1[V_DIM:V_DIM + 1])
        o = o * lax.rsqrt(jnp.mean(o * o, axis=0, keepdims=True) + NORM_EPS)
        o = o * sg_ref[...] * out_scale
        o_ref[:, g * V_DIM:(g + 1) * V_DIM] = o.T.astype(BF16)


def _attn_prompt(scal, q_bf, k_bf, vt_ext, bias_diag, bias_prev, sg_col, batch, seq):
    blk = ATTN_BLOCK
    nblk = seq // blk
    vrows = V_DIM + ONES_ROWS
    ncombo = 2 * GQA_GROUP
    grid_spec = pltpu.PrefetchScalarGridSpec(
        num_scalar_prefetch=1,
        grid=(batch, N_KV_HEADS, nblk),
        in_specs=[pl.BlockSpec((blk, GQA_GROUP * 2 * HEAD_DIM), lambda b, h, i, s: (b * nblk + i, h)),
                  pl.BlockSpec((seq, 2 * HEAD_DIM), lambda b, h, i, s: (b, h)),
                  pl.BlockSpec((None, nblk, vrows, blk), lambda b, h, i, s: (b * N_KV_HEADS + h, 0, 0, 0)),
                  pl.BlockSpec((GQA_GROUP, 1, 2 * blk), lambda b, h, i, s: (h, 0, 0)),
                  pl.BlockSpec((GQA_GROUP, 1, 2 * blk), lambda b, h, i, s: (h, 0, 0)),
                  pl.BlockSpec((V_DIM, 1), lambda b, h, i, s: (0, 0))],
        out_specs=pl.BlockSpec((blk, GQA_GROUP * V_DIM), lambda b, h, i, s: (b * nblk + i, h)),
        scratch_shapes=[pltpu.VMEM((ncombo, blk, LANES), BF16),
                        pltpu.VMEM((1, LANES), F32),
                        pltpu.VMEM((ncombo, 1, blk), F32),
                        pltpu.VMEM((ncombo, 1, blk), F32),
                        pltpu.VMEM((ncombo, vrows, blk), F32),
                        pltpu.VMEM((GQA_GROUP, blk, blk), F32),
                        pltpu.VMEM((GQA_GROUP, blk, blk), F32)],
    )
    return pl.pallas_call(
        _attn_prompt_body,
        grid_spec=grid_spec,
        out_shape=jax.ShapeDtypeStruct((batch * seq, N_HEADS * V_DIM), BF16),
        compiler_params=_cparams(("arbitrary", "arbitrary", "arbitrary")),
        name="attn_prompt",
    )(scal, q_bf, k_bf, vt_ext, bias_diag, bias_prev, sg_col)


def _attn_sample_body(n_pages, pt_ref, layer_ref, scal_ref, q_ref, knew_ref, vnew_ref, bias_ref, sg_ref, *refs):
    k_pages = refs[:n_pages]
    v_pages = refs[n_pages:2 * n_pages]
    o_ref, kb_sc, vb_sc = refs[2 * n_pages:]
    page_rows = k_pages[0].shape[2]
    past = n_pages * page_rows
    n_new = knew_ref.shape[1]
    lam = scal_ref[0]
    out_scale = scal_ref[1]
    half = q_ref.shape[1] // 2

    @pl.when(pl.program_id(0) == 0)
    def _():
        tail = kb_sc.shape[0] - past - n_new
        kb_sc[past + n_new:, :] = jnp.zeros((tail, LANES), BF16)
        vb_sc[past + n_new:, :] = jnp.zeros((tail, LANES), BF16)

    for j in range(n_pages):
        kb_sc[j * page_rows:(j + 1) * page_rows, :] = k_pages[j][0, 0].astype(BF16)
        vb_sc[j * page_rows:(j + 1) * page_rows, :] = v_pages[j][0, 0].astype(BF16)
    kb_sc[past:past + n_new, :] = knew_ref[0]
    vb_sc[past:past + n_new, :] = vnew_ref[0]
    s = lax.dot_general(q_ref[0], kb_sc[...], _NT, preferred_element_type=F32) + bias_ref[...]
    p = jnp.exp(s - jnp.max(s, axis=-1, keepdims=True))
    p = p / jnp.sum(p, axis=-1, keepdims=True)
    a = p[:half] - lam * p[half:]
    o = _dot(a.astype(BF16), vb_sc[...])
    o = o * lax.rsqrt(jnp.mean(o * o, axis=-1, keepdims=True) + NORM_EPS)
    o_ref[0] = o * sg_ref[...] * out_scale


def _attn_sample(page_table_flat, layer_idx, scal, q_rows, knew, vnew, bias_s, sg_row, cache_k, cache_v, n_pages):
    nb, rows, _ = q_rows.shape
    n_new = knew.shape[1]
    page_rows = cache_k.shape[2]
    keys = bias_s.shape[1]

    def page_spec(j):
        return pl.BlockSpec((1, 1, page_rows, LANES), lambda b, pt, l: (l[0], pt[b * n_pages + j], 0, 0))

    grid_spec = pltpu.PrefetchScalarGridSpec(
        num_scalar_prefetch=2,
        grid=(nb,),
        in_specs=[pl.BlockSpec(memory_space=pltpu.SMEM),
                  pl.BlockSpec((1, rows, LANES), lambda b, pt, l: (b, 0, 0)),
                  pl.BlockSpec((1, n_new, LANES), lambda b, pt, l: (b, 0, 0)),
                  pl.BlockSpec((1, n_new, LANES), lambda b, pt, l: (b, 0, 0)),
                  pl.BlockSpec((rows, keys), lambda b, pt, l: (0, 0)),
                  pl.BlockSpec((1, V_DIM), lambda b, pt, l: (0, 0))]
                 + [page_spec(j) for j in range(n_pages)] * 2,
        out_specs=pl.BlockSpec((1, rows // 2, V_DIM), lambda b, pt, l: (b, 0, 0)),
        scratch_shapes=[pltpu.VMEM((keys, LANES), BF16),
                        pltpu.VMEM((keys, LANES), BF16)],
    )
    return pl.pallas_call(
        functools.partial(_attn_sample_body, n_pages),
        grid_spec=grid_spec,
        out_shape=jax.ShapeDtypeStruct((nb, rows // 2, V_DIM), F32),
        compiler_params=_cparams(("arbitrary",)),
        name="attn_sample",
    )(page_table_flat, layer_idx, scal, q_rows, knew, vnew, bias_s, sg_row,
      *([cache_k] * n_pages), *([cache_v] * n_pages))


def _lru_coeffs(xc, wa_ref, wi_ref, ba_ref, bi_ref, sp_ref):
    xb = xc.astype(BF16)
    ra, ri = [], []
    for j in range(xc.shape[1] // MXU_DIM):
        cols = slice(j * MXU_DIM, (j + 1) * MXU_DIM)
        ra.append(_dot(xb[:, cols], wa_ref[j]))
        ri.append(_dot(xb[:, cols], wi_ref[j]))
    r = jax.nn.sigmoid(jnp.concatenate(ra, axis=1) + ba_ref[...])
    i = jax.nn.sigmoid(jnp.concatenate(ri, axis=1) + bi_ref[...])
    log_a = -LRU_C * r * sp_ref[...]
    a = jnp.exp(log_a)
    mult = jnp.sqrt(-jnp.tanh(log_a) * (a * a + 1.0))
    return a, mult, i


def _rglru_prompt_body(xr_ref, yr_ref, cw_ref, cb_ref, wa_ref, wi_ref, ba_ref, bi_ref, sp_ref,
                       rnn_ref, hl_ref, xe_sc, a_sc, b_sc, h_sc):
    i = pl.program_id(1)
    tt = xr_ref.shape[0]

    @pl.when(i == 0)
    def _():
        xe_sc[0:SUBLANES, :] = jnp.zeros((SUBLANES, D_MODEL), F32)
        h_sc[...] = jnp.zeros(h_sc.shape, F32)

    xe_sc[SUBLANES:SUBLANES + tt, :] = xr_ref[...]
    xc = cb_ref[...]
    for j in range(CONV_W):
        off = SUBLANES - (CONV_W - 1) + j
        xc = xc + xe_sc[off:off + tt, :] * cw_ref[j:j + 1, :]
    a, mult, gate_i = _lru_coeffs(xc, wa_ref, wi_ref, ba_ref, bi_ref, sp_ref)
    row = lax.broadcasted_iota(jnp.int32, (tt, 1), 0)
    mult = jnp.where((row == 0) & (i == 0), 1.0, mult)
    a_sc[...] = a
    b_sc[...] = mult * gate_i * xc

    def step(t, h):
        h = a_sc[pl.ds(t, 1), :] * h + b_sc[pl.ds(t, 1), :]
        b_sc[pl.ds(t, 1), :] = h
        return h

    h = lax.fori_loop(0, tt, step, h_sc[...], unroll=8)
    h_sc[...] = h
    hl_ref[...] = h
    rnn_ref[...] = (b_sc[...] * jax.nn.gelu(yr_ref[...])).astype(BF16)
    xe_sc[0:SUBLANES, :] = xe_sc[tt:tt + SUBLANES, :]


def _rglru_prompt(xr, yr, cw, cb, wa_bd, wi_bd, ba, bi, sp, batch, seq):
    tt = min(TIME_TILE, seq)
    nt = seq // tt
    row = pl.BlockSpec((tt, D_MODEL), lambda b, i: (b * nt + i, 0))
    nbd = D_MODEL // MXU_DIM
    return pl.pallas_call(
        _rglru_prompt_body,
        grid=(batch, nt),
        in_specs=[row, row,
                  pl.BlockSpec((CONV_W, D_MODEL), lambda b, i: (0, 0)),
                  pl.BlockSpec((1, D_MODEL), lambda b, i: (0, 0)),
                  pl.BlockSpec((nbd, MXU_DIM, MXU_DIM), lambda b, i: (0, 0, 0)),
                  pl.BlockSpec((nbd, MXU_DIM, MXU_DIM), lambda b, i: (0, 0, 0)),
                  pl.BlockSpec((1, D_MODEL), lambda b, i: (0, 0)),
                  pl.BlockSpec((1, D_MODEL), lambda b, i: (0, 0)),
                  pl.BlockSpec((1, D_MODEL), lambda b, i: (0, 0))],
        out_specs=[row, pl.BlockSpec((None, 1, D_MODEL), lambda b, i: (b, 0, 0))],
        out_shape=[jax.ShapeDtypeStruct((batch * seq, D_MODEL), BF16),
                   jax.ShapeDtypeStruct((batch, 1, D_MODEL), F32)],
        scratch_shapes=[pltpu.VMEM((tt + SUBLANES, D_MODEL), F32),
                        pltpu.VMEM((tt, D_MODEL), F32),
                        pltpu.VMEM((tt, D_MODEL), F32),
                        pltpu.VMEM((1, D_MODEL), F32)],
        compiler_params=_cparams(("arbitrary", "arbitrary")),
        name="rglru_prompt",
    )(xr, yr, cw, cb, wa_bd, wi_bd, ba, bi, sp)


def _rglru_sample_body(n_steps, first_is_start, xr_ref, yr_ref, buf_ref, h0_ref, cw_ref, cb_ref, wa_ref, wi_ref, ba_ref, bi_ref,
                       sp_ref, rnn_ref, hl_ref):
    nb = h0_ref.shape[0]
    taps = [buf_ref[j] for j in range(CONV_W - 1)] + [xr_ref[t * nb:(t + 1) * nb, :] for t in range(n_steps)]
    h = h0_ref[...]
    for t in range(n_steps):
        xc = cb_ref[...]
        for j in range(CONV_W):
            xc = xc + taps[t + j] * cw_ref[j:j + 1, :]
        a, mult, gate_i = _lru_coeffs(xc, wa_ref, wi_ref, ba_ref, bi_ref, sp_ref)
        if first_is_start and t == 0:
            mult = jnp.ones_like(mult)
        h = a * h + mult * gate_i * xc
        rnn_ref[t * nb:(t + 1) * nb, :] = (h * jax.nn.gelu(yr_ref[t * nb:(t + 1) * nb, :])).astype(BF16)
    hl_ref[...] = h


def _rglru_sample(xr, yr, buf, h0, cw, cb, wa_bd, wi_bd, ba, bi, sp, n_steps, first_is_start):
    t = xr.shape[0]
    return pl.pallas_call(
        functools.partial(_rglru_sample_body, n_steps, first_is_start),
        out_shape=[jax.ShapeDtypeStruct((t, D_MODEL), BF16),
                   jax.ShapeDtypeStruct(h0.shape, F32)],
        compiler_params=pltpu.CompilerParams(vmem_limit_bytes=VMEM_LIMIT_BYTES),
        name="rglru_sample",
    )(xr, yr, buf, h0, cw, cb, wa_bd, wi_bd, ba, bi, sp)


def _route(logits):
    lane = lax.broadcasted_iota(jnp.int32, logits.shape, 1).astype(F32)
    none = float(ROUTER_LANES)
    is_group = lane < N_GROUPS
    lg = jnp.where(is_group, logits, -jnp.inf)
    eg = jnp.exp(lg - jnp.max(lg, axis=-1, keepdims=True))
    gp = eg / jnp.sum(eg, axis=-1, keepdims=True)
    gw = jnp.max(gp, axis=-1, keepdims=True)
    gi = jnp.min(jnp.where(is_group & (gp == gw), lane, none), axis=-1, keepdims=True)
    first = N_GROUPS + gi * EXPERTS_PER_GROUP
    in_group = (lane >= first) & (lane < first + EXPERTS_PER_GROUP)
    le = jnp.where(in_group, logits, -jnp.inf)
    ee = jnp.exp(le - jnp.max(le, axis=-1, keepdims=True))
    pe = ee / jnp.sum(ee, axis=-1, keepdims=True)
    p1 = jnp.max(pe, axis=-1, keepdims=True)
    i1 = jnp.min(jnp.where(in_group & (pe == p1), lane, none), axis=-1, keepdims=True)
    rest = in_group & (lane != i1)
    pr = jnp.where(rest, pe, -1.0)
    p2 = jnp.max(pr, axis=-1, keepdims=True)
    i2 = jnp.min(jnp.where(rest & (pr == p2), lane, none), axis=-1, keepdims=True)
    den = p1 + p2
    w1 = gw * (p1 / den)
    w2 = gw * (p2 / den)
    rec = jnp.where(lane == 0.0, i1 - N_GROUPS, 0.0)
    rec = jnp.where(lane == 1.0, i2 - N_GROUPS, rec)
    rec = jnp.where(lane == 2.0, w1, rec)
    rec = jnp.where(lane == 3.0, w2, rec)
    return rec, jnp.where(lane == i1, 1.0, 0.0), jnp.where(lane == i2, 1.0, 0.0)


def _merge_body(x_ref, attn_ref, rnn_ref, ga_ref, gr_ref, g1_ref, sc2_ref, sh2_ref, n2g_ref,
                wpa_ref, wpr_ref, wo_ref, wrh_ref, wrl_ref, br_ref, ltri_ref, cnt_in_ref,
                x1_ref, u2_ref, rec_ref, cnt_ref, cnt_sc):
    tile = x_ref.shape[0]
    n_parts = 2 if tile % (2 * SUBLANES) == 0 else 1
    parts = [slice(p * tile // n_parts, (p + 1) * tile // n_parts) for p in range(n_parts)]
    per_row = lambda ref, rows: ref[rows, :] if ref.shape[0] == tile else ref[...]
    def proj(rows):
        return _dot(attn_ref[rows, :], wpa_ref[...]), _dot(rnn_ref[rows, :], wpr_ref[...])

    def mixed(rows, a, r):
        m = jax.nn.sigmoid(ga_ref[rows, :]) * a + jax.nn.sigmoid(gr_ref[rows, :]) * r
        return _dot(m.astype(BF16), wo_ref[...])

    def residual(rows, mx):
        x1 = x_ref[rows, :] + per_row(g1_ref, rows) * mx
        x1_ref[rows, :] = x1
        u2 = _rms_rows(x1, n2g_ref[...]) * (1.0 + per_row(sc2_ref, rows)) + per_row(sh2_ref, rows)
        u2_ref[rows, :] = u2
        u_hi = u2.astype(BF16)
        u_lo = (u2 - u_hi.astype(F32)).astype(BF16)
        return (_dot(u_hi, wrh_ref[...]) + _dot(u_lo, wrh_ref[...]) + _dot(u_hi, wrl_ref[...])) + br_ref[...]

    projs = [proj(parts[0])]
    mixes, logits = [], []
    for p in range(n_parts):
        if p + 1 < n_parts:
            projs.append(proj(parts[p + 1]))
        mixes.append(mixed(parts[p], *projs[p]))
        if p >= 1:
            logits.append(residual(parts[p - 1], mixes[p - 1]))
    logits.append(residual(parts[-1], mixes[-1]))
    rec, hot1, hot2 = _route(jnp.concatenate(logits, axis=0))

    @pl.when(pl.program_id(0) == 0)
    def _():
        cnt_sc[...] = cnt_in_ref[...]
    cnt = cnt_sc[...]
    tot1 = jnp.sum(hot1, axis=0, keepdims=True)
    tot2 = jnp.sum(hot2, axis=0, keepdims=True)
    before1 = _dot(ltri_ref[...], hot1.astype(BF16)) + cnt
    before2 = _dot(ltri_ref[...], hot2.astype(BF16)) + (cnt + tot1)
    rank1 = jnp.sum(hot1 * before1, axis=-1, keepdims=True)
    rank2 = jnp.sum(hot2 * before2, axis=-1, keepdims=True)
    cnt_sc[...] = cnt + tot1 + tot2
    cnt_ref[...] = cnt_sc[...]
    lane = lax.broadcasted_iota(jnp.int32, rec.shape, 1)
    rec = jnp.where(lane == 4, rank1, rec)
    rec_ref[...] = jnp.where(lane == 5, rank2, rec)


def _merge(x, attn, rnn, ga, gr, mods, rows_per_group, cnt_in, n2g, wpa, wpr, wo, wr_hi, wr_lo, br):
    t = x.shape[0]
    tile = min(MERGE_TILE, t)
    assert t % tile == 0 and (mods.ndim == 2 or rows_per_group % tile == 0)
    row = lambda n: pl.BlockSpec((tile, n), lambda i: (i, 0))
    sq = _resident((D_MODEL, D_MODEL))
    r = jnp.arange(tile)
    ltri = (r[None, :] < r[:, None]).astype(BF16)
    return pl.pallas_call(
        _merge_body,
        grid=(t // tile,),
        in_specs=[row(D_MODEL), row(D_MODEL), row(D_MODEL), row(D_MODEL), row(D_MODEL),
                  _mod_spec(mods, tile, rows_per_group, 2),
                  _mod_spec(mods, tile, rows_per_group, 4),
                  _mod_spec(mods, tile, rows_per_group, 3),
                  _resident((1, D_MODEL)), sq, sq, sq,
                  _resident((D_MODEL, ROUTER_LANES)), _resident((D_MODEL, ROUTER_LANES)),
                  _resident((1, ROUTER_LANES)), _resident((tile, tile)), _resident((1, ROUTER_LANES))],
        out_specs=[row(D_MODEL), row(D_MODEL), row(ROUTER_LANES), _resident((1, ROUTER_LANES))],
        out_shape=[jax.ShapeDtypeStruct((t, D_MODEL), F32),
                   jax.ShapeDtypeStruct((t, D_MODEL), F32),
                   jax.ShapeDtypeStruct((t, ROUTER_LANES), F32),
                   jax.ShapeDtypeStruct((1, ROUTER_LANES), F32)],
        scratch_shapes=[pltpu.VMEM((1, ROUTER_LANES), F32)],
        compiler_params=_cparams(("arbitrary",)),
        name="merge",
    )(x, attn, rnn, ga, gr, mods, mods, mods, n2g, wpa, wpr, wo, wr_hi, wr_lo, br, ltri, cnt_in)


ROW_UNROLL = 8


def _for_rows(n_rows, fn):
    def body(j, carry):
        for s in range(ROW_UNROLL):
            fn(j * ROW_UNROLL + s)
        return carry
    lax.fori_loop(0, n_rows // ROW_UNROLL, body, 0)


def _dispatch_body(t_all, n_first, pos_ref, ua_ref, ub_ref, init_hbm, us_hbm, sem):
    del init_hbm
    i = pl.program_id(0)

    def scatter_tile(u_ref, base):
        tile = u_ref.shape[0]

        def issue(r):
            for slot in range(2):
                pltpu.make_async_copy(u_ref.at[pl.ds(r, 1)],
                                      us_hbm.at[pl.ds(pos_ref[slot * t_all + base + r], 1)], sem.at[0]).start()
        _for_rows(tile, issue)
        for _ in range(2):
            pltpu.make_async_copy(u_ref, us_hbm.at[pl.ds(0, tile)], sem.at[0]).wait()

    @pl.when(i < n_first)
    def _():
        scatter_tile(ua_ref, i * ua_ref.shape[0])

    @pl.when(i >= n_first)
    def _():
        scatter_tile(ub_ref, n_first * ua_ref.shape[0] + (i - n_first) * ub_ref.shape[0])


def _dispatch(pos_flat, u_first, u_second, init):
    n_sorted_rows = init.shape[0]
    t_all = u_first.shape[0] + u_second.shape[0]
    tile_a = min(TOKEN_TILE, u_first.shape[0])
    tile_b = min(TOKEN_TILE, u_second.shape[0])
    n_a = u_first.shape[0] // tile_a
    n_b = u_second.shape[0] // tile_b
    assert n_a * tile_a == u_first.shape[0] and n_b * tile_b == u_second.shape[0]
    assert tile_a % ROW_UNROLL == 0 and tile_b % ROW_UNROLL == 0
    grid_spec = pltpu.PrefetchScalarGridSpec(
        num_scalar_prefetch=1,
        grid=(n_a + n_b,),
        in_specs=[pl.BlockSpec((tile_a, D_MODEL), lambda i, pos: (jnp.minimum(i, n_a - 1), 0)),
                  pl.BlockSpec((tile_b, D_MODEL), lambda i, pos: (jnp.maximum(i - n_a, 0), 0)),
                  pl.BlockSpec(memory_space=pl.ANY)],
        out_specs=pl.BlockSpec(memory_space=pl.ANY),
        scratch_shapes=[pltpu.SemaphoreType.DMA((1,))],
    )
    return pl.pallas_call(
        functools.partial(_dispatch_body, t_all, n_a),
        grid_spec=grid_spec,
        out_shape=jax.ShapeDtypeStruct((n_sorted_rows, D_MODEL), F32),
        input_output_aliases={3: 0},
        compiler_params=_cparams(("arbitrary",)),
        name="dispatch",
    )(pos_flat, u_first, u_second, init)


def _experts_body(te_ref, nv_ref, last_ref, layer_ref, u_ref, wg_ref, wu_ref, wd_ref, y_ref, wg_sc, wu_sc, wd_sc):
    del last_ref, layer_ref
    i = pl.program_id(0)

    @pl.when((i == 0) | (te_ref[i] != te_ref[jnp.maximum(i - 1, 0)]))
    def _():
        wg_sc[...] = wg_ref[...].astype(BF16)
        wu_sc[...] = wu_ref[...].astype(BF16)
        wd_sc[...] = wd_ref[...].astype(BF16)

    @pl.when(nv_ref[i] > 0)
    def _():
        u = u_ref[...].astype(BF16)
        hg = _dot(u, wg_sc[...])
        hid = (hg * jax.nn.sigmoid(hg)) * _dot(u, wu_sc[...])
        y_ref[...] = _dot(hid.astype(BF16), wd_sc[...])

    @pl.when(nv_ref[i] == 0)
    def _():
        y_ref[...] = jnp.zeros(y_ref.shape, F32)


def _experts(tile_expert, tile_valid, last_tile, layer_idx, u_sorted, w_gate, w_up, w_down):
    tile = EXPERT_TILE
    n_tiles = tile_expert.shape[0]
    wspec = lambda a, b: pl.BlockSpec((None, None, a, b), lambda i, te, nv, last, lyr: (lyr[0], te[i], 0, 0))
    grid_spec = pltpu.PrefetchScalarGridSpec(
        num_scalar_prefetch=4,
        grid=(n_tiles,),
        in_specs=[pl.BlockSpec((tile, D_MODEL), lambda i, te, nv, last, lyr: (jnp.minimum(i, last[0]), 0)),
                  wspec(D_MODEL, D_EXPERT), wspec(D_MODEL, D_EXPERT), wspec(D_EXPERT, D_MODEL)],
        out_specs=pl.BlockSpec((tile, D_MODEL), lambda i, te, nv, last, lyr: (i, 0)),
        scratch_shapes=[pltpu.VMEM((D_MODEL, D_EXPERT), BF16),
                        pltpu.VMEM((D_MODEL, D_EXPERT), BF16),
                        pltpu.VMEM((D_EXPERT, D_MODEL), BF16)],
    )
    return pl.pallas_call(
        _experts_body,
        grid_spec=grid_spec,
        out_shape=jax.ShapeDtypeStruct(u_sorted.shape, F32),
        compiler_params=_cparams(("arbitrary",)),
        name="experts",
    )(tile_expert, tile_valid, last_tile, layer_idx, u_sorted, w_gate, w_up, w_down)


def _dispatch_plan(rec_all, counts_row):
    t_all = rec_all.shape[0]
    tile = EXPERT_TILE
    n_tiles = -(-2 * t_all // tile) + N_EXPERTS
    counts = counts_row[0, N_GROUPS:N_GROUPS + N_EXPERTS].astype(jnp.int32)
    tiles_per_e = (counts + tile - 1) // tile
    tile_end = jnp.cumsum(tiles_per_e)
    tile_start = tile_end - tiles_per_e
    experts = jnp.arange(N_EXPERTS, dtype=jnp.int32)
    e = rec_all[:, 0:2].astype(jnp.int32)
    rank = rec_all[:, 4:6].astype(jnp.int32)
    row_start = jnp.sum(jnp.where(e[:, :, None] == experts, tile_start * tile, 0), axis=-1)
    pos_flat = (row_start + rank).T.reshape(-1)
    n_used = tile_end[-1]
    tile_ids = jnp.arange(n_tiles, dtype=jnp.int32)
    clamped = jnp.minimum(tile_ids, n_used - 1)
    te = jnp.sum((clamped[:, None] >= tile_end[None, :]).astype(jnp.int32), axis=-1)
    te = jnp.minimum(te, N_EXPERTS - 1)
    first = jnp.sum(jnp.where(te[:, None] == experts, tile_start, 0), axis=-1)
    cnt_t = jnp.sum(jnp.where(te[:, None] == experts, counts, 0), axis=-1)
    valid = jnp.clip(cnt_t - (tile_ids - first) * tile, 0, tile)
    valid = jnp.where(tile_ids < n_used, valid, 0).astype(jnp.int32)
    return pos_flat, te, valid, (n_used - 1).reshape(1).astype(jnp.int32), n_tiles * tile


def _combine_body(t_all, row_offset, pos_ref, x_ref, rec_ref, g2_ref, ys_hbm, o_ref, ybuf, sem):
    i = pl.program_id(0)
    n = pl.num_programs(0)
    tile = x_ref.shape[0]
    slot = i % 2

    def start_gather(tile_idx, buf):
        base = row_offset + tile_idx * tile

        def issue(r):
            for k in range(2):
                pltpu.make_async_copy(ys_hbm.at[pl.ds(pos_ref[k * t_all + base + r], 1)],
                                      ybuf.at[buf, k, pl.ds(r, 1)], sem.at[buf]).start()
        _for_rows(tile, issue)

    @pl.when(i == 0)
    def _():
        start_gather(0, 0)

    @pl.when(i + 1 < n)
    def _():
        start_gather(i + 1, 1 - slot)

    for k in range(2):
        pltpu.make_async_copy(ys_hbm.at[pl.ds(0, tile)], ybuf.at[slot, k], sem.at[slot]).wait()
    w1 = rec_ref[:, 2:3]
    w2 = rec_ref[:, 3:4]
    o_ref[...] = x_ref[...] + g2_ref[...] * (w1 * ybuf[slot, 0] + w2 * ybuf[slot, 1])


def _combine(pos_flat, x1, rec, y_sorted, mods, rows_per_group, row_offset, t_all):
    t = x1.shape[0]
    tile = min(TOKEN_TILE, t)
    assert tile % ROW_UNROLL == 0
    if mods.ndim == 3:
        tiles_per_group = rows_per_group // tile
        g2_spec = pl.BlockSpec((None, 1, D_MODEL), lambda i, pos: (i // tiles_per_group, 0, 5))
    else:
        g2_spec = pl.BlockSpec((tile, D_MODEL), lambda i, pos: (i, 5))
    grid_spec = pltpu.PrefetchScalarGridSpec(
        num_scalar_prefetch=1,
        grid=(t // tile,),
        in_specs=[pl.BlockSpec((tile, D_MODEL), lambda i, pos: (i, 0)),
                  pl.BlockSpec((tile, ROUTER_LANES), lambda i, pos: (i, 0)),
                  g2_spec,
                  pl.BlockSpec(memory_space=pl.ANY)],
        out_specs=pl.BlockSpec((tile, D_MODEL), lambda i, pos: (i, 0)),
        scratch_shapes=[pltpu.VMEM((2, 2, tile, D_MODEL), F32),
                        pltpu.SemaphoreType.DMA((2,))],
    )
    return pl.pallas_call(
        functools.partial(_combine_body, t_all, row_offset),
        grid_spec=grid_spec,
        out_shape=jax.ShapeDtypeStruct((t, D_MODEL), F32),
        compiler_params=_cparams(("arbitrary",)),
        name="combine",
    )(pos_flat, x1, rec, mods, y_sorted)


def _block_diag(w):
    per = MXU_DIM // RNN_BLOCK
    w4 = w.reshape(N_RNN_BLOCKS // per, per, RNN_BLOCK, RNN_BLOCK)
    eye = jnp.eye(per, dtype=w.dtype)
    return jnp.einsum("jmcd,mn->jmcnd", w4, eye).reshape(N_RNN_BLOCKS // per, MXU_DIM, MXU_DIM)


def kernel(x_prompt, x_sample, cache_k, cache_v, state_rnn, state_conv, page_table, c_prompt, c_sample, rel_bias, w_mod, b_mod, norm1_g, norm2_g, w_in, q_norm_g, k_norm_g, lam_q1, lam_k1, lam_q2, lam_k2, subln_g, conv_w, conv_b, lru_wa, lru_ba, lru_wi, lru_bi, lru_lambda, w_pa, w_pr, w_o, w_rg, b_rg, w_re, b_re, w_e_gate, w_e_up, w_e_down):
    bp, tp, _ = x_prompt.shape
    bs, ts, _ = x_sample.shape
    depth = w_in.shape[0]
    n_pool = cache_k.shape[1]
    n_pages = page_table.shape[1]
    past_len = n_pages * PAGE_SIZE
    t_p = bp * tp
    t_s = bs * ts
    t_all = t_p + t_s
    blk = ATTN_BLOCK
    nblk = tp // blk
    assert tp % blk == 0 and tp % TIME_TILE == 0 and t_p % TOKEN_TILE == 0
    assert t_s % SUBLANES == 0 and t_p % min(TOKEN_TILE, t_s) == 0
    log2e = math.log2(math.e)

    rows_c = -(-(bp + bs) // SUBLANES) * SUBLANES
    c_all = jnp.zeros((rows_c, D_MODEL), F32).at[:bp].set(c_prompt).at[bp:bp + bs].set(c_sample)
    mods = _mods(c_all, w_mod, b_mod)
    mods_p = mods[:, :bp].reshape(depth, bp, 1, 6 * D_MODEL)
    mods_s = jnp.tile(mods[:, bp:bp + bs], (1, ts, 1))

    w_in_bf = w_in.astype(BF16)
    w_pa_bf, w_pr_bf, w_o_bf = w_pa.astype(BF16), w_pr.astype(BF16), w_o.astype(BF16)
    qg_row = jnp.tile(q_norm_g, (1, Q_COLS // HEAD_DIM)).reshape(depth, 1, Q_COLS)
    kg_row = jnp.tile(k_norm_g, (1, K_COLS // HEAD_DIM)).reshape(depth, 1, K_COLS)
    seg = jnp.arange(MXU_DIM) // HEAD_DIM
    chunk_p = ((seg[:, None] == seg[None, :]).astype(F32) / HEAD_DIM).astype(BF16)
    wr = jnp.concatenate([w_rg, w_re], axis=-1)
    wr = jnp.pad(wr, ((0, 0), (0, 0), (0, ROUTER_LANES - wr.shape[-1])))
    wr_hi = wr.astype(BF16)
    wr_lo = (wr - wr_hi.astype(F32)).astype(BF16)
    br = jnp.pad(jnp.concatenate([b_rg, b_re], axis=-1), ((0, 0), (0, ROUTER_LANES - N_GROUPS - N_EXPERTS)))
    br = br.reshape(depth, 1, ROUTER_LANES)
    wa_bd = jax.vmap(_block_diag)(lru_wa).astype(BF16)
    wi_bd = jax.vmap(_block_diag)(lru_wi).astype(BF16)
    ba = lru_ba.reshape(depth, 1, D_MODEL)
    bi = lru_bi.reshape(depth, 1, D_MODEL)
    sp = jax.nn.softplus(-lru_lambda.astype(F32)).reshape(depth, 1, D_MODEL)
    cb = conv_b.reshape(depth, 1, D_MODEL)
    lam_f = lambda a, b: jnp.exp(jnp.sum(a.astype(F32) * b.astype(F32), axis=-1))
    lam_base = lam_f(lam_q1, lam_k1) - lam_f(lam_q2, lam_k2)

    bias_diag = _bias_table(rel_bias, blk, 0) * log2e
    bias_prev = _bias_table(rel_bias, blk, blk) * log2e
    rb = rel_bias.astype(F32) * log2e
    bias_scal = jnp.concatenate([rb[NUM_BUCKETS - 1],
                                 jnp.max(rb, axis=0), jnp.max(rb, axis=0) - jnp.min(rb, axis=0)])

    page_rows = PAGE_SIZE * N_KV_HEADS
    n_new = -(-ts * N_KV_HEADS // (2 * SUBLANES)) * (2 * SUBLANES)
    keys_s = n_pages * page_rows + -(-n_new // LANES) * LANES
    n_pos = past_len + ts
    dist = jnp.arange(-(ts - 1), n_pos, dtype=jnp.int32)
    flipped = _bias_of_distance(rel_bias, dist)[:, ::-1]
    by_token = jnp.stack([flipped[:, ts - 1 - t:ts - 1 - t + n_pos] for t in range(ts)], axis=1)
    by_row = jnp.transpose(by_token.reshape(N_KV_HEADS, GQA_GROUP, ts, n_pos), (0, 2, 1, 3))
    own_head = (jnp.arange(N_KV_HEADS)[:, None, None, None, None]
                == jnp.arange(N_KV_HEADS)[None, None, None, None, :])
    bias_rows = jnp.where(own_head, by_row[..., None], MASK_VALUE)
    bias_rows = bias_rows.reshape(N_KV_HEADS * ts * GQA_GROUP, n_pos * N_KV_HEADS)
    bias_rows = jnp.pad(bias_rows, ((0, 0), (0, keys_s - n_pos * N_KV_HEADS)), constant_values=MASK_VALUE)
    bias_s = jnp.concatenate([bias_rows, bias_rows], axis=0)
    pt_flat = page_table.reshape(-1).astype(jnp.int32)
    cache_k_rows = cache_k.reshape(depth, n_pool, page_rows, LANES)
    cache_v_rows = cache_v.reshape(depth, n_pool, page_rows, LANES)

    xp = x_prompt.reshape(t_p, D_MODEL)
    xs = jnp.swapaxes(x_sample, 0, 1).reshape(t_s, D_MODEL)
    outs = [[] for _ in range(8)]
    for l in range(depth):
        lam_init = 0.8 - 0.6 * math.exp(-0.3 * l)
        lam = lam_base[l] + lam_init
        sg_col = subln_g[l].astype(F32).reshape(V_DIM, 1)
        sg_row = subln_g[l].astype(F32).reshape(1, V_DIM)
        scal_s = jnp.stack([lam, jnp.asarray(1.0 - lam_init, F32)]).astype(F32)
        scal_p = jnp.concatenate([scal_s, bias_scal])
        lp = (norm1_g[l].reshape(1, D_MODEL), w_in_bf, qg_row[l], kg_row[l], chunk_p)
        rg = (conv_w[l], cb[l], wa_bd[l], wi_bd[l], ba[l], bi[l], sp[l])
        mg = (norm2_g[l].reshape(1, D_MODEL), w_pa_bf[l], w_pr_bf[l], w_o_bf[l], wr_hi[l], wr_lo[l], br[l])

        q, k, kb, v, xr, yr, ga, gr, vt_ext = _in_proj(xp, mods_p[l], tp, ATTN_SCALE * log2e, blk, l, *lp)
        attn = _attn_prompt(scal_p, q, kb, vt_ext, bias_diag, bias_prev, sg_col, bp, tp)
        rnn, h_last = _rglru_prompt(xr, yr, *rg, bp, tp)
        x1p, u2p, recp, cnt = _merge(xp, attn, rnn, ga, gr, mods_p[l], tp,
                                     jnp.zeros((1, ROUTER_LANES), F32), *mg)
        outs[0].append(k.reshape(bp, tp, N_KV_HEADS, 2 * HEAD_DIM))
        outs[1].append(v.reshape(bp, tp, N_KV_HEADS, V_DIM))
        outs[2].append(h_last.reshape(bp, D_MODEL))
        outs[3].append(xr.reshape(bp, tp, D_MODEL)[:, tp - (CONV_W - 1):])

        q, k, kb, v, xr, yr, ga, gr = _in_proj(xs, mods_s[l], ts, ATTN_SCALE, None, l, *lp)
        q6 = q.reshape(ts, bs, N_KV_HEADS, GQA_GROUP, 2, HEAD_DIM)
        q6 = jnp.transpose(q6, (1, 4, 2, 0, 3, 5))
        zq = jnp.zeros_like(q6[:, 0])
        q_rows = jnp.stack([jnp.concatenate([q6[:, 0], zq], axis=-1),
                            jnp.concatenate([zq, q6[:, 1]], axis=-1)], axis=1)
        q_rows = q_rows.reshape(bs, 2 * N_KV_HEADS * ts * GQA_GROUP, 2 * HEAD_DIM)
        new_rows = lambda z: jnp.pad(jnp.swapaxes(z.reshape(ts, bs, N_KV_HEADS, LANES), 0, 1)
                                     .reshape(bs, ts * N_KV_HEADS, LANES),
                                     ((0, 0), (0, n_new - ts * N_KV_HEADS), (0, 0)))
        o_s = _attn_sample(pt_flat, jnp.full((1,), l, jnp.int32), scal_s, q_rows, new_rows(kb),
                           new_rows(v.astype(BF16)), bias_s, sg_row, cache_k_rows, cache_v_rows, n_pages)
        attn = jnp.transpose(o_s.reshape(bs, N_KV_HEADS, ts, GQA_GROUP, V_DIM), (2, 0, 1, 3, 4))
        attn = attn.reshape(t_s, N_HEADS * V_DIM).astype(BF16)
        buf = jnp.swapaxes(state_conv[l], 0, 1)
        rnn, h_last = _rglru_sample(xr, yr, buf, state_rnn[l], *rg, ts, past_len == 0)
        x1s, u2s, recs, cnt = _merge(xs, attn, rnn, ga, gr, mods_s[l], ts, cnt, *mg)
        to_bt = lambda z, n: jnp.swapaxes(z.reshape(ts, bs, n), 0, 1)
        outs[4].append(to_bt(k, K_COLS).reshape(bs, ts, N_KV_HEADS, 2 * HEAD_DIM))
        outs[5].append(to_bt(v, V_COLS).reshape(bs, ts, N_KV_HEADS, V_DIM))
        outs[6].append(h_last)
        xin = jnp.concatenate([state_conv[l], to_bt(xr, D_MODEL)], axis=1)
        outs[7].append(xin[:, -(CONV_W - 1):])

        pos_flat, te, valid, last_tile, n_sorted = _dispatch_plan(jnp.concatenate([recp, recs], axis=0), cnt)
        u_sorted = _dispatch(pos_flat, u2p, u2s, jnp.zeros((n_sorted, D_MODEL), F32) if l == 0 else u_sorted)
        y_sorted = _experts(te, valid, last_tile, jnp.full((1,), l, jnp.int32), u_sorted,
                            w_e_gate, w_e_up, w_e_down)
        xp = _combine(pos_flat, x1p, recp, y_sorted, mods_p[l], tp, 0, t_all)
        xs = _combine(pos_flat, x1s, recs, y_sorted, mods_s[l], ts, t_p, t_all)

    y_sample = jnp.swapaxes(xs.reshape(ts, bs, D_MODEL), 0, 1)
    stack = lambda i: jnp.stack(outs[i])
    return (xp.reshape(bp, tp, D_MODEL), y_sample, stack(0), stack(1), stack(2), stack(3),
            stack(4), stack(5), stack(6), stack(7))
```

```python
import functools
import math

import jax
import jax.numpy as jnp
from jax import lax
from jax.experimental import pallas as pl
from jax.experimental.pallas import tpu as pltpu

F32 = jnp.float32
BF16 = jnp.bfloat16

D_MODEL = 1024
HEAD_DIM = 64
N_HEADS = 8
N_KV_HEADS = 4
GQA_GROUP = 2
V_DIM = 128
ATTN_SCALE = HEAD_DIM ** -0.5
MASK_VALUE = -1e30
NUM_BUCKETS = 32
MAX_EXACT = 16
MAX_DISTANCE = 128
N_RNN_BLOCKS = 16
RNN_BLOCK = 64
CONV_W = 4
LRU_C = 8.0
N_GROUPS = 4
EXPERTS_PER_GROUP = 8
N_EXPERTS = 32
D_EXPERT = 512
NORM_EPS = 1e-6
PAGE_SIZE = 128
Q_COLS = 1024
K_COLS = 512
V_COLS = 512
D_IN = 6144

LANES = 128
SUBLANES = 8
MXU_DIM = 256
VMEM_LIMIT_BYTES = 56 * 1024 * 1024

TOKEN_TILE = 256
MERGE_TILE = 512
ATTN_BLOCK = 512
TIME_TILE = 512
EXPERT_TILE = 512
ROUTER_LANES = 128
ONES_ROWS = 16

_NT = (((1,), (1,)), ((), ()))


def _cparams(sem):
    return pltpu.CompilerParams(dimension_semantics=sem, vmem_limit_bytes=VMEM_LIMIT_BYTES)


def _dot(a, b):
    return jnp.dot(a, b, preferred_element_type=F32)


def _resident(shape):
    n = len(shape)
    return pl.BlockSpec(shape, lambda *_: (0,) * n)


_MODS_COLS = 1536


def _mods_body(c_ref, w_ref, b_ref, o_ref):
    c = c_ref[...]
    s = c * jax.nn.sigmoid(c)
    o_ref[...] = jnp.dot(s, w_ref[...], precision=lax.Precision.HIGHEST,
                         preferred_element_type=F32) + b_ref[...]


def _mods(c_all, w_mod, b_mod):
    rows = c_all.shape[0]
    depth, _, n = w_mod.shape
    return pl.pallas_call(
        _mods_body,
        grid=(depth, n // _MODS_COLS),
        in_specs=[pl.BlockSpec((rows, D_MODEL), lambda l, j: (0, 0)),
                  pl.BlockSpec((None, D_MODEL, _MODS_COLS), lambda l, j: (l, 0, j)),
                  pl.BlockSpec((None, 1, _MODS_COLS), lambda l, j: (l, 0, j))],
        out_specs=pl.BlockSpec((None, rows, _MODS_COLS), lambda l, j: (l, 0, j)),
        out_shape=jax.ShapeDtypeStruct((depth, rows, n), F32),
        compiler_params=_cparams(("arbitrary", "arbitrary")),
        name="mods",
    )(c_all, w_mod, b_mod.reshape(depth, 1, n))


def _rms_rows(x, g):
    ms = jnp.mean(x * x, axis=-1, keepdims=True)
    return x * lax.rsqrt(ms + NORM_EPS) * g


def _chunk_mean_square(z, p):
    parts = []
    for j in range(z.shape[1] // MXU_DIM):
        zz = z[:, j * MXU_DIM:(j + 1) * MXU_DIM]
        parts.append(_dot((zz * zz).astype(BF16), p))
    return jnp.concatenate(parts, axis=1)


def _in_proj_body(q_scale, x_ref, sc_ref, sh_ref, g_ref, w_ref, qg_ref, kg_ref, p_ref,
                  q_ref, k_ref, kb_ref, v_ref, xr_ref, yr_ref, ga_ref, gr_ref, vt_ref=None):
    u = (_rms_rows(x_ref[...], g_ref[...]) * (1.0 + sc_ref[...]) + sh_ref[...]).astype(BF16)
    p = p_ref[...]
    zq = _dot(u, w_ref[:, 0:Q_COLS])
    q = zq * lax.rsqrt(_chunk_mean_square(zq, p) + NORM_EPS) * qg_ref[...]
    q_ref[...] = (q * q_scale).astype(BF16)
    zk = _dot(u, w_ref[:, Q_COLS:Q_COLS + K_COLS])
    k = zk * lax.rsqrt(_chunk_mean_square(zk, p) + NORM_EPS) * kg_ref[...]
    kb_ref[...] = k.astype(BF16)
    c0 = Q_COLS + K_COLS
    v = _dot(u, w_ref[:, c0:c0 + V_COLS])
    if k_ref.ndim == 3:
        for h in range(N_KV_HEADS):
            k_ref[:, h, :] = k[:, h * LANES:(h + 1) * LANES]
            v_ref[:, h, :] = v[:, h * LANES:(h + 1) * LANES]
    else:
        k_ref[...] = k
        v_ref[...] = v
    if vt_ref is not None:
        row = lax.broadcasted_iota(jnp.int32, (ONES_ROWS, v.shape[0]), 0)
        ones_rows = jnp.where(row == 0, 1.0, 0.0).astype(BF16)
        for h in range(N_KV_HEADS):
            vt_ref[h, 0:V_DIM, :] = v[:, h * V_DIM:(h + 1) * V_DIM].T.astype(BF16)
            vt_ref[h, V_DIM:V_DIM + ONES_ROWS, :] = ones_rows
    c0 += V_COLS
    xr_ref[...] = _dot(u, w_ref[:, c0:c0 + D_MODEL])
    c0 += D_MODEL
    yr_ref[...] = _dot(u, w_ref[:, c0:c0 + D_MODEL])
    c0 += D_MODEL
    ga_ref[...] = _dot(u, w_ref[:, c0:c0 + D_MODEL])
    c0 += D_MODEL
    gr_ref[...] = _dot(u, w_ref[:, c0:c0 + D_MODEL])


def _mod_spec(mods, tile, rows_per_group, col_block):
    if mods.ndim == 3:
        tiles_per_group = rows_per_group // tile
        return pl.BlockSpec((None, 1, D_MODEL), lambda i: (i // tiles_per_group, 0, col_block))
    return pl.BlockSpec((tile, D_MODEL), lambda i: (i, col_block))


def _in_proj(x, mods, rows_per_group, q_scale, vt_block, layer, norm_g, w_in_bf, qg_row, kg_row, chunk_p):
    t = x.shape[0]
    tile = min(TOKEN_TILE, t)
    row = lambda n: pl.BlockSpec((tile, n), lambda i: (i, 0))
    outs = [(Q_COLS, BF16), (K_COLS, F32), (K_COLS, BF16), (V_COLS, F32),
            (D_MODEL, F32), (D_MODEL, F32), (D_MODEL, F32), (D_MODEL, F32)]
    out_specs = [row(n) for n, _ in outs]
    out_shape = [jax.ShapeDtypeStruct((t, n), dt) for n, dt in outs]
    if vt_block is not None:
        for n in (1, 3):
            out_specs[n] = pl.BlockSpec((tile, N_KV_HEADS, LANES), lambda i: (i, 0, 0))
            out_shape[n] = jax.ShapeDtypeStruct((t, N_KV_HEADS, LANES), F32)
        tiles_per_seq = rows_per_group // tile
        per_block = vt_block // tile
        vrows = V_DIM + ONES_ROWS
        out_specs.append(pl.BlockSpec(
            (N_KV_HEADS, None, vrows, tile),
            lambda i: (i // tiles_per_seq, (i % tiles_per_seq) // per_block, 0, (i % tiles_per_seq) % per_block)))
        out_shape.append(jax.ShapeDtypeStruct(
            (t // rows_per_group * N_KV_HEADS, rows_per_group // vt_block, vrows, vt_block), BF16))
    return pl.pallas_call(
        functools.partial(_in_proj_body, q_scale),
        grid=(t // tile,),
        in_specs=[row(D_MODEL),
                  _mod_spec(mods, tile, rows_per_group, 1),
                  _mod_spec(mods, tile, rows_per_group, 0),
                  _resident((1, D_MODEL)),
                  pl.BlockSpec((None, D_MODEL, D_IN), lambda i: (layer, 0, 0)),
                  _resident((1, Q_COLS)),
                  _resident((1, K_COLS)),
                  _resident((MXU_DIM, MXU_DIM))],
        out_specs=out_specs,
        out_shape=out_shape,
        compiler_params=_cparams(("arbitrary",)),
        name="in_proj",
    )(x, mods, mods, norm_g, w_in_bf, qg_row, kg_row, chunk_p)


def _bucket_of_distance(n):
    n = jnp.maximum(n, 0)
    nf = jnp.maximum(n, 1).astype(F32)
    large = MAX_EXACT + (jnp.log(nf / MAX_EXACT) / math.log(MAX_DISTANCE / MAX_EXACT)
                         * (NUM_BUCKETS - MAX_EXACT)).astype(jnp.int32)
    return jnp.where(n < MAX_EXACT, n, jnp.minimum(large, NUM_BUCKETS - 1))


def _bias_of_distance(rel_bias, n):
    b = rel_bias.astype(F32)[_bucket_of_distance(n)]
    b = jnp.where((n >= 0)[..., None], b, MASK_VALUE)
    return jnp.moveaxis(b, -1, 0)


def _bias_table(rel_bias, blk, offset):
    j = jnp.arange(2 * blk, dtype=jnp.int32)
    diff = jnp.where(j < blk, j, j - 2 * blk)
    return _bias_of_distance(rel_bias, diff + offset)[:, None, :]


_S_LAM, _S_OUT = 0, 1
_S_FAR = 2
_S_BMAX = _S_FAR + N_HEADS
_S_BRANGE = _S_BMAX + N_HEADS
BOUND_MARGIN = 1.01
SAFE_EXPONENT_SPAN = 100.0


def _attn_prompt_body(scal_ref, q_ref, k_ref, vt_ref, tdiag_ref, tprev_ref, sg_ref,
                      o_ref, qpad_sc, kmax_sc, bq_sc, m_sc, acc_sc, bdiag_ref, bprev_ref):
    h = pl.program_id(1)
    qi = pl.program_id(2)
    blk = q_ref.shape[0]
    ncombo = 2 * GQA_GROUP
    lam = scal_ref[_S_LAM]
    out_scale = scal_ref[_S_OUT]

    @pl.when(qi == 0)
    def _():
        kk = k_ref[...].astype(F32)
        r = lax.broadcasted_iota(jnp.int32, (LANES, LANES), 0) // HEAD_DIM
        c = lax.broadcasted_iota(jnp.int32, (LANES, LANES), 1) // HEAD_DIM
        same = jnp.where(r == c, 1.0, 0.0).astype(BF16)
        ksq = _dot((kk * kk).astype(BF16), same)
        kmax_sc[...] = jnp.sqrt(jnp.max(ksq, axis=0, keepdims=True))
        for table_ref, tile_ref in ((tdiag_ref, bdiag_ref), (tprev_ref, bprev_ref)):
            for g in range(GQA_GROUP):
                rows = jnp.broadcast_to(table_ref[g], (blk, 2 * blk))
                tile_ref[g] = pltpu.roll(rows, 0, 1, stride=1, stride_axis=0)[:, :blk]

    lane = lax.broadcasted_iota(jnp.int32, (blk, LANES), 1)
    ones = jnp.ones((SUBLANES, LANES), BF16)
    span = jnp.zeros((1, 1), F32)
    for g in range(GQA_GROUP):
        q2 = q_ref[:, g * LANES:(g + 1) * LANES]
        for c in range(2):
            idx = 2 * g + c
            qp = jnp.where((lane < HEAD_DIM) == (c == 0), q2, jnp.zeros_like(q2))
            qpad_sc[idx] = qp
            qf = qp.astype(F32)
            qsq = lax.dot_general(ones, (qf * qf).astype(BF16), _NT, preferred_element_type=F32)[0:1]
            kmax = jnp.max(kmax_sc[:, c * HEAD_DIM:(c + 1) * HEAD_DIM], axis=1, keepdims=True)
            bqk = BOUND_MARGIN * jnp.sqrt(qsq) * kmax
            bq_sc[idx] = bqk + scal_ref[_S_BMAX + h * GQA_GROUP + g]
            span = jnp.maximum(span, 2.0 * jnp.max(bqk, axis=1, keepdims=True)
                               + scal_ref[_S_BRANGE + h * GQA_GROUP + g])
    acc_sc[...] = jnp.zeros(acc_sc.shape, F32)
    far_bias = lambda g: scal_ref[_S_FAR + h * GQA_GROUP + g]
    n_far = jnp.maximum(qi - 1, 0)

    def k_block(kj):
        return k_ref[pl.ds(pl.multiple_of(kj * blk, blk), blk), :]

    def fast_chain(steps):
        work = [(s, idx) for s in range(len(steps)) for idx in range(ncombo)]
        k_blks = [k_block(kj) for kj, _, _ in steps]
        vts = [vt_ref[kj] for kj, _, _ in steps]
        qk = lambda w: lax.dot_general(k_blks[w[0]], qpad_sc[w[1]], _NT, preferred_element_type=F32)
        st_next = qk(work[0])
        for n, (s, idx) in enumerate(work):
            st = st_next
            if n + 1 < len(work):
                st_next = qk(work[n + 1])
            _, shift_of_group, tile_of_group = steps[s]
            g = idx // 2
            if tile_of_group is not None:
                st = st + tile_of_group(g)
            bound = bq_sc[idx] if shift_of_group is None else bq_sc[idx] - shift_of_group(g)
            acc_sc[idx] += _dot(vts[s], jnp.exp2(st - bound).astype(BF16))

    def exact_step(kj, shift_of_group, tile_of_group):
        k_blk = k_block(kj)
        vt = vt_ref[kj]
        for idx in range(ncombo):
            g = idx // 2
            st = lax.dot_general(k_blk, qpad_sc[idx], _NT, preferred_element_type=F32)
            if tile_of_group is not None:
                st = st + tile_of_group(g)
            shift = 0.0 if shift_of_group is None else shift_of_group(g)
            m_old = m_sc[idx]
            m_new = jnp.maximum(m_old, jnp.max(st, axis=0, keepdims=True) + shift)
            p = jnp.exp2(st - (m_new - shift)).astype(BF16)
            acc_sc[idx] = jnp.exp2(m_old - m_new) * acc_sc[idx] + _dot(vt, p)
            m_sc[idx] = m_new

    prev_tile = lambda g: bprev_ref[g]
    diag_tile = lambda g: bdiag_ref[g]

    def run_fast():
        def far_pair(j, carry):
            fast_chain([(2 * j, far_bias, None), (2 * j + 1, far_bias, None)])
            return carry
        lax.fori_loop(0, n_far // 2, far_pair, 0)

        @pl.when(n_far % 2 == 1)
        def _():
            fast_chain([(n_far - 1, far_bias, None)])

        @pl.when(qi >= 1)
        def _():
            fast_chain([(qi - 1, None, prev_tile), (qi, None, diag_tile)])

        @pl.when(qi == 0)
        def _():
            fast_chain([(qi, None, diag_tile)])

    def run_exact():
        m_sc[...] = jnp.full(m_sc.shape, -jnp.inf, F32)

        def far_step(kj, carry):
            exact_step(kj, far_bias, None)
            return carry
        lax.fori_loop(0, n_far, far_step, 0)

        @pl.when(qi >= 1)
        def _():
            exact_step(qi - 1, None, prev_tile)
        exact_step(qi, None, diag_tile)

    lax.cond(span[0, 0] <= SAFE_EXPONENT_SPAN, run_fast, run_exact)

    for g in range(GQA_GROUP):
        a0 = acc_sc[2 * g]
        a1 = acc_sc[2 * g + 1]
        o = a0[:V_DIM] * (1.0 / a0[V_DIM:V_DIM + 1]) - a1[:V_DIM] * (lam / a1[V_DIM:V_DIM + 1])
        o = o * lax.rsqrt(jnp.mean(o * o, axis=0, keepdims=True) + NORM_EPS)
        o = o * sg_ref[...] * out_scale
        o_ref[:, g * V_DIM:(g + 1) * V_DIM] = o.T.astype(BF16)


def _attn_prompt(scal, q_bf, k_bf, vt_ext, bias_diag, bias_prev, sg_col, batch, seq):
    blk = ATTN_BLOCK
    nblk = seq // blk
    vrows = V_DIM + ONES_ROWS
    ncombo = 2 * GQA_GROUP
    grid_spec = pltpu.PrefetchScalarGridSpec(
        num_scalar_prefetch=1,
        grid=(batch, N_KV_HEADS, nblk),
        in_specs=[pl.BlockSpec((blk, GQA_GROUP * 2 * HEAD_DIM), lambda b, h, i, s: (b * nblk + i, h)),
                  pl.BlockSpec((seq, 2 * HEAD_DIM), lambda b, h, i, s: (b, h)),
                  pl.BlockSpec((None, nblk, vrows, blk), lambda b, h, i, s: (b * N_KV_HEADS + h, 0, 0, 0)),
                  pl.BlockSpec((GQA_GROUP, 1, 2 * blk), lambda b, h, i, s: (h, 0, 0)),
                  pl.BlockSpec((GQA_GROUP, 1, 2 * blk), lambda b, h, i, s: (h, 0, 0)),
                  pl.BlockSpec((V_DIM, 1), lambda b, h, i, s: (0, 0))],
        out_specs=pl.BlockSpec((blk, GQA_GROUP * V_DIM), lambda b, h, i, s: (b * nblk + i, h)),
        scratch_shapes=[pltpu.VMEM((ncombo, blk, LANES), BF16),
                        pltpu.VMEM((1, LANES), F32),
                        pltpu.VMEM((ncombo, 1, blk), F32),
                        pltpu.VMEM((ncombo, 1, blk), F32),
                        pltpu.VMEM((ncombo, vrows, blk), F32),
                        pltpu.VMEM((GQA_GROUP, blk, blk), F32),
                        pltpu.VMEM((GQA_GROUP, blk, blk), F32)],
    )
    return pl.pallas_call(
        _attn_prompt_body,
        grid_spec=grid_spec,
        out_shape=jax.ShapeDtypeStruct((batch * seq, N_HEADS * V_DIM), BF16),
        compiler_params=_cparams(("arbitrary", "arbitrary", "arbitrary")),
        name="attn_prompt",
    )(scal, q_bf, k_bf, vt_ext, bias_diag, bias_prev, sg_col)


def _attn_sample_body(n_pages, pt_ref, layer_ref, scal_ref, q_ref, knew_ref, vnew_ref, bias_ref, sg_ref, *refs):
    k_pages = refs[:n_pages]
    v_pages = refs[n_pages:2 * n_pages]
    o_ref, kb_sc, vb_sc = refs[2 * n_pages:]
    page_rows = k_pages[0].shape[2]
    past = n_pages * page_rows
    n_new = knew_ref.shape[1]
    lam = scal_ref[0]
    out_scale = scal_ref[1]
    half = q_ref.shape[1] // 2

    @pl.when(pl.program_id(0) == 0)
    def _():
        tail = kb_sc.shape[0] - past - n_new
        kb_sc[past + n_new:, :] = jnp.zeros((tail, LANES), BF16)
        vb_sc[past + n_new:, :] = jnp.zeros((tail, LANES), BF16)

    for j in range(n_pages):
        kb_sc[j * page_rows:(j + 1) * page_rows, :] = k_pages[j][0, 0].astype(BF16)
        vb_sc[j * page_rows:(j + 1) * page_rows, :] = v_pages[j][0, 0].astype(BF16)
    kb_sc[past:past + n_new, :] = knew_ref[0]
    vb_sc[past:past + n_new, :] = vnew_ref[0]
    s = lax.dot_general(q_ref[0], kb_sc[...], _NT, preferred_element_type=F32) + bias_ref[...]
    p = jnp.exp(s - jnp.max(s, axis=-1, keepdims=True))
    p = p / jnp.sum(p, axis=-1, keepdims=True)
    a = p[:half] - lam * p[half:]
    o = _dot(a.astype(BF16), vb_sc[...])
    o = o * lax.rsqrt(jnp.mean(o * o, axis=-1, keepdims=True) + NORM_EPS)
    o_ref[0] = o * sg_ref[...] * out_scale


def _attn_sample(page_table_flat, layer_idx, scal, q_rows, knew, vnew, bias_s, sg_row, cache_k, cache_v, n_pages):
    nb, rows, _ = q_rows.shape
    n_new = knew.shape[1]
    page_rows = cache_k.shape[2]
    keys = bias_s.shape[1]

    def page_spec(j):
        return pl.BlockSpec((1, 1, page_rows, LANES), lambda b, pt, l: (l[0], pt[b * n_pages + j], 0, 0))

    grid_spec = pltpu.PrefetchScalarGridSpec(
        num_scalar_prefetch=2,
        grid=(nb,),
        in_specs=[pl.BlockSpec(memory_space=pltpu.SMEM),
                  pl.BlockSpec((1, rows, LANES), lambda b, pt, l: (b, 0, 0)),
                  pl.BlockSpec((1, n_new, LANES), lambda b, pt, l: (b, 0, 0)),
                  pl.BlockSpec((1, n_new, LANES), lambda b, pt, l: (b, 0, 0)),
                  pl.BlockSpec((rows, keys), lambda b, pt, l: (0, 0)),
                  pl.BlockSpec((1, V_DIM), lambda b, pt, l: (0, 0))]
                 + [page_spec(j) for j in range(n_pages)] * 2,
        out_specs=pl.BlockSpec((1, rows // 2, V_DIM), lambda b, pt, l: (b, 0, 0)),
        scratch_shapes=[pltpu.VMEM((keys, LANES), BF16),
                        pltpu.VMEM((keys, LANES), BF16)],
    )
    return pl.pallas_call(
        functools.partial(_attn_sample_body, n_pages),
        grid_spec=grid_spec,
        out_shape=jax.ShapeDtypeStruct((nb, rows // 2, V_DIM), F32),
        compiler_params=_cparams(("arbitrary",)),
        name="attn_sample",
    )(page_table_flat, layer_idx, scal, q_rows, knew, vnew, bias_s, sg_row,
      *([cache_k] * n_pages), *([cache_v] * n_pages))


def _lru_coeffs(xc, wa_ref, wi_ref, ba_ref, bi_ref, sp_ref):
    xb = xc.astype(BF16)
    ra, ri = [], []
    for j in range(xc.shape[1] // MXU_DIM):
        cols = slice(j * MXU_DIM, (j + 1) * MXU_DIM)
        ra.append(_dot(xb[:, cols], wa_ref[j]))
        ri.append(_dot(xb[:, cols], wi_ref[j]))
    r = jax.nn.sigmoid(jnp.concatenate(ra, axis=1) + ba_ref[...])
    i = jax.nn.sigmoid(jnp.concatenate(ri, axis=1) + bi_ref[...])
    log_a = -LRU_C * r * sp_ref[...]
    a = jnp.exp(log_a)
    mult = jnp.sqrt(-jnp.tanh(log_a) * (a * a + 1.0))
    return a, mult, i


def _rglru_prompt_body(xr_ref, yr_ref, cw_ref, cb_ref, wa_ref, wi_ref, ba_ref, bi_ref, sp_ref,
                       rnn_ref, hl_ref, xe_sc, a_sc, b_sc, h_sc):
    i = pl.program_id(1)
    tt = xr_ref.shape[0]

    @pl.when(i == 0)
    def _():
        xe_sc[0:SUBLANES, :] = jnp.zeros((SUBLANES, D_MODEL), F32)
        h_sc[...] = jnp.zeros(h_sc.shape, F32)

    xe_sc[SUBLANES:SUBLANES + tt, :] = xr_ref[...]
    xc = cb_ref[...]
    for j in range(CONV_W):
        off = SUBLANES - (CONV_W - 1) + j
        xc = xc + xe_sc[off:off + tt, :] * cw_ref[j:j + 1, :]
    a, mult, gate_i = _lru_coeffs(xc, wa_ref, wi_ref, ba_ref, bi_ref, sp_ref)
    row = lax.broadcasted_iota(jnp.int32, (tt, 1), 0)
    mult = jnp.where((row == 0) & (i == 0), 1.0, mult)
    a_sc[...] = a
    b_sc[...] = mult * gate_i * xc

    def step(t, h):
        h = a_sc[pl.ds(t, 1), :] * h + b_sc[pl.ds(t, 1), :]
        b_sc[pl.ds(t, 1), :] = h
        return h

    h = lax.fori_loop(0, tt, step, h_sc[...], unroll=8)
    h_sc[...] = h
    hl_ref[...] = h
    rnn_ref[...] = (b_sc[...] * jax.nn.gelu(yr_ref[...])).astype(BF16)
    xe_sc[0:SUBLANES, :] = xe_sc[tt:tt + SUBLANES, :]


def _rglru_prompt(xr, yr, cw, cb, wa_bd, wi_bd, ba, bi, sp, batch, seq):
    tt = min(TIME_TILE, seq)
    nt = seq // tt
    row = pl.BlockSpec((tt, D_MODEL), lambda b, i: (b * nt + i, 0))
    nbd = D_MODEL // MXU_DIM
    return pl.pallas_call(
        _rglru_prompt_body,
        grid=(batch, nt),
        in_specs=[row, row,
                  pl.BlockSpec((CONV_W, D_MODEL), lambda b, i: (0, 0)),
                  pl.BlockSpec((1, D_MODEL), lambda b, i: (0, 0)),
                  pl.BlockSpec((nbd, MXU_DIM, MXU_DIM), lambda b, i: (0, 0, 0)),
                  pl.BlockSpec((nbd, MXU_DIM, MXU_DIM), lambda b, i: (0, 0, 0)),
                  pl.BlockSpec((1, D_MODEL), lambda b, i: (0, 0)),
                  pl.BlockSpec((1, D_MODEL), lambda b, i: (0, 0)),
                  pl.BlockSpec((1, D_MODEL), lambda b, i: (0, 0))],
        out_specs=[row, pl.BlockSpec((None, 1, D_MODEL), lambda b, i: (b, 0, 0))],
        out_shape=[jax.ShapeDtypeStruct((batch * seq, D_MODEL), BF16),
                   jax.ShapeDtypeStruct((batch, 1, D_MODEL), F32)],
        scratch_shapes=[pltpu.VMEM((tt + SUBLANES, D_MODEL), F32),
                        pltpu.VMEM((tt, D_MODEL), F32),
                        pltpu.VMEM((tt, D_MODEL), F32),
                        pltpu.VMEM((1, D_MODEL), F32)],
        compiler_params=_cparams(("arbitrary", "arbitrary")),
        name="rglru_prompt",
    )(xr, yr, cw, cb, wa_bd, wi_bd, ba, bi, sp)


def _rglru_sample_body(n_steps, first_is_start, xr_ref, yr_ref, buf_ref, h0_ref, cw_ref, cb_ref, wa_ref, wi_ref, ba_ref, bi_ref,
                       sp_ref, rnn_ref, hl_ref):
    nb = h0_ref.shape[0]
    taps = [buf_ref[j] for j in range(CONV_W - 1)] + [xr_ref[t * nb:(t + 1) * nb, :] for t in range(n_steps)]
    h = h0_ref[...]
    for t in range(n_steps):
        xc = cb_ref[...]
        for j in range(CONV_W):
            xc = xc + taps[t + j] * cw_ref[j:j + 1, :]
        a, mult, gate_i = _lru_coeffs(xc, wa_ref, wi_ref, ba_ref, bi_ref, sp_ref)
        if first_is_start and t == 0:
            mult = jnp.ones_like(mult)
        h = a * h + mult * gate_i * xc
        rnn_ref[t * nb:(t + 1) * nb, :] = (h * jax.nn.gelu(yr_ref[t * nb:(t + 1) * nb, :])).astype(BF16)
    hl_ref[...] = h


def _rglru_sample(xr, yr, buf, h0, cw, cb, wa_bd, wi_bd, ba, bi, sp, n_steps, first_is_start):
    t = xr.shape[0]
    return pl.pallas_call(
        functools.partial(_rglru_sample_body, n_steps, first_is_start),
        out_shape=[jax.ShapeDtypeStruct((t, D_MODEL), BF16),
                   jax.ShapeDtypeStruct(h0.shape, F32)],
        compiler_params=pltpu.CompilerParams(vmem_limit_bytes=VMEM_LIMIT_BYTES),
        name="rglru_sample",
    )(xr, yr, buf, h0, cw, cb, wa_bd, wi_bd, ba, bi, sp)


def _route(logits):
    lane = lax.broadcasted_iota(jnp.int32, logits.shape, 1).astype(F32)
    none = float(ROUTER_LANES)
    is_group = lane < N_GROUPS
    lg = jnp.where(is_group, logits, -jnp.inf)
    eg = jnp.exp(lg - jnp.max(lg, axis=-1, keepdims=True))
    gp = eg / jnp.sum(eg, axis=-1, keepdims=True)
    gw = jnp.max(gp, axis=-1, keepdims=True)
    gi = jnp.min(jnp.where(is_group & (gp == gw), lane, none), axis=-1, keepdims=True)
    first = N_GROUPS + gi * EXPERTS_PER_GROUP
    in_group = (lane >= first) & (lane < first + EXPERTS_PER_GROUP)
    le = jnp.where(in_group, logits, -jnp.inf)
    ee = jnp.exp(le - jnp.max(le, axis=-1, keepdims=True))
    pe = ee / jnp.sum(ee, axis=-1, keepdims=True)
    p1 = jnp.max(pe, axis=-1, keepdims=True)
    i1 = jnp.min(jnp.where(in_group & (pe == p1), lane, none), axis=-1, keepdims=True)
    rest = in_group & (lane != i1)
    pr = jnp.where(rest, pe, -1.0)
    p2 = jnp.max(pr, axis=-1, keepdims=True)
    i2 = jnp.min(jnp.where(rest & (pr == p2), lane, none), axis=-1, keepdims=True)
    den = p1 + p2
    w1 = gw * (p1 / den)
    w2 = gw * (p2 / den)
    rec = jnp.where(lane == 0.0, i1 - N_GROUPS, 0.0)
    rec = jnp.where(lane == 1.0, i2 - N_GROUPS, rec)
    rec = jnp.where(lane == 2.0, w1, rec)
    rec = jnp.where(lane == 3.0, w2, rec)
    return rec, jnp.where(lane == i1, 1.0, 0.0), jnp.where(lane == i2, 1.0, 0.0)


def _merge_body(x_ref, attn_ref, rnn_ref, ga_ref, gr_ref, g1_ref, sc2_ref, sh2_ref, n2g_ref,
                wpa_ref, wpr_ref, wo_ref, wrh_ref, wrl_ref, br_ref, ltri_ref, cnt_in_ref,
                x1_ref, u2_ref, rec_ref, cnt_ref, cnt_sc):
    tile = x_ref.shape[0]
    n_parts = 2 if tile % (2 * SUBLANES) == 0 else 1
    parts = [slice(p * tile // n_parts, (p + 1) * tile // n_parts) for p in range(n_parts)]
    per_row = lambda ref, rows: ref[rows, :] if ref.shape[0] == tile else ref[...]
    def proj(rows):
        return _dot(attn_ref[rows, :], wpa_ref[...]), _dot(rnn_ref[rows, :], wpr_ref[...])

    def mixed(rows, a, r):
        m = jax.nn.sigmoid(ga_ref[rows, :]) * a + jax.nn.sigmoid(gr_ref[rows, :]) * r
        return _dot(m.astype(BF16), wo_ref[...])

    def residual(rows, mx):
        x1 = x_ref[rows, :] + per_row(g1_ref, rows) * mx
        x1_ref[rows, :] = x1
        u2 = _rms_rows(x1, n2g_ref[...]) * (1.0 + per_row(sc2_ref, rows)) + per_row(sh2_ref, rows)
        u2_ref[rows, :] = u2
        u_hi = u2.astype(BF16)
        u_lo = (u2 - u_hi.astype(F32)).astype(BF16)
        return (_dot(u_hi, wrh_ref[...]) + _dot(u_lo, wrh_ref[...]) + _dot(u_hi, wrl_ref[...])) + br_ref[...]

    projs = [proj(parts[0])]
    mixes, logits = [], []
    for p in range(n_parts):
        if p + 1 < n_parts:
            projs.append(proj(parts[p + 1]))
        mixes.append(mixed(parts[p], *projs[p]))
        if p >= 1:
            logits.append(residual(parts[p - 1], mixes[p - 1]))
    logits.append(residual(parts[-1], mixes[-1]))
    rec, hot1, hot2 = _route(jnp.concatenate(logits, axis=0))

    @pl.when(pl.program_id(0) == 0)
    def _():
        cnt_sc[...] = cnt_in_ref[...]
    cnt = cnt_sc[...]
    tot1 = jnp.sum(hot1, axis=0, keepdims=True)
    tot2 = jnp.sum(hot2, axis=0, keepdims=True)
    before1 = _dot(ltri_ref[...], hot1.astype(BF16)) + cnt
    before2 = _dot(ltri_ref[...], hot2.astype(BF16)) + (cnt + tot1)
    rank1 = jnp.sum(hot1 * before1, axis=-1, keepdims=True)
    rank2 = jnp.sum(hot2 * before2, axis=-1, keepdims=True)
    cnt_sc[...] = cnt + tot1 + tot2
    cnt_ref[...] = cnt_sc[...]
    lane = lax.broadcasted_iota(jnp.int32, rec.shape, 1)
    rec = jnp.where(lane == 4, rank1, rec)
    rec_ref[...] = jnp.where(lane == 5, rank2, rec)


def _merge(x, attn, rnn, ga, gr, mods, rows_per_group, cnt_in, n2g, wpa, wpr, wo, wr_hi, wr_lo, br):
    t = x.shape[0]
    tile = min(MERGE_TILE, t)
    assert t % tile == 0 and (mods.ndim == 2 or rows_per_group % tile == 0)
    row = lambda n: pl.BlockSpec((tile, n), lambda i: (i, 0))
    sq = _resident((D_MODEL, D_MODEL))
    r = jnp.arange(tile)
    ltri = (r[None, :] < r[:, None]).astype(BF16)
    return pl.pallas_call(
        _merge_body,
        grid=(t // tile,),
        in_specs=[row(D_MODEL), row(D_MODEL), row(D_MODEL), row(D_MODEL), row(D_MODEL),
                  _mod_spec(mods, tile, rows_per_group, 2),
                  _mod_spec(mods, tile, rows_per_group, 4),
                  _mod_spec(mods, tile, rows_per_group, 3),
                  _resident((1, D_MODEL)), sq, sq, sq,
                  _resident((D_MODEL, ROUTER_LANES)), _resident((D_MODEL, ROUTER_LANES)),
                  _resident((1, ROUTER_LANES)), _resident((tile, tile)), _resident((1, ROUTER_LANES))],
        out_specs=[row(D_MODEL), row(D_MODEL), row(ROUTER_LANES), _resident((1, ROUTER_LANES))],
        out_shape=[jax.ShapeDtypeStruct((t, D_MODEL), F32),
                   jax.ShapeDtypeStruct((t, D_MODEL), F32),
                   jax.ShapeDtypeStruct((t, ROUTER_LANES), F32),
                   jax.ShapeDtypeStruct((1, ROUTER_LANES), F32)],
        scratch_shapes=[pltpu.VMEM((1, ROUTER_LANES), F32)],
        compiler_params=_cparams(("arbitrary",)),
        name="merge",
    )(x, attn, rnn, ga, gr, mods, mods, mods, n2g, wpa, wpr, wo, wr_hi, wr_lo, br, ltri, cnt_in)


ROW_UNROLL = 8


def _for_rows(n_rows, fn):
    def body(j, carry):
        for s in range(ROW_UNROLL):
            fn(j * ROW_UNROLL + s)
        return carry
    lax.fori_loop(0, n_rows // ROW_UNROLL, body, 0)


def _dispatch_body(t_all, n_first, pos_ref, ua_ref, ub_ref, init_hbm, us_hbm, sem):
    del init_hbm
    i = pl.program_id(0)

    def scatter_tile(u_ref, base):
        tile = u_ref.shape[0]

        def issue(r):
            for slot in range(2):
                pltpu.make_async_copy(u_ref.at[pl.ds(r, 1)],
                                      us_hbm.at[pl.ds(pos_ref[slot * t_all + base + r], 1)], sem.at[0]).start()
        _for_rows(tile, issue)
        for _ in range(2):
            pltpu.make_async_copy(u_ref, us_hbm.at[pl.ds(0, tile)], sem.at[0]).wait()

    @pl.when(i < n_first)
    def _():
        scatter_tile(ua_ref, i * ua_ref.shape[0])

    @pl.when(i >= n_first)
    def _():
        scatter_tile(ub_ref, n_first * ua_ref.shape[0] + (i - n_first) * ub_ref.shape[0])


def _dispatch(pos_flat, u_first, u_second, init):
    n_sorted_rows = init.shape[0]
    t_all = u_first.shape[0] + u_second.shape[0]
    tile_a = min(TOKEN_TILE, u_first.shape[0])
    tile_b = min(TOKEN_TILE, u_second.shape[0])
    n_a = u_first.shape[0] // tile_a
    n_b = u_second.shape[0] // tile_b
    assert n_a * tile_a == u_first.shape[0] and n_b * tile_b == u_second.shape[0]
    assert tile_a % ROW_UNROLL == 0 and tile_b % ROW_UNROLL == 0
    grid_spec = pltpu.PrefetchScalarGridSpec(
        num_scalar_prefetch=1,
        grid=(n_a + n_b,),
        in_specs=[pl.BlockSpec((tile_a, D_MODEL), lambda i, pos: (jnp.minimum(i, n_a - 1), 0)),
                  pl.BlockSpec((tile_b, D_MODEL), lambda i, pos: (jnp.maximum(i - n_a, 0), 0)),
                  pl.BlockSpec(memory_space=pl.ANY)],
        out_specs=pl.BlockSpec(memory_space=pl.ANY),
        scratch_shapes=[pltpu.SemaphoreType.DMA((1,))],
    )
    return pl.pallas_call(
        functools.partial(_dispatch_body, t_all, n_a),
        grid_spec=grid_spec,
        out_shape=jax.ShapeDtypeStruct((n_sorted_rows, D_MODEL), F32),
        input_output_aliases={3: 0},
        compiler_params=_cparams(("arbitrary",)),
        name="dispatch",
    )(pos_flat, u_first, u_second, init)


def _experts_body(te_ref, nv_ref, last_ref, layer_ref, u_ref, wg_ref, wu_ref, wd_ref, y_ref, wg_sc, wu_sc, wd_sc):
    del last_ref, layer_ref
    i = pl.program_id(0)

    @pl.when((i == 0) | (te_ref[i] != te_ref[jnp.maximum(i - 1, 0)]))
    def _():
        wg_sc[...] = wg_ref[...].astype(BF16)
        wu_sc[...] = wu_ref[...].astype(BF16)
        wd_sc[...] = wd_ref[...].astype(BF16)

    @pl.when(nv_ref[i] > 0)
    def _():
        u = u_ref[...].astype(BF16)
        hg = _dot(u, wg_sc[...])
        hid = (hg * jax.nn.sigmoid(hg)) * _dot(u, wu_sc[...])
        y_ref[...] = _dot(hid.astype(BF16), wd_sc[...])

    @pl.when(nv_ref[i] == 0)
    def _():
        y_ref[...] = jnp.zeros(y_ref.shape, F32)


def _experts(tile_expert, tile_valid, last_tile, layer_idx, u_sorted, w_gate, w_up, w_down):
    tile = EXPERT_TILE
    n_tiles = tile_expert.shape[0]
    wspec = lambda a, b: pl.BlockSpec((None, None, a, b), lambda i, te, nv, last, lyr: (lyr[0], te[i], 0, 0))
    grid_spec = pltpu.PrefetchScalarGridSpec(
        num_scalar_prefetch=4,
        grid=(n_tiles,),
        in_specs=[pl.BlockSpec((tile, D_MODEL), lambda i, te, nv, last, lyr: (jnp.minimum(i, last[0]), 0)),
                  wspec(D_MODEL, D_EXPERT), wspec(D_MODEL, D_EXPERT), wspec(D_EXPERT, D_MODEL)],
        out_specs=pl.BlockSpec((tile, D_MODEL), lambda i, te, nv, last, lyr: (i, 0)),
        scratch_shapes=[pltpu.VMEM((D_MODEL, D_EXPERT), BF16),
                        pltpu.VMEM((D_MODEL, D_EXPERT), BF16),
                        pltpu.VMEM((D_EXPERT, D_MODEL), BF16)],
    )
    return pl.pallas_call(
        _experts_body,
        grid_spec=grid_spec,
        out_shape=jax.ShapeDtypeStruct(u_sorted.shape, F32),
        compiler_params=_cparams(("arbitrary",)),
        name="experts",
    )(tile_expert, tile_valid, last_tile, layer_idx, u_sorted, w_gate, w_up, w_down)


def _dispatch_plan(rec_all, counts_row):
    t_all = rec_all.shape[0]
    tile = EXPERT_TILE
    n_tiles = -(-2 * t_all // tile) + N_EXPERTS
    counts = counts_row[0, N_GROUPS:N_GROUPS + N_EXPERTS].astype(jnp.int32)
    tiles_per_e = (counts + tile - 1) // tile
    tile_end = jnp.cumsum(tiles_per_e)
    tile_start = tile_end - tiles_per_e
    experts = jnp.arange(N_EXPERTS, dtype=jnp.int32)
    e = rec_all[:, 0:2].astype(jnp.int32)
    rank = rec_all[:, 4:6].astype(jnp.int32)
    row_start = jnp.sum(jnp.where(e[:, :, None] == experts, tile_start * tile, 0), axis=-1)
    pos_flat = (row_start + rank).T.reshape(-1)
    n_used = tile_end[-1]
    tile_ids = jnp.arange(n_tiles, dtype=jnp.int32)
    clamped = jnp.minimum(tile_ids, n_used - 1)
    te = jnp.sum((clamped[:, None] >= tile_end[None, :]).astype(jnp.int32), axis=-1)
    te = jnp.minimum(te, N_EXPERTS - 1)
    first = jnp.sum(jnp.where(te[:, None] == experts, tile_start, 0), axis=-1)
    cnt_t = jnp.sum(jnp.where(te[:, None] == experts, counts, 0), axis=-1)
    valid = jnp.clip(cnt_t - (tile_ids - first) * tile, 0, tile)
    valid = jnp.where(tile_ids < n_used, valid, 0).astype(jnp.int32)
    return pos_flat, te, valid, (n_used - 1).reshape(1).astype(jnp.int32), n_tiles * tile


def _combine_body(t_all, row_offset, pos_ref, x_ref, rec_ref, g2_ref, ys_hbm, o_ref, ybuf, sem):
    i = pl.program_id(0)
    n = pl.num_programs(0)
    tile = x_ref.shape[0]
    slot = i % 2

    def start_gather(tile_idx, buf):
        base = row_offset + tile_idx * tile

        def issue(r):
            for k in range(2):
                pltpu.make_async_copy(ys_hbm.at[pl.ds(pos_ref[k * t_all + base + r], 1)],
                                      ybuf.at[buf, k, pl.ds(r, 1)], sem.at[buf]).start()
        _for_rows(tile, issue)

    @pl.when(i == 0)
    def _():
        start_gather(0, 0)

    @pl.when(i + 1 < n)
    def _():
        start_gather(i + 1, 1 - slot)

    for k in range(2):
        pltpu.make_async_copy(ys_hbm.at[pl.ds(0, tile)], ybuf.at[slot, k], sem.at[slot]).wait()
    w1 = rec_ref[:, 2:3]
    w2 = rec_ref[:, 3:4]
    o_ref[...] = x_ref[...] + g2_ref[...] * (w1 * ybuf[slot, 0] + w2 * ybuf[slot, 1])


def _combine(pos_flat, x1, rec, y_sorted, mods, rows_per_group, row_offset, t_all):
    t = x1.shape[0]
    tile = min(TOKEN_TILE, t)
    assert tile % ROW_UNROLL == 0
    if mods.ndim == 3:
        tiles_per_group = rows_per_group // tile
        g2_spec = pl.BlockSpec((None, 1, D_MODEL), lambda i, pos: (i // tiles_per_group, 0, 5))
    else:
        g2_spec = pl.BlockSpec((tile, D_MODEL), lambda i, pos: (i, 5))
    grid_spec = pltpu.PrefetchScalarGridSpec(
        num_scalar_prefetch=1,
        grid=(t // tile,),
        in_specs=[pl.BlockSpec((tile, D_MODEL), lambda i, pos: (i, 0)),
                  pl.BlockSpec((tile, ROUTER_LANES), lambda i, pos: (i, 0)),
                  g2_spec,
                  pl.BlockSpec(memory_space=pl.ANY)],
        out_specs=pl.BlockSpec((tile, D_MODEL), lambda i, pos: (i, 0)),
        scratch_shapes=[pltpu.VMEM((2, 2, tile, D_MODEL), F32),
                        pltpu.SemaphoreType.DMA((2,))],
    )
    return pl.pallas_call(
        functools.partial(_combine_body, t_all, row_offset),
        grid_spec=grid_spec,
        out_shape=jax.ShapeDtypeStruct((t, D_MODEL), F32),
        compiler_params=_cparams(("arbitrary",)),
        name="combine",
    )(pos_flat, x1, rec, mods, y_sorted)


def _block_diag(w):
    per = MXU_DIM // RNN_BLOCK
    w4 = w.reshape(N_RNN_BLOCKS // per, per, RNN_BLOCK, RNN_BLOCK)
    eye = jnp.eye(per, dtype=w.dtype)
    return jnp.einsum("jmcd,mn->jmcnd", w4, eye).reshape(N_RNN_BLOCKS // per, MXU_DIM, MXU_DIM)


def kernel(x_prompt, x_sample, cache_k, cache_v, state_rnn, state_conv, page_table, c_prompt, c_sample, rel_bias, w_mod, b_mod, norm1_g, norm2_g, w_in, q_norm_g, k_norm_g, lam_q1, lam_k1, lam_q2, lam_k2, subln_g, conv_w, conv_b, lru_wa, lru_ba, lru_wi, lru_bi, lru_lambda, w_pa, w_pr, w_o, w_rg, b_rg, w_re, b_re, w_e_gate, w_e_up, w_e_down):
    bp, tp, _ = x_prompt.shape
    bs, ts, _ = x_sample.shape
    depth = w_in.shape[0]
    n_pool = cache_k.shape[1]
    n_pages = page_table.shape[1]
    past_len = n_pages * PAGE_SIZE
    t_p = bp * tp
    t_s = bs * ts
    t_all = t_p + t_s
    blk = ATTN_BLOCK
    nblk = tp // blk
    assert tp % blk == 0 and tp % TIME_TILE == 0 and t_p % TOKEN_TILE == 0
    assert t_s % SUBLANES == 0 and t_p % min(TOKEN_TILE, t_s) == 0
    log2e = math.log2(math.e)

    rows_c = -(-(bp + bs) // SUBLANES) * SUBLANES
    c_all = jnp.zeros((rows_c, D_MODEL), F32).at[:bp].set(c_prompt).at[bp:bp + bs].set(c_sample)
    mods = _mods(c_all, w_mod, b_mod)
    mods_p = mods[:, :bp].reshape(depth, bp, 1, 6 * D_MODEL)
    mods_s = jnp.tile(mods[:, bp:bp + bs], (1, ts, 1))

    w_in_bf = w_in.astype(BF16)
    w_pa_bf, w_pr_bf, w_o_bf = w_pa.astype(BF16), w_pr.astype(BF16), w_o.astype(BF16)
    qg_row = jnp.tile(q_norm_g, (1, Q_COLS // HEAD_DIM)).reshape(depth, 1, Q_COLS)
    kg_row = jnp.tile(k_norm_g, (1, K_COLS // HEAD_DIM)).reshape(depth, 1, K_COLS)
    seg = jnp.arange(MXU_DIM) // HEAD_DIM
    chunk_p = ((seg[:, None] == seg[None, :]).astype(F32) / HEAD_DIM).astype(BF16)
    wr = jnp.concatenate([w_rg, w_re], axis=-1)
    wr = jnp.pad(wr, ((0, 0), (0, 0), (0, ROUTER_LANES - wr.shape[-1])))
    wr_hi = wr.astype(BF16)
    wr_lo = (wr - wr_hi.astype(F32)).astype(BF16)
    br = jnp.pad(jnp.concatenate([b_rg, b_re], axis=-1), ((0, 0), (0, ROUTER_LANES - N_GROUPS - N_EXPERTS)))
    br = br.reshape(depth, 1, ROUTER_LANES)
    wa_bd = jax.vmap(_block_diag)(lru_wa).astype(BF16)
    wi_bd = jax.vmap(_block_diag)(lru_wi).astype(BF16)
    ba = lru_ba.reshape(depth, 1, D_MODEL)
    bi = lru_bi.reshape(depth, 1, D_MODEL)
    sp = jax.nn.softplus(-lru_lambda.astype(F32)).reshape(depth, 1, D_MODEL)
    cb = conv_b.reshape(depth, 1, D_MODEL)
    lam_f = lambda a, b: jnp.exp(jnp.sum(a.astype(F32) * b.astype(F32), axis=-1))
    lam_base = lam_f(lam_q1, lam_k1) - lam_f(lam_q2, lam_k2)

    bias_diag = _bias_table(rel_bias, blk, 0) * log2e
    bias_prev = _bias_table(rel_bias, blk, blk) * log2e
    rb = rel_bias.astype(F32) * log2e
    bias_scal = jnp.concatenate([rb[NUM_BUCKETS - 1],
                                 jnp.max(rb, axis=0), jnp.max(rb, axis=0) - jnp.min(rb, axis=0)])

    page_rows = PAGE_SIZE * N_KV_HEADS
    n_new = -(-ts * N_KV_HEADS // (2 * SUBLANES)) * (2 * SUBLANES)
    keys_s = n_pages * page_rows + -(-n_new // LANES) * LANES
    n_pos = past_len + ts
    dist = jnp.arange(-(ts - 1), n_pos, dtype=jnp.int32)
    flipped = _bias_of_distance(rel_bias, dist)[:, ::-1]
    by_token = jnp.stack([flipped[:, ts - 1 - t:ts - 1 - t + n_pos] for t in range(ts)], axis=1)
    by_row = jnp.transpose(by_token.reshape(N_KV_HEADS, GQA_GROUP, ts, n_pos), (0, 2, 1, 3))
    own_head = (jnp.arange(N_KV_HEADS)[:, None, None, None, None]
                == jnp.arange(N_KV_HEADS)[None, None, None, None, :])
    bias_rows = jnp.where(own_head, by_row[..., None], MASK_VALUE)
    bias_rows = bias_rows.reshape(N_KV_HEADS * ts * GQA_GROUP, n_pos * N_KV_HEADS)
    bias_rows = jnp.pad(bias_rows, ((0, 0), (0, keys_s - n_pos * N_KV_HEADS)), constant_values=MASK_VALUE)
    bias_s = jnp.concatenate([bias_rows, bias_rows], axis=0)
    pt_flat = page_table.reshape(-1).astype(jnp.int32)
    cache_k_rows = cache_k.reshape(depth, n_pool, page_rows, LANES)
    cache_v_rows = cache_v.reshape(depth, n_pool, page_rows, LANES)

    xp = x_prompt.reshape(t_p, D_MODEL)
    xs = jnp.swapaxes(x_sample, 0, 1).reshape(t_s, D_MODEL)
    outs = [[] for _ in range(8)]
    for l in range(depth):
        lam_init = 0.8 - 0.6 * math.exp(-0.3 * l)
        lam = lam_base[l] + lam_init
        sg_col = subln_g[l].astype(F32).reshape(V_DIM, 1)
        sg_row = subln_g[l].astype(F32).reshape(1, V_DIM)
        scal_s = jnp.stack([lam, jnp.asarray(1.0 - lam_init, F32)]).astype(F32)
        scal_p = jnp.concatenate([scal_s, bias_scal])
        lp = (norm1_g[l].reshape(1, D_MODEL), w_in_bf, qg_row[l], kg_row[l], chunk_p)
        rg = (conv_w[l], cb[l], wa_bd[l], wi_bd[l], ba[l], bi[l], sp[l])
        mg = (norm2_g[l].reshape(1, D_MODEL), w_pa_bf[l], w_pr_bf[l], w_o_bf[l], wr_hi[l], wr_lo[l], br[l])

        q, k, kb, v, xr, yr, ga, gr, vt_ext = _in_proj(xp, mods_p[l], tp, ATTN_SCALE * log2e, blk, l, *lp)
        attn = _attn_prompt(scal_p, q, kb, vt_ext, bias_diag, bias_prev, sg_col, bp, tp)
        rnn, h_last = _rglru_prompt(xr, yr, *rg, bp, tp)
        x1p, u2p, recp, cnt = _merge(xp, attn, rnn, ga, gr, mods_p[l], tp,
                                     jnp.zeros((1, ROUTER_LANES), F32), *mg)
        outs[0].append(k.reshape(bp, tp, N_KV_HEADS, 2 * HEAD_DIM))
        outs[1].append(v.reshape(bp, tp, N_KV_HEADS, V_DIM))
        outs[2].append(h_last.reshape(bp, D_MODEL))
        outs[3].append(xr.reshape(bp, tp, D_MODEL)[:, tp - (CONV_W - 1):])

        q, k, kb, v, xr, yr, ga, gr = _in_proj(xs, mods_s[l], ts, ATTN_SCALE, None, l, *lp)
        q6 = q.reshape(ts, bs, N_KV_HEADS, GQA_GROUP, 2, HEAD_DIM)
        q6 = jnp.transpose(q6, (1, 4, 2, 0, 3, 5))
        zq = jnp.zeros_like(q6[:, 0])
        q_rows = jnp.stack([jnp.concatenate([q6[:, 0], zq], axis=-1),
                            jnp.concatenate([zq, q6[:, 1]], axis=-1)], axis=1)
        q_rows = q_rows.reshape(bs, 2 * N_KV_HEADS * ts * GQA_GROUP, 2 * HEAD_DIM)
        new_rows = lambda z: jnp.pad(jnp.swapaxes(z.reshape(ts, bs, N_KV_HEADS, LANES), 0, 1)
                                     .reshape(bs, ts * N_KV_HEADS, LANES),
                                     ((0, 0), (0, n_new - ts * N_KV_HEADS), (0, 0)))
        o_s = _attn_sample(pt_flat, jnp.full((1,), l, jnp.int32), scal_s, q_rows, new_rows(kb),
                           new_rows(v.astype(BF16)), bias_s, sg_row, cache_k_rows, cache_v_rows, n_pages)
        attn = jnp.transpose(o_s.reshape(bs, N_KV_HEADS, ts, GQA_GROUP, V_DIM), (2, 0, 1, 3, 4))
        attn = attn.reshape(t_s, N_HEADS * V_DIM).astype(BF16)
        buf = jnp.swapaxes(state_conv[l], 0, 1)
        rnn, h_last = _rglru_sample(xr, yr, buf, state_rnn[l], *rg, ts, past_len == 0)
        x1s, u2s, recs, cnt = _merge(xs, attn, rnn, ga, gr, mods_s[l], ts, cnt, *mg)
        to_bt = lambda z, n: jnp.swapaxes(z.reshape(ts, bs, n), 0, 1)
        outs[4].append(to_bt(k, K_COLS).reshape(bs, ts, N_KV_HEADS, 2 * HEAD_DIM))
        outs[5].append(to_bt(v, V_COLS).reshape(bs, ts, N_KV_HEADS, V_DIM))
        outs[6].append(h_last)
        xin = jnp.concatenate([state_conv[l], to_bt(xr, D_MODEL)], axis=1)
        outs[7].append(xin[:, -(CONV_W - 1):])

        pos_flat, te, valid, last_tile, n_sorted = _dispatch_plan(jnp.concatenate([recp, recs], axis=0), cnt)
        u_sorted = _dispatch(pos_flat, u2p, u2s, jnp.zeros((n_sorted, D_MODEL), F32) if l == 0 else u_sorted)
        y_sorted = _experts(te, valid, last_tile, jnp.full((1,), l, jnp.int32), u_sorted,
                            w_e_gate, w_e_up, w_e_down)
        xp = _combine(pos_flat, x1p, recp, y_sorted, mods_p[l], tp, 0, t_all)
        xs = _combine(pos_flat, x1s, recs, y_sorted, mods_s[l], ts, t_p, t_all)

    y_sample = jnp.swapaxes(xs.reshape(ts, bs, D_MODEL), 0, 1)
    stack = lambda i: jnp.stack(outs[i])
    return (xp.reshape(bp, tp, D_MODEL), y_sample, stack(0), stack(1), stack(2), stack(3),
            stack(4), stack(5), stack(6), stack(7))
```

```python
import functools
import math

import jax
import jax.numpy as jnp
from jax import lax
from jax.experimental import pallas as pl
from jax.experimental.pallas import tpu as pltpu

F32 = jnp.float32
BF16 = jnp.bfloat16

D_MODEL = 1024
HEAD_DIM = 64
N_HEADS = 8
N_KV_HEADS = 4
GQA_GROUP = 2
V_DIM = 128
ATTN_SCALE = HEAD_DIM ** -0.5
MASK_VALUE = -1e30
NUM_BUCKETS = 32
MAX_EXACT = 16
MAX_DISTANCE = 128
N_RNN_BLOCKS = 16
RNN_BLOCK = 64
CONV_W = 4
LRU_C = 8.0
N_GROUPS = 4
EXPERTS_PER_GROUP = 8
N_EXPERTS = 32
D_EXPERT = 512
NORM_EPS = 1e-6
PAGE_SIZE = 128
Q_COLS = 1024
K_COLS = 512
V_COLS = 512
D_IN = 6144

LANES = 128
SUBLANES = 8
MXU_DIM = 256
VMEM_LIMIT_BYTES = 56 * 1024 * 1024

TOKEN_TILE = 256
MERGE_TILE = 512
ATTN_BLOCK = 512
FAR_CHAIN = 4
TIME_TILE = 512
EXPERT_TILE = 512
ROUTER_LANES = 128
ONES_ROWS = 16

_NT = (((1,), (1,)), ((), ()))


def _cparams(sem):
    return pltpu.CompilerParams(dimension_semantics=sem, vmem_limit_bytes=VMEM_LIMIT_BYTES)


def _dot(a, b):
    return jnp.dot(a, b, preferred_element_type=F32)


def _resident(shape):
    n = len(shape)
    return pl.BlockSpec(shape, lambda *_: (0,) * n)


_MODS_COLS = 1536


def _mods_body(c_ref, w_ref, b_ref, o_ref):
    c = c_ref[...]
    s = c * jax.nn.sigmoid(c)
    o_ref[...] = jnp.dot(s, w_ref[...], precision=lax.Precision.HIGHEST,
                         preferred_element_type=F32) + b_ref[...]


def _mods(c_all, w_mod, b_mod):
    rows = c_all.shape[0]
    depth, _, n = w_mod.shape
    return pl.pallas_call(
        _mods_body,
        grid=(depth, n // _MODS_COLS),
        in_specs=[pl.BlockSpec((rows, D_MODEL), lambda l, j: (0, 0)),
                  pl.BlockSpec((None, D_MODEL, _MODS_COLS), lambda l, j: (l, 0, j)),
                  pl.BlockSpec((None, 1, _MODS_COLS), lambda l, j: (l, 0, j))],
        out_specs=pl.BlockSpec((None, rows, _MODS_COLS), lambda l, j: (l, 0, j)),
        out_shape=jax.ShapeDtypeStruct((depth, rows, n), F32),
        compiler_params=_cparams(("arbitrary", "arbitrary")),
        name="mods",
    )(c_all, w_mod, b_mod.reshape(depth, 1, n))


def _rms_rows(x, g):
    ms = jnp.mean(x * x, axis=-1, keepdims=True)
    return x * lax.rsqrt(ms + NORM_EPS) * g


def _chunk_mean_square(z, p):
    parts = []
    for j in range(z.shape[1] // MXU_DIM):
        zz = z[:, j * MXU_DIM:(j + 1) * MXU_DIM]
        parts.append(_dot((zz * zz).astype(BF16), p))
    return jnp.concatenate(parts, axis=1)


def _in_proj_body(q_scale, x_ref, sc_ref, sh_ref, g_ref, w_ref, qg_ref, kg_ref, p_ref,
                  q_ref, k_ref, kb_ref, v_ref, xr_ref, yr_ref, ga_ref, gr_ref, vt_ref=None):
    u = (_rms_rows(x_ref[...], g_ref[...]) * (1.0 + sc_ref[...]) + sh_ref[...]).astype(BF16)
    p = p_ref[...]
    gates0 = Q_COLS + K_COLS + V_COLS
    gate_cols = lambda n: slice(gates0 + n * D_MODEL, gates0 + (n + 1) * D_MODEL)
    zq = _dot(u, w_ref[:, 0:Q_COLS])
    xr_ref[...] = _dot(u, w_ref[:, gate_cols(0)])
    q = zq * lax.rsqrt(_chunk_mean_square(zq, p) + NORM_EPS) * qg_ref[...]
    q_ref[...] = (q * q_scale).astype(BF16)
    zk = _dot(u, w_ref[:, Q_COLS:Q_COLS + K_COLS])
    yr_ref[...] = _dot(u, w_ref[:, gate_cols(1)])
    k = zk * lax.rsqrt(_chunk_mean_square(zk, p) + NORM_EPS) * kg_ref[...]
    kb_ref[...] = k.astype(BF16)
    c0 = Q_COLS + K_COLS
    v = _dot(u, w_ref[:, c0:c0 + V_COLS])
    ga_ref[...] = _dot(u, w_ref[:, gate_cols(2)])

    def store_heads(ref, z):
        if ref.ndim == 3:
            for h in range(N_KV_HEADS):
                ref[:, h, :] = z[:, h * LANES:(h + 1) * LANES]
        else:
            ref[...] = z
    store_heads(k_ref, k)
    gr_ref[...] = _dot(u, w_ref[:, gate_cols(3)])
    store_heads(v_ref, v)
    if vt_ref is not None:
        row = lax.broadcasted_iota(jnp.int32, (ONES_ROWS, v.shape[0]), 0)
        ones_rows = jnp.where(row == 0, 1.0, 0.0).astype(BF16)
        for h in range(N_KV_HEADS):
            vt_ref[h, 0:V_DIM, :] = v[:, h * V_DIM:(h + 1) * V_DIM].T.astype(BF16)
            vt_ref[h, V_DIM:V_DIM + ONES_ROWS, :] = ones_rows


def _mod_spec(mods, tile, rows_per_group, col_block):
    if mods.ndim == 3:
        tiles_per_group = rows_per_group // tile
        return pl.BlockSpec((None, 1, D_MODEL), lambda i: (i // tiles_per_group, 0, col_block))
    return pl.BlockSpec((tile, D_MODEL), lambda i: (i, col_block))


def _in_proj(x, mods, rows_per_group, q_scale, vt_block, layer, norm_g, w_in_bf, qg_row, kg_row, chunk_p):
    t = x.shape[0]
    tile = min(TOKEN_TILE, t)
    row = lambda n: pl.BlockSpec((tile, n), lambda i: (i, 0))
    outs = [(Q_COLS, BF16), (K_COLS, F32), (K_COLS, BF16), (V_COLS, F32),
            (D_MODEL, F32), (D_MODEL, F32), (D_MODEL, F32), (D_MODEL, F32)]
    out_specs = [row(n) for n, _ in outs]
    out_shape = [jax.ShapeDtypeStruct((t, n), dt) for n, dt in outs]
    if vt_block is not None:
        for n in (1, 3):
            out_specs[n] = pl.BlockSpec((tile, N_KV_HEADS, LANES), lambda i: (i, 0, 0))
            out_shape[n] = jax.ShapeDtypeStruct((t, N_KV_HEADS, LANES), F32)
        tiles_per_seq = rows_per_group // tile
        per_block = vt_block // tile
        vrows = V_DIM + ONES_ROWS
        out_specs.append(pl.BlockSpec(
            (N_KV_HEADS, None, vrows, tile),
            lambda i: (i // tiles_per_seq, (i % tiles_per_seq) // per_block, 0, (i % tiles_per_seq) % per_block)))
        out_shape.append(jax.ShapeDtypeStruct(
            (t // rows_per_group * N_KV_HEADS, rows_per_group // vt_block, vrows, vt_block), BF16))
    return pl.pallas_call(
        functools.partial(_in_proj_body, q_scale),
        grid=(t // tile,),
        in_specs=[row(D_MODEL),
                  _mod_spec(mods, tile, rows_per_group, 1),
                  _mod_spec(mods, tile, rows_per_group, 0),
                  _resident((1, D_MODEL)),
                  pl.BlockSpec((None, D_MODEL, D_IN), lambda i: (layer, 0, 0)),
                  _resident((1, Q_COLS)),
                  _resident((1, K_COLS)),
                  _resident((MXU_DIM, MXU_DIM))],
        out_specs=out_specs,
        out_shape=out_shape,
        compiler_params=_cparams(("arbitrary",)),
        name="in_proj",
    )(x, mods, mods, norm_g, w_in_bf, qg_row, kg_row, chunk_p)


def _bucket_of_distance(n):
    n = jnp.maximum(n, 0)
    nf = jnp.maximum(n, 1).astype(F32)
    large = MAX_EXACT + (jnp.log(nf / MAX_EXACT) / math.log(MAX_DISTANCE / MAX_EXACT)
                         * (NUM_BUCKETS - MAX_EXACT)).astype(jnp.int32)
    return jnp.where(n < MAX_EXACT, n, jnp.minimum(large, NUM_BUCKETS - 1))


def _bias_of_distance(rel_bias, n):
    b = rel_bias.astype(F32)[_bucket_of_distance(n)]
    b = jnp.where((n >= 0)[..., None], b, MASK_VALUE)
    return jnp.moveaxis(b, -1, 0)


def _bias_table(rel_bias, blk, offset):
    j = jnp.arange(2 * blk, dtype=jnp.int32)
    diff = jnp.where(j < blk, j, j - 2 * blk)
    return _bias_of_distance(rel_bias, diff + offset)[:, None, :]


_S_LAM, _S_OUT = 0, 1
_S_FAR = 2
_S_BMAX = _S_FAR + N_HEADS
_S_BRANGE = _S_BMAX + N_HEADS
BOUND_MARGIN = 1.01
SAFE_EXPONENT_SPAN = 100.0


def _attn_prompt_body(scal_ref, q_ref, k_ref, vt_ref, tdiag_ref, tprev_ref, sg_ref,
                      o_ref, qpad_sc, kmax_sc, bq_sc, m_sc, acc_sc, bdiag_ref, bprev_ref):
    h = pl.program_id(1)
    qi = pl.program_id(2)
    blk = q_ref.shape[0]
    ncombo = 2 * GQA_GROUP
    lam = scal_ref[_S_LAM]
    out_scale = scal_ref[_S_OUT]

    @pl.when(qi == 0)
    def _():
        kk = k_ref[...].astype(F32)
        r = lax.broadcasted_iota(jnp.int32, (LANES, LANES), 0) // HEAD_DIM
        c = lax.broadcasted_iota(jnp.int32, (LANES, LANES), 1) // HEAD_DIM
        same = jnp.where(r == c, 1.0, 0.0).astype(BF16)
        ksq = _dot((kk * kk).astype(BF16), same)
        kmax_sc[...] = jnp.sqrt(jnp.max(ksq, axis=0, keepdims=True))
        for table_ref, tile_ref in ((tdiag_ref, bdiag_ref), (tprev_ref, bprev_ref)):
            for g in range(GQA_GROUP):
                rows = jnp.broadcast_to(table_ref[g], (blk, 2 * blk))
                tile_ref[g] = pltpu.roll(rows, 0, 1, stride=1, stride_axis=0)[:, :blk]

    lane = lax.broadcasted_iota(jnp.int32, (blk, LANES), 1)
    ones = jnp.ones((SUBLANES, LANES), BF16)
    span = jnp.zeros((1, 1), F32)
    for g in range(GQA_GROUP):
        q2 = q_ref[:, g * LANES:(g + 1) * LANES]
        for c in range(2):
            idx = 2 * g + c
            qp = jnp.where((lane < HEAD_DIM) == (c == 0), q2, jnp.zeros_like(q2))
            qpad_sc[idx] = qp
            qf = qp.astype(F32)
            qsq = lax.dot_general(ones, (qf * qf).astype(BF16), _NT, preferred_element_type=F32)[0:1]
            kmax = jnp.max(kmax_sc[:, c * HEAD_DIM:(c + 1) * HEAD_DIM], axis=1, keepdims=True)
            bqk = BOUND_MARGIN * jnp.sqrt(qsq) * kmax
            bq_sc[idx] = bqk + scal_ref[_S_BMAX + h * GQA_GROUP + g]
            span = jnp.maximum(span, 2.0 * jnp.max(bqk, axis=1, keepdims=True)
                               + scal_ref[_S_BRANGE + h * GQA_GROUP + g])
    acc_sc[...] = jnp.zeros(acc_sc.shape, F32)
    far_bias = lambda g: scal_ref[_S_FAR + h * GQA_GROUP + g]
    n_far = jnp.maximum(qi - 1, 0)

    def k_block(kj):
        return k_ref[pl.ds(pl.multiple_of(kj * blk, blk), blk), :]

    def fast_chain(steps):
        work = [(s, idx) for s in range(len(steps)) for idx in range(ncombo)]
        k_blks = [k_block(kj) for kj, _, _ in steps]
        vts = [vt_ref[kj] for kj, _, _ in steps]
        qk = lambda w: lax.dot_general(k_blks[w[0]], qpad_sc[w[1]], _NT, preferred_element_type=F32)
        st_next = qk(work[0])
        for n, (s, idx) in enumerate(work):
            st = st_next
            if n + 1 < len(work):
                st_next = qk(work[n + 1])
            _, shift_of_group, tile_of_group = steps[s]
            g = idx // 2
            if tile_of_group is not None:
                st = st + tile_of_group(g)
            bound = bq_sc[idx] if shift_of_group is None else bq_sc[idx] - shift_of_group(g)
            acc_sc[idx] += _dot(vts[s], jnp.exp2(st - bound).astype(BF16))

    def exact_step(kj, shift_of_group, tile_of_group):
        k_blk = k_block(kj)
        vt = vt_ref[kj]
        for idx in range(ncombo):
            g = idx // 2
            st = lax.dot_general(k_blk, qpad_sc[idx], _NT, preferred_element_type=F32)
            if tile_of_group is not None:
                st = st + tile_of_group(g)
            shift = 0.0 if shift_of_group is None else shift_of_group(g)
            m_old = m_sc[idx]
            m_new = jnp.maximum(m_old, jnp.max(st, axis=0, keepdims=True) + shift)
            p = jnp.exp2(st - (m_new - shift)).astype(BF16)
            acc_sc[idx] = jnp.exp2(m_old - m_new) * acc_sc[idx] + _dot(vt, p)
            m_sc[idx] = m_new

    prev_tile = lambda g: bprev_ref[g]
    diag_tile = lambda g: bdiag_ref[g]

    def run_fast():
        far = lambda kj: (kj, far_bias, None)
        near = [(qi - 1, None, prev_tile), (qi, None, diag_tile)]
        n_quads = n_far // FAR_CHAIN

        def far_quad(j, carry):
            fast_chain([far(FAR_CHAIN * j + s) for s in range(FAR_CHAIN)])
            return carry
        lax.fori_loop(0, n_quads, far_quad, 0)
        left = n_far - FAR_CHAIN * n_quads

        @pl.when(left >= 2)
        def _():
            fast_chain([far(FAR_CHAIN * n_quads), far(FAR_CHAIN * n_quads + 1)])

        @pl.when(left % 2 == 1)
        def _():
            fast_chain([far(n_far - 1)] + near)

        @pl.when((left % 2 == 0) & (qi >= 1))
        def _():
            fast_chain(near)

        @pl.when(qi == 0)
        def _():
            fast_chain(near[1:])

    def run_exact():
        m_sc[...] = jnp.full(m_sc.shape, -jnp.inf, F32)

        def far_step(kj, carry):
            exact_step(kj, far_bias, None)
            return carry
        lax.fori_loop(0, n_far, far_step, 0)

        @pl.when(qi >= 1)
        def _():
            exact_step(qi - 1, None, prev_tile)
        exact_step(qi, None, diag_tile)

    lax.cond(span[0, 0] <= SAFE_EXPONENT_SPAN, run_fast, run_exact)

    for g in range(GQA_GROUP):
        a0 = acc_sc[2 * g]
        a1 = acc_sc[2 * g + 1]
        o = a0[:V_DIM] * (1.0 / a0[V_DIM:V_DIM + 1]) - a1[:V_DIM] * (lam / a1[V_DIM:V_DIM + 1])
        o = o * lax.rsqrt(jnp.mean(o * o, axis=0, keepdims=True) + NORM_EPS)
        o = o * sg_ref[...] * out_scale
        o_ref[:, g * V_DIM:(g + 1) * V_DIM] = o.T.astype(BF16)


def _attn_prompt(scal, q_bf, k_bf, vt_ext, bias_diag, bias_prev, sg_col, batch, seq):
    blk = ATTN_BLOCK
    nblk = seq // blk
    vrows = V_DIM + ONES_ROWS
    ncombo = 2 * GQA_GROUP
    grid_spec = pltpu.PrefetchScalarGridSpec(
        num_scalar_prefetch=1,
        grid=(batch, N_KV_HEADS, nblk),
        in_specs=[pl.BlockSpec((blk, GQA_GROUP * 2 * HEAD_DIM), lambda b, h, i, s: (b * nblk + i, h)),
                  pl.BlockSpec((seq, 2 * HEAD_DIM), lambda b, h, i, s: (b, h)),
                  pl.BlockSpec((None, nblk, vrows, blk), lambda b, h, i, s: (b * N_KV_HEADS + h, 0, 0, 0)),
                  pl.BlockSpec((GQA_GROUP, 1, 2 * blk), lambda b, h, i, s: (h, 0, 0)),
                  pl.BlockSpec((GQA_GROUP, 1, 2 * blk), lambda b, h, i, s: (h, 0, 0)),
                  pl.BlockSpec((V_DIM, 1), lambda b, h, i, s: (0, 0))],
        out_specs=pl.BlockSpec((blk, GQA_GROUP * V_DIM), lambda b, h, i, s: (b * nblk + i, h)),
        scratch_shapes=[pltpu.VMEM((ncombo, blk, LANES), BF16),
                        pltpu.VMEM((1, LANES), F32),
                        pltpu.VMEM((ncombo, 1, blk), F32),
                        pltpu.VMEM((ncombo, 1, blk), F32),
                        pltpu.VMEM((ncombo, vrows, blk), F32),
                        pltpu.VMEM((GQA_GROUP, blk, blk), F32),
                        pltpu.VMEM((GQA_GROUP, blk, blk), F32)],
    )
    return pl.pallas_call(
        _attn_prompt_body,
        grid_spec=grid_spec,
        out_shape=jax.ShapeDtypeStruct((batch * seq, N_HEADS * V_DIM), BF16),
        compiler_params=_cparams(("arbitrary", "arbitrary", "arbitrary")),
        name="attn_prompt",
    )(scal, q_bf, k_bf, vt_ext, bias_diag, bias_prev, sg_col)


def _attn_sample_body(n_pages, pt_ref, layer_ref, scal_ref, q_ref, knew_ref, vnew_ref, bias_ref, sg_ref, *refs):
    k_pages = refs[:n_pages]
    v_pages = refs[n_pages:2 * n_pages]
    o_ref, kb_sc, vb_sc = refs[2 * n_pages:]
    page_rows = k_pages[0].shape[2]
    past = n_pages * page_rows
    n_new = knew_ref.shape[1]
    lam = scal_ref[0]
    out_scale = scal_ref[1]
    half = q_ref.shape[1] // 2

    @pl.when(pl.program_id(0) == 0)
    def _():
        tail = kb_sc.shape[0] - past - n_new
        kb_sc[past + n_new:, :] = jnp.zeros((tail, LANES), BF16)
        vb_sc[past + n_new:, :] = jnp.zeros((tail, LANES), BF16)

    for j in range(n_pages):
        kb_sc[j * page_rows:(j + 1) * page_rows, :] = k_pages[j][0, 0].astype(BF16)
        vb_sc[j * page_rows:(j + 1) * page_rows, :] = v_pages[j][0, 0].astype(BF16)
    kb_sc[past:past + n_new, :] = knew_ref[0]
    vb_sc[past:past + n_new, :] = vnew_ref[0]
    s = lax.dot_general(q_ref[0], kb_sc[...], _NT, preferred_element_type=F32) + bias_ref[...]
    p = jnp.exp(s - jnp.max(s, axis=-1, keepdims=True))
    p = p / jnp.sum(p, axis=-1, keepdims=True)
    a = p[:half] - lam * p[half:]
    o = _dot(a.astype(BF16), vb_sc[...])
    o = o * lax.rsqrt(jnp.mean(o * o, axis=-1, keepdims=True) + NORM_EPS)
    o_ref[0] = o * sg_ref[...] * out_scale


def _attn_sample(page_table_flat, layer_idx, scal, q_rows, knew, vnew, bias_s, sg_row, cache_k, cache_v, n_pages):
    nb, rows, _ = q_rows.shape
    n_new = knew.shape[1]
    page_rows = cache_k.shape[2]
    keys = bias_s.shape[1]

    def page_spec(j):
        return pl.BlockSpec((1, 1, page_rows, LANES), lambda b, pt, l: (l[0], pt[b * n_pages + j], 0, 0))

    grid_spec = pltpu.PrefetchScalarGridSpec(
        num_scalar_prefetch=2,
        grid=(nb,),
        in_specs=[pl.BlockSpec(memory_space=pltpu.SMEM),
                  pl.BlockSpec((1, rows, LANES), lambda b, pt, l: (b, 0, 0)),
                  pl.BlockSpec((1, n_new, LANES), lambda b, pt, l: (b, 0, 0)),
                  pl.BlockSpec((1, n_new, LANES), lambda b, pt, l: (b, 0, 0)),
                  pl.BlockSpec((rows, keys), lambda b, pt, l: (0, 0)),
                  pl.BlockSpec((1, V_DIM), lambda b, pt, l: (0, 0))]
                 + [page_spec(j) for j in range(n_pages)] * 2,
        out_specs=pl.BlockSpec((1, rows // 2, V_DIM), lambda b, pt, l: (b, 0, 0)),
        scratch_shapes=[pltpu.VMEM((keys, LANES), BF16),
                        pltpu.VMEM((keys, LANES), BF16)],
    )
    return pl.pallas_call(
        functools.partial(_attn_sample_body, n_pages),
        grid_spec=grid_spec,
        out_shape=jax.ShapeDtypeStruct((nb, rows // 2, V_DIM), F32),
        compiler_params=_cparams(("arbitrary",)),
        name="attn_sample",
    )(page_table_flat, layer_idx, scal, q_rows, knew, vnew, bias_s, sg_row,
      *([cache_k] * n_pages), *([cache_v] * n_pages))


def _lru_coeffs(xc, wa_ref, wi_ref, ba_ref, bi_ref, sp_ref):
    xb = xc.astype(BF16)
    ra, ri = [], []
    for j in range(xc.shape[1] // MXU_DIM):
        cols = slice(j * MXU_DIM, (j + 1) * MXU_DIM)
        ra.append(_dot(xb[:, cols], wa_ref[j]))
        ri.append(_dot(xb[:, cols], wi_ref[j]))
    r = jax.nn.sigmoid(jnp.concatenate(ra, axis=1) + ba_ref[...])
    i = jax.nn.sigmoid(jnp.concatenate(ri, axis=1) + bi_ref[...])
    log_a = -LRU_C * r * sp_ref[...]
    a = jnp.exp(log_a)
    mult = jnp.sqrt(-jnp.tanh(log_a) * (a * a + 1.0))
    return a, mult, i


def _rglru_prompt_body(xr_ref, yr_ref, cw_ref, cb_ref, wa_ref, wi_ref, ba_ref, bi_ref, sp_ref,
                       rnn_ref, hl_ref, xe_sc, a_sc, b_sc, h_sc):
    i = pl.program_id(1)
    tt = xr_ref.shape[0]

    @pl.when(i == 0)
    def _():
        xe_sc[0:SUBLANES, :] = jnp.zeros((SUBLANES, D_MODEL), F32)
        h_sc[...] = jnp.zeros(h_sc.shape, F32)

    xe_sc[SUBLANES:SUBLANES + tt, :] = xr_ref[...]
    xc = cb_ref[...]
    for j in range(CONV_W):
        off = SUBLANES - (CONV_W - 1) + j
        xc = xc + xe_sc[off:off + tt, :] * cw_ref[j:j + 1, :]
    a, mult, gate_i = _lru_coeffs(xc, wa_ref, wi_ref, ba_ref, bi_ref, sp_ref)
    row = lax.broadcasted_iota(jnp.int32, (tt, 1), 0)
    mult = jnp.where((row == 0) & (i == 0), 1.0, mult)
    a_sc[...] = a
    b_sc[...] = mult * gate_i * xc

    def step(t, h):
        h = a_sc[pl.ds(t, 1), :] * h + b_sc[pl.ds(t, 1), :]
        b_sc[pl.ds(t, 1), :] = h
        return h

    h = lax.fori_loop(0, tt, step, h_sc[...], unroll=8)
    h_sc[...] = h
    hl_ref[...] = h
    rnn_ref[...] = (b_sc[...] * jax.nn.gelu(yr_ref[...])).astype(BF16)
    xe_sc[0:SUBLANES, :] = xe_sc[tt:tt + SUBLANES, :]


def _rglru_prompt(xr, yr, cw, cb, wa_bd, wi_bd, ba, bi, sp, batch, seq):
    tt = min(TIME_TILE, seq)
    nt = seq // tt
    row = pl.BlockSpec((tt, D_MODEL), lambda b, i: (b * nt + i, 0))
    nbd = D_MODEL // MXU_DIM
    return pl.pallas_call(
        _rglru_prompt_body,
        grid=(batch, nt),
        in_specs=[row, row,
                  pl.BlockSpec((CONV_W, D_MODEL), lambda b, i: (0, 0)),
                  pl.BlockSpec((1, D_MODEL), lambda b, i: (0, 0)),
                  pl.BlockSpec((nbd, MXU_DIM, MXU_DIM), lambda b, i: (0, 0, 0)),
                  pl.BlockSpec((nbd, MXU_DIM, MXU_DIM), lambda b, i: (0, 0, 0)),
                  pl.BlockSpec((1, D_MODEL), lambda b, i: (0, 0)),
                  pl.BlockSpec((1, D_MODEL), lambda b, i: (0, 0)),
                  pl.BlockSpec((1, D_MODEL), lambda b, i: (0, 0))],
        out_specs=[row, pl.BlockSpec((None, 1, D_MODEL), lambda b, i: (b, 0, 0))],
        out_shape=[jax.ShapeDtypeStruct((batch * seq, D_MODEL), BF16),
                   jax.ShapeDtypeStruct((batch, 1, D_MODEL), F32)],
        scratch_shapes=[pltpu.VMEM((tt + SUBLANES, D_MODEL), F32),
                        pltpu.VMEM((tt, D_MODEL), F32),
                        pltpu.VMEM((tt, D_MODEL), F32),
                        pltpu.VMEM((1, D_MODEL), F32)],
        compiler_params=_cparams(("arbitrary", "arbitrary")),
        name="rglru_prompt",
    )(xr, yr, cw, cb, wa_bd, wi_bd, ba, bi, sp)


def _rglru_sample_body(n_steps, first_is_start, xr_ref, yr_ref, buf_ref, h0_ref, cw_ref, cb_ref, wa_ref, wi_ref, ba_ref, bi_ref,
                       sp_ref, rnn_ref, hl_ref):
    nb = h0_ref.shape[0]
    taps = [buf_ref[j] for j in range(CONV_W - 1)] + [xr_ref[t * nb:(t + 1) * nb, :] for t in range(n_steps)]
    h = h0_ref[...]
    for t in range(n_steps):
        xc = cb_ref[...]
        for j in range(CONV_W):
            xc = xc + taps[t + j] * cw_ref[j:j + 1, :]
        a, mult, gate_i = _lru_coeffs(xc, wa_ref, wi_ref, ba_ref, bi_ref, sp_ref)
        if first_is_start and t == 0:
            mult = jnp.ones_like(mult)
        h = a * h + mult * gate_i * xc
        rnn_ref[t * nb:(t + 1) * nb, :] = (h * jax.nn.gelu(yr_ref[t * nb:(t + 1) * nb, :])).astype(BF16)
    hl_ref[...] = h


def _rglru_sample(xr, yr, buf, h0, cw, cb, wa_bd, wi_bd, ba, bi, sp, n_steps, first_is_start):
    t = xr.shape[0]
    return pl.pallas_call(
        functools.partial(_rglru_sample_body, n_steps, first_is_start),
        out_shape=[jax.ShapeDtypeStruct((t, D_MODEL), BF16),
                   jax.ShapeDtypeStruct(h0.shape, F32)],
        compiler_params=pltpu.CompilerParams(vmem_limit_bytes=VMEM_LIMIT_BYTES),
        name="rglru_sample",
    )(xr, yr, buf, h0, cw, cb, wa_bd, wi_bd, ba, bi, sp)


def _route(logits):
    lane = lax.broadcasted_iota(jnp.int32, logits.shape, 1).astype(F32)
    none = float(ROUTER_LANES)
    is_group = lane < N_GROUPS
    lg = jnp.where(is_group, logits, -jnp.inf)
    eg = jnp.exp(lg - jnp.max(lg, axis=-1, keepdims=True))
    gp = eg / jnp.sum(eg, axis=-1, keepdims=True)
    gw = jnp.max(gp, axis=-1, keepdims=True)
    gi = jnp.min(jnp.where(is_group & (gp == gw), lane, none), axis=-1, keepdims=True)
    first = N_GROUPS + gi * EXPERTS_PER_GROUP
    in_group = (lane >= first) & (lane < first + EXPERTS_PER_GROUP)
    le = jnp.where(in_group, logits, -jnp.inf)
    ee = jnp.exp(le - jnp.max(le, axis=-1, keepdims=True))
    pe = ee / jnp.sum(ee, axis=-1, keepdims=True)
    p1 = jnp.max(pe, axis=-1, keepdims=True)
    i1 = jnp.min(jnp.where(in_group & (pe == p1), lane, none), axis=-1, keepdims=True)
    rest = in_group & (lane != i1)
    pr = jnp.where(rest, pe, -1.0)
    p2 = jnp.max(pr, axis=-1, keepdims=True)
    i2 = jnp.min(jnp.where(rest & (pr == p2), lane, none), axis=-1, keepdims=True)
    den = p1 + p2
    w1 = gw * (p1 / den)
    w2 = gw * (p2 / den)
    rec = jnp.where(lane == 0.0, i1 - N_GROUPS, 0.0)
    rec = jnp.where(lane == 1.0, i2 - N_GROUPS, rec)
    rec = jnp.where(lane == 2.0, w1, rec)
    rec = jnp.where(lane == 3.0, w2, rec)
    return rec, jnp.where(lane == i1, 1.0, 0.0), jnp.where(lane == i2, 1.0, 0.0)


def _merge_body(x_ref, attn_ref, rnn_ref, ga_ref, gr_ref, g1_ref, sc2_ref, sh2_ref, n2g_ref,
                wpa_ref, wpr_ref, wo_ref, wrh_ref, wrl_ref, br_ref, ltri_ref, cnt_in_ref,
                x1_ref, u2_ref, rec_ref, cnt_ref, cnt_sc):
    tile = x_ref.shape[0]
    n_parts = 2 if tile % (2 * SUBLANES) == 0 else 1
    parts = [slice(p * tile // n_parts, (p + 1) * tile // n_parts) for p in range(n_parts)]
    per_row = lambda ref, rows: ref[rows, :] if ref.shape[0] == tile else ref[...]
    def proj(rows):
        return _dot(attn_ref[rows, :], wpa_ref[...]), _dot(rnn_ref[rows, :], wpr_ref[...])

    def mixed(rows, a, r):
        m = jax.nn.sigmoid(ga_ref[rows, :]) * a + jax.nn.sigmoid(gr_ref[rows, :]) * r
        return _dot(m.astype(BF16), wo_ref[...])

    def residual(rows, mx):
        x1 = x_ref[rows, :] + per_row(g1_ref, rows) * mx
        x1_ref[rows, :] = x1
        u2 = _rms_rows(x1, n2g_ref[...]) * (1.0 + per_row(sc2_ref, rows)) + per_row(sh2_ref, rows)
        u2_ref[rows, :] = u2
        u_hi = u2.astype(BF16)
        u_lo = (u2 - u_hi.astype(F32)).astype(BF16)
        return (_dot(u_hi, wrh_ref[...]) + _dot(u_lo, wrh_ref[...]) + _dot(u_hi, wrl_ref[...])) + br_ref[...]

    projs = [proj(parts[0])]
    mixes, logits = [], []
    for p in range(n_parts):
        if p + 1 < n_parts:
            projs.append(proj(parts[p + 1]))
        mixes.append(mixed(parts[p], *projs[p]))
        if p >= 1:
            logits.append(residual(parts[p - 1], mixes[p - 1]))
    logits.append(residual(parts[-1], mixes[-1]))
    rec, hot1, hot2 = _route(jnp.concatenate(logits, axis=0))

    @pl.when(pl.program_id(0) == 0)
    def _():
        cnt_sc[...] = cnt_in_ref[...]
    cnt = cnt_sc[...]
    tot1 = jnp.sum(hot1, axis=0, keepdims=True)
    tot2 = jnp.sum(hot2, axis=0, keepdims=True)
    before1 = _dot(ltri_ref[...], hot1.astype(BF16)) + cnt
    before2 = _dot(ltri_ref[...], hot2.astype(BF16)) + (cnt + tot1)
    rank1 = jnp.sum(hot1 * before1, axis=-1, keepdims=True)
    rank2 = jnp.sum(hot2 * before2, axis=-1, keepdims=True)
    cnt_sc[...] = cnt + tot1 + tot2
    cnt_ref[...] = cnt_sc[...]
    lane = lax.broadcasted_iota(jnp.int32, rec.shape, 1)
    rec = jnp.where(lane == 4, rank1, rec)
    rec_ref[...] = jnp.where(lane == 5, rank2, rec)


def _merge(x, attn, rnn, ga, gr, mods, rows_per_group, cnt_in, n2g, wpa, wpr, wo, wr_hi, wr_lo, br):
    t = x.shape[0]
    tile = min(MERGE_TILE, t)
    assert t % tile == 0 and (mods.ndim == 2 or rows_per_group % tile == 0)
    row = lambda n: pl.BlockSpec((tile, n), lambda i: (i, 0))
    sq = _resident((D_MODEL, D_MODEL))
    r = jnp.arange(tile)
    ltri = (r[None, :] < r[:, None]).astype(BF16)
    return pl.pallas_call(
        _merge_body,
        grid=(t // tile,),
        in_specs=[row(D_MODEL), row(D_MODEL), row(D_MODEL), row(D_MODEL), row(D_MODEL),
                  _mod_spec(mods, tile, rows_per_group, 2),
                  _mod_spec(mods, tile, rows_per_group, 4),
                  _mod_spec(mods, tile, rows_per_group, 3),
                  _resident((1, D_MODEL)), sq, sq, sq,
                  _resident((D_MODEL, ROUTER_LANES)), _resident((D_MODEL, ROUTER_LANES)),
                  _resident((1, ROUTER_LANES)), _resident((tile, tile)), _resident((1, ROUTER_LANES))],
        out_specs=[row(D_MODEL), row(D_MODEL), row(ROUTER_LANES), _resident((1, ROUTER_LANES))],
        out_shape=[jax.ShapeDtypeStruct((t, D_MODEL), F32),
                   jax.ShapeDtypeStruct((t, D_MODEL), F32),
                   jax.ShapeDtypeStruct((t, ROUTER_LANES), F32),
                   jax.ShapeDtypeStruct((1, ROUTER_LANES), F32)],
        scratch_shapes=[pltpu.VMEM((1, ROUTER_LANES), F32)],
        compiler_params=_cparams(("arbitrary",)),
        name="merge",
    )(x, attn, rnn, ga, gr, mods, mods, mods, n2g, wpa, wpr, wo, wr_hi, wr_lo, br, ltri, cnt_in)


ROW_UNROLL = 8


def _for_rows(n_rows, fn):
    def body(j, carry):
        for s in range(ROW_UNROLL):
            fn(j * ROW_UNROLL + s)
        return carry
    lax.fori_loop(0, n_rows // ROW_UNROLL, body, 0)


def _dispatch_body(t_all, n_first, pos_ref, ua_ref, ub_ref, init_hbm, us_hbm, sem):
    del init_hbm
    i = pl.program_id(0)

    def scatter_tile(u_ref, base):
        tile = u_ref.shape[0]

        def issue(r):
            for slot in range(2):
                pltpu.make_async_copy(u_ref.at[pl.ds(r, 1)],
                                      us_hbm.at[pl.ds(pos_ref[slot * t_all + base + r], 1)], sem.at[0]).start()
        _for_rows(tile, issue)
        for _ in range(2):
            pltpu.make_async_copy(u_ref, us_hbm.at[pl.ds(0, tile)], sem.at[0]).wait()

    @pl.when(i < n_first)
    def _():
        scatter_tile(ua_ref, i * ua_ref.shape[0])

    @pl.when(i >= n_first)
    def _():
        scatter_tile(ub_ref, n_first * ua_ref.shape[0] + (i - n_first) * ub_ref.shape[0])


def _dispatch(pos_flat, u_first, u_second, init):
    n_sorted_rows = init.shape[0]
    t_all = u_first.shape[0] + u_second.shape[0]
    tile_a = min(TOKEN_TILE, u_first.shape[0])
    tile_b = min(TOKEN_TILE, u_second.shape[0])
    n_a = u_first.shape[0] // tile_a
    n_b = u_second.shape[0] // tile_b
    assert n_a * tile_a == u_first.shape[0] and n_b * tile_b == u_second.shape[0]
    assert tile_a % ROW_UNROLL == 0 and tile_b % ROW_UNROLL == 0
    grid_spec = pltpu.PrefetchScalarGridSpec(
        num_scalar_prefetch=1,
        grid=(n_a + n_b,),
        in_specs=[pl.BlockSpec((tile_a, D_MODEL), lambda i, pos: (jnp.minimum(i, n_a - 1), 0)),
                  pl.BlockSpec((tile_b, D_MODEL), lambda i, pos: (jnp.maximum(i - n_a, 0), 0)),
                  pl.BlockSpec(memory_space=pl.ANY)],
        out_specs=pl.BlockSpec(memory_space=pl.ANY),
        scratch_shapes=[pltpu.SemaphoreType.DMA((1,))],
    )
    return pl.pallas_call(
        functools.partial(_dispatch_body, t_all, n_a),
        grid_spec=grid_spec,
        out_shape=jax.ShapeDtypeStruct((n_sorted_rows, D_MODEL), F32),
        input_output_aliases={3: 0},
        compiler_params=_cparams(("arbitrary",)),
        name="dispatch",
    )(pos_flat, u_first, u_second, init)


def _experts_body(te_ref, nv_ref, last_ref, layer_ref, u_ref, wg_ref, wu_ref, wd_ref, y_ref, wg_sc, wu_sc, wd_sc):
    del last_ref, layer_ref
    i = pl.program_id(0)

    @pl.when((i == 0) | (te_ref[i] != te_ref[jnp.maximum(i - 1, 0)]))
    def _():
        wg_sc[...] = wg_ref[...].astype(BF16)
        wu_sc[...] = wu_ref[...].astype(BF16)
        wd_sc[...] = wd_ref[...].astype(BF16)

    @pl.when(nv_ref[i] > 0)
    def _():
        u = u_ref[...].astype(BF16)
        hg = _dot(u, wg_sc[...])
        hid = (hg * jax.nn.sigmoid(hg)) * _dot(u, wu_sc[...])
        y_ref[...] = _dot(hid.astype(BF16), wd_sc[...])

    @pl.when(nv_ref[i] == 0)
    def _():
        y_ref[...] = jnp.zeros(y_ref.shape, F32)


def _experts(tile_expert, tile_valid, last_tile, layer_idx, u_sorted, w_gate, w_up, w_down):
    tile = EXPERT_TILE
    n_tiles = tile_expert.shape[0]
    wspec = lambda a, b: pl.BlockSpec((None, None, a, b), lambda i, te, nv, last, lyr: (lyr[0], te[i], 0, 0))
    grid_spec = pltpu.PrefetchScalarGridSpec(
        num_scalar_prefetch=4,
        grid=(n_tiles,),
        in_specs=[pl.BlockSpec((tile, D_MODEL), lambda i, te, nv, last, lyr: (jnp.minimum(i, last[0]), 0)),
                  wspec(D_MODEL, D_EXPERT), wspec(D_MODEL, D_EXPERT), wspec(D_EXPERT, D_MODEL)],
        out_specs=pl.BlockSpec((tile, D_MODEL), lambda i, te, nv, last, lyr: (i, 0)),
        scratch_shapes=[pltpu.VMEM((D_MODEL, D_EXPERT), BF16),
                        pltpu.VMEM((D_MODEL, D_EXPERT), BF16),
                        pltpu.VMEM((D_EXPERT, D_MODEL), BF16)],
    )
    return pl.pallas_call(
        _experts_body,
        grid_spec=grid_spec,
        out_shape=jax.ShapeDtypeStruct(u_sorted.shape, F32),
        compiler_params=_cparams(("arbitrary",)),
        name="experts",
    )(tile_expert, tile_valid, last_tile, layer_idx, u_sorted, w_gate, w_up, w_down)


def _dispatch_plan(rec_all, counts_row):
    t_all = rec_all.shape[0]
    tile = EXPERT_TILE
    n_tiles = -(-2 * t_all // tile) + N_EXPERTS
    counts = counts_row[0, N_GROUPS:N_GROUPS + N_EXPERTS].astype(jnp.int32)
    tiles_per_e = (counts + tile - 1) // tile
    tile_end = jnp.cumsum(tiles_per_e)
    tile_start = tile_end - tiles_per_e
    experts = jnp.arange(N_EXPERTS, dtype=jnp.int32)
    e = rec_all[:, 0:2].astype(jnp.int32)
    rank = rec_all[:, 4:6].astype(jnp.int32)
    row_start = jnp.sum(jnp.where(e[:, :, None] == experts, tile_start * tile, 0), axis=-1)
    pos_flat = (row_start + rank).T.reshape(-1)
    n_used = tile_end[-1]
    tile_ids = jnp.arange(n_tiles, dtype=jnp.int32)
    clamped = jnp.minimum(tile_ids, n_used - 1)
    te = jnp.sum((clamped[:, None] >= tile_end[None, :]).astype(jnp.int32), axis=-1)
    te = jnp.minimum(te, N_EXPERTS - 1)
    first = jnp.sum(jnp.where(te[:, None] == experts, tile_start, 0), axis=-1)
    cnt_t = jnp.sum(jnp.where(te[:, None] == experts, counts, 0), axis=-1)
    valid = jnp.clip(cnt_t - (tile_ids - first) * tile, 0, tile)
    valid = jnp.where(tile_ids < n_used, valid, 0).astype(jnp.int32)
    return pos_flat, te, valid, (n_used - 1).reshape(1).astype(jnp.int32), n_tiles * tile


def _combine_body(t_all, row_offset, pos_ref, x_ref, rec_ref, g2_ref, ys_hbm, o_ref, ybuf, sem):
    i = pl.program_id(0)
    n = pl.num_programs(0)
    tile = x_ref.shape[0]
    slot = i % 2

    def start_gather(tile_idx, buf):
        base = row_offset + tile_idx * tile

        def issue(r):
            for k in range(2):
                pltpu.make_async_copy(ys_hbm.at[pl.ds(pos_ref[k * t_all + base + r], 1)],
                                      ybuf.at[buf, k, pl.ds(r, 1)], sem.at[buf]).start()
        _for_rows(tile, issue)

    @pl.when(i == 0)
    def _():
        start_gather(0, 0)

    @pl.when(i + 1 < n)
    def _():
        start_gather(i + 1, 1 - slot)

    for k in range(2):
        pltpu.make_async_copy(ys_hbm.at[pl.ds(0, tile)], ybuf.at[slot, k], sem.at[slot]).wait()
    w1 = rec_ref[:, 2:3]
    w2 = rec_ref[:, 3:4]
    o_ref[...] = x_ref[...] + g2_ref[...] * (w1 * ybuf[slot, 0] + w2 * ybuf[slot, 1])


def _combine(pos_flat, x1, rec, y_sorted, mods, rows_per_group, row_offset, t_all):
    t = x1.shape[0]
    tile = min(TOKEN_TILE, t)
    assert tile % ROW_UNROLL == 0
    if mods.ndim == 3:
        tiles_per_group = rows_per_group // tile
        g2_spec = pl.BlockSpec((None, 1, D_MODEL), lambda i, pos: (i // tiles_per_group, 0, 5))
    else:
        g2_spec = pl.BlockSpec((tile, D_MODEL), lambda i, pos: (i, 5))
    grid_spec = pltpu.PrefetchScalarGridSpec(
        num_scalar_prefetch=1,
        grid=(t // tile,),
        in_specs=[pl.BlockSpec((tile, D_MODEL), lambda i, pos: (i, 0)),
                  pl.BlockSpec((tile, ROUTER_LANES), lambda i, pos: (i, 0)),
                  g2_spec,
                  pl.BlockSpec(memory_space=pl.ANY)],
        out_specs=pl.BlockSpec((tile, D_MODEL), lambda i, pos: (i, 0)),
        scratch_shapes=[pltpu.VMEM((2, 2, tile, D_MODEL), F32),
                        pltpu.SemaphoreType.DMA((2,))],
    )
    return pl.pallas_call(
        functools.partial(_combine_body, t_all, row_offset),
        grid_spec=grid_spec,
        out_shape=jax.ShapeDtypeStruct((t, D_MODEL), F32),
        compiler_params=_cparams(("arbitrary",)),
        name="combine",
    )(pos_flat, x1, rec, mods, y_sorted)


def _block_diag(w):
    per = MXU_DIM // RNN_BLOCK
    w4 = w.reshape(N_RNN_BLOCKS // per, per, RNN_BLOCK, RNN_BLOCK)
    eye = jnp.eye(per, dtype=w.dtype)
    return jnp.einsum("jmcd,mn->jmcnd", w4, eye).reshape(N_RNN_BLOCKS // per, MXU_DIM, MXU_DIM)


def kernel(x_prompt, x_sample, cache_k, cache_v, state_rnn, state_conv, page_table, c_prompt, c_sample, rel_bias, w_mod, b_mod, norm1_g, norm2_g, w_in, q_norm_g, k_norm_g, lam_q1, lam_k1, lam_q2, lam_k2, subln_g, conv_w, conv_b, lru_wa, lru_ba, lru_wi, lru_bi, lru_lambda, w_pa, w_pr, w_o, w_rg, b_rg, w_re, b_re, w_e_gate, w_e_up, w_e_down):
    bp, tp, _ = x_prompt.shape
    bs, ts, _ = x_sample.shape
    depth = w_in.shape[0]
    n_pool = cache_k.shape[1]
    n_pages = page_table.shape[1]
    past_len = n_pages * PAGE_SIZE
    t_p = bp * tp
    t_s = bs * ts
    t_all = t_p + t_s
    blk = ATTN_BLOCK
    nblk = tp // blk
    assert tp % blk == 0 and tp % TIME_TILE == 0 and t_p % TOKEN_TILE == 0
    assert t_s % SUBLANES == 0 and t_p % min(TOKEN_TILE, t_s) == 0
    log2e = math.log2(math.e)

    rows_c = -(-(bp + bs) // SUBLANES) * SUBLANES
    c_all = jnp.zeros((rows_c, D_MODEL), F32).at[:bp].set(c_prompt).at[bp:bp + bs].set(c_sample)
    mods = _mods(c_all, w_mod, b_mod)
    mods_p = mods[:, :bp].reshape(depth, bp, 1, 6 * D_MODEL)
    mods_s = jnp.tile(mods[:, bp:bp + bs], (1, ts, 1))

    w_in_bf = w_in.astype(BF16)
    w_pa_bf, w_pr_bf, w_o_bf = w_pa.astype(BF16), w_pr.astype(BF16), w_o.astype(BF16)
    qg_row = jnp.tile(q_norm_g, (1, Q_COLS // HEAD_DIM)).reshape(depth, 1, Q_COLS)
    kg_row = jnp.tile(k_norm_g, (1, K_COLS // HEAD_DIM)).reshape(depth, 1, K_COLS)
    seg = jnp.arange(MXU_DIM) // HEAD_DIM
    chunk_p = ((seg[:, None] == seg[None, :]).astype(F32) / HEAD_DIM).astype(BF16)
    wr = jnp.concatenate([w_rg, w_re], axis=-1)
    wr = jnp.pad(wr, ((0, 0), (0, 0), (0, ROUTER_LANES - wr.shape[-1])))
    wr_hi = wr.astype(BF16)
    wr_lo = (wr - wr_hi.astype(F32)).astype(BF16)
    br = jnp.pad(jnp.concatenate([b_rg, b_re], axis=-1), ((0, 0), (0, ROUTER_LANES - N_GROUPS - N_EXPERTS)))
    br = br.reshape(depth, 1, ROUTER_LANES)
    wa_bd = jax.vmap(_block_diag)(lru_wa).astype(BF16)
    wi_bd = jax.vmap(_block_diag)(lru_wi).astype(BF16)
    ba = lru_ba.reshape(depth, 1, D_MODEL)
    bi = lru_bi.reshape(depth, 1, D_MODEL)
    sp = jax.nn.softplus(-lru_lambda.astype(F32)).reshape(depth, 1, D_MODEL)
    cb = conv_b.reshape(depth, 1, D_MODEL)
    lam_f = lambda a, b: jnp.exp(jnp.sum(a.astype(F32) * b.astype(F32), axis=-1))
    lam_base = lam_f(lam_q1, lam_k1) - lam_f(lam_q2, lam_k2)

    bias_diag = _bias_table(rel_bias, blk, 0) * log2e
    bias_prev = _bias_table(rel_bias, blk, blk) * log2e
    rb = rel_bias.astype(F32) * log2e
    bias_scal = jnp.concatenate([rb[NUM_BUCKETS - 1],
                                 jnp.max(rb, axis=0), jnp.max(rb, axis=0) - jnp.min(rb, axis=0)])

    page_rows = PAGE_SIZE * N_KV_HEADS
    n_new = -(-ts * N_KV_HEADS // (2 * SUBLANES)) * (2 * SUBLANES)
    keys_s = n_pages * page_rows + -(-n_new // LANES) * LANES
    n_pos = past_len + ts
    dist = jnp.arange(-(ts - 1), n_pos, dtype=jnp.int32)
    flipped = _bias_of_distance(rel_bias, dist)[:, ::-1]
    by_token = jnp.stack([flipped[:, ts - 1 - t:ts - 1 - t + n_pos] for t in range(ts)], axis=1)
    by_row = jnp.transpose(by_token.reshape(N_KV_HEADS, GQA_GROUP, ts, n_pos), (0, 2, 1, 3))
    own_head = (jnp.arange(N_KV_HEADS)[:, None, None, None, None]
                == jnp.arange(N_KV_HEADS)[None, None, None, None, :])
    bias_rows = jnp.where(own_head, by_row[..., None], MASK_VALUE)
    bias_rows = bias_rows.reshape(N_KV_HEADS * ts * GQA_GROUP, n_pos * N_KV_HEADS)
    bias_rows = jnp.pad(bias_rows, ((0, 0), (0, keys_s - n_pos * N_KV_HEADS)), constant_values=MASK_VALUE)
    bias_s = jnp.concatenate([bias_rows, bias_rows], axis=0)
    pt_flat = page_table.reshape(-1).astype(jnp.int32)
    cache_k_rows = cache_k.reshape(depth, n_pool, page_rows, LANES)
    cache_v_rows = cache_v.reshape(depth, n_pool, page_rows, LANES)

    xp = x_prompt.reshape(t_p, D_MODEL)
    xs = jnp.swapaxes(x_sample, 0, 1).reshape(t_s, D_MODEL)
    outs = [[] for _ in range(8)]
    for l in range(depth):
        lam_init = 0.8 - 0.6 * math.exp(-0.3 * l)
        lam = lam_base[l] + lam_init
        sg_col = subln_g[l].astype(F32).reshape(V_DIM, 1)
        sg_row = subln_g[l].astype(F32).reshape(1, V_DIM)
        scal_s = jnp.stack([lam, jnp.asarray(1.0 - lam_init, F32)]).astype(F32)
        scal_p = jnp.concatenate([scal_s, bias_scal])
        lp = (norm1_g[l].reshape(1, D_MODEL), w_in_bf, qg_row[l], kg_row[l], chunk_p)
        rg = (conv_w[l], cb[l], wa_bd[l], wi_bd[l], ba[l], bi[l], sp[l])
        mg = (norm2_g[l].reshape(1, D_MODEL), w_pa_bf[l], w_pr_bf[l], w_o_bf[l], wr_hi[l], wr_lo[l], br[l])

        q, k, kb, v, xr, yr, ga, gr, vt_ext = _in_proj(xp, mods_p[l], tp, ATTN_SCALE * log2e, blk, l, *lp)
        attn = _attn_prompt(scal_p, q, kb, vt_ext, bias_diag, bias_prev, sg_col, bp, tp)
        rnn, h_last = _rglru_prompt(xr, yr, *rg, bp, tp)
        x1p, u2p, recp, cnt = _merge(xp, attn, rnn, ga, gr, mods_p[l], tp,
                                     jnp.zeros((1, ROUTER_LANES), F32), *mg)
        outs[0].append(k.reshape(bp, tp, N_KV_HEADS, 2 * HEAD_DIM))
        outs[1].append(v.reshape(bp, tp, N_KV_HEADS, V_DIM))
        outs[2].append(h_last.reshape(bp, D_MODEL))
        outs[3].append(xr.reshape(bp, tp, D_MODEL)[:, tp - (CONV_W - 1):])

        q, k, kb, v, xr, yr, ga, gr = _in_proj(xs, mods_s[l], ts, ATTN_SCALE, None, l, *lp)
        q6 = q.reshape(ts, bs, N_KV_HEADS, GQA_GROUP, 2, HEAD_DIM)
        q6 = jnp.transpose(q6, (1, 4, 2, 0, 3, 5))
        zq = jnp.zeros_like(q6[:, 0])
        q_rows = jnp.stack([jnp.concatenate([q6[:, 0], zq], axis=-1),
                            jnp.concatenate([zq, q6[:, 1]], axis=-1)], axis=1)
        q_rows = q_rows.reshape(bs, 2 * N_KV_HEADS * ts * GQA_GROUP, 2 * HEAD_DIM)
        new_rows = lambda z: jnp.pad(jnp.swapaxes(z.reshape(ts, bs, N_KV_HEADS, LANES), 0, 1)
                                     .reshape(bs, ts * N_KV_HEADS, LANES),
                                     ((0, 0), (0, n_new - ts * N_KV_HEADS), (0, 0)))
        o_s = _attn_sample(pt_flat, jnp.full((1,), l, jnp.int32), scal_s, q_rows, new_rows(kb),
                           new_rows(v.astype(BF16)), bias_s, sg_row, cache_k_rows, cache_v_rows, n_pages)
        attn = jnp.transpose(o_s.reshape(bs, N_KV_HEADS, ts, GQA_GROUP, V_DIM), (2, 0, 1, 3, 4))
        attn = attn.reshape(t_s, N_HEADS * V_DIM).astype(BF16)
        buf = jnp.swapaxes(state_conv[l], 0, 1)
        rnn, h_last = _rglru_sample(xr, yr, buf, state_rnn[l], *rg, ts, past_len == 0)
        x1s, u2s, recs, cnt = _merge(xs, attn, rnn, ga, gr, mods_s[l], ts, cnt, *mg)
        to_bt = lambda z, n: jnp.swapaxes(z.reshape(ts, bs, n), 0, 1)
        outs[4].append(to_bt(k, K_COLS).reshape(bs, ts, N_KV_HEADS, 2 * HEAD_DIM))
        outs[5].append(to_bt(v, V_COLS).reshape(bs, ts, N_KV_HEADS, V_DIM))
        outs[6].append(h_last)
        xin = jnp.concatenate([state_conv[l], to_bt(xr, D_MODEL)], axis=1)
        outs[7].append(xin[:, -(CONV_W - 1):])

        pos_flat, te, valid, last_tile, n_sorted = _dispatch_plan(jnp.concatenate([recp, recs], axis=0), cnt)
        u_sorted = _dispatch(pos_flat, u2p, u2s, jnp.zeros((n_sorted, D_MODEL), F32) if l == 0 else u_sorted)
        y_sorted = _experts(te, valid, last_tile, jnp.full((1,), l, jnp.int32), u_sorted,
                            w_e_gate, w_e_up, w_e_down)
        xp = _combine(pos_flat, x1p, recp, y_sorted, mods_p[l], tp, 0, t_all)
        xs = _combine(pos_flat, x1s, recs, y_sorted, mods_s[l], ts, t_p, t_all)

    y_sample = jnp.swapaxes(xs.reshape(ts, bs, D_MODEL), 0, 1)
    stack = lambda i: jnp.stack(outs[i])
    return (xp.reshape(bp, tp, D_MODEL), y_sample, stack(0), stack(1), stack(2), stack(3),
            stack(4), stack(5), stack(6), stack(7))
```

```python
import functools
import math

import jax
import jax.numpy as jnp
from jax import lax
from jax.experimental import pallas as pl
from jax.experimental.pallas import tpu as pltpu

F32 = jnp.float32
BF16 = jnp.bfloat16

D_MODEL = 1024
HEAD_DIM = 64
N_HEADS = 8
N_KV_HEADS = 4
GQA_GROUP = 2
V_DIM = 128
ATTN_SCALE = HEAD_DIM ** -0.5
MASK_VALUE = -1e30
NUM_BUCKETS = 32
MAX_EXACT = 16
MAX_DISTANCE = 128
N_RNN_BLOCKS = 16
RNN_BLOCK = 64
CONV_W = 4
LRU_C = 8.0
N_GROUPS = 4
EXPERTS_PER_GROUP = 8
N_EXPERTS = 32
D_EXPERT = 512
NORM_EPS = 1e-6
PAGE_SIZE = 128
Q_COLS = 1024
K_COLS = 512
V_COLS = 512
D_IN = 6144

LANES = 128
SUBLANES = 8
MXU_DIM = 256
VMEM_LIMIT_BYTES = 56 * 1024 * 1024

TOKEN_TILE = 256
MERGE_TILE = 512
ATTN_BLOCK = 512
FAR_CHAIN = 4
TIME_TILE = 512
EXPERT_TILE = 512
ROUTER_LANES = 128
ONES_ROWS = 16

_NT = (((1,), (1,)), ((), ()))


def _cparams(sem):
    return pltpu.CompilerParams(dimension_semantics=sem, vmem_limit_bytes=VMEM_LIMIT_BYTES)


def _dot(a, b):
    return jnp.dot(a, b, preferred_element_type=F32)


def _resident(shape):
    n = len(shape)
    return pl.BlockSpec(shape, lambda *_: (0,) * n)


_MODS_COLS = 1536


def _mods_body(c_ref, w_ref, b_ref, o_ref):
    c = c_ref[...]
    s = c * jax.nn.sigmoid(c)
    o_ref[...] = jnp.dot(s, w_ref[...], precision=lax.Precision.HIGHEST,
                         preferred_element_type=F32) + b_ref[...]


def _mods(c_all, w_mod, b_mod):
    rows = c_all.shape[0]
    depth, _, n = w_mod.shape
    return pl.pallas_call(
        _mods_body,
        grid=(depth, n // _MODS_COLS),
        in_specs=[pl.BlockSpec((rows, D_MODEL), lambda l, j: (0, 0)),
                  pl.BlockSpec((None, D_MODEL, _MODS_COLS), lambda l, j: (l, 0, j)),
                  pl.BlockSpec((None, 1, _MODS_COLS), lambda l, j: (l, 0, j))],
        out_specs=pl.BlockSpec((None, rows, _MODS_COLS), lambda l, j: (l, 0, j)),
        out_shape=jax.ShapeDtypeStruct((depth, rows, n), F32),
        compiler_params=_cparams(("arbitrary", "arbitrary")),
        name="mods",
    )(c_all, w_mod, b_mod.reshape(depth, 1, n))


def _rms_rows(x, g):
    ms = jnp.mean(x * x, axis=-1, keepdims=True)
    return x * lax.rsqrt(ms + NORM_EPS) * g


def _chunk_mean_square(z, p):
    parts = []
    for j in range(z.shape[1] // MXU_DIM):
        zz = z[:, j * MXU_DIM:(j + 1) * MXU_DIM]
        parts.append(_dot((zz * zz).astype(BF16), p))
    return jnp.concatenate(parts, axis=1)


def _in_proj_body(q_scale, x_ref, sc_ref, sh_ref, g_ref, w_ref, qg_ref, kg_ref, p_ref,
                  q_ref, k_ref, kb_ref, v_ref, xr_ref, yr_ref, ga_ref, gr_ref, vt_ref=None):
    u = (_rms_rows(x_ref[...], g_ref[...]) * (1.0 + sc_ref[...]) + sh_ref[...]).astype(BF16)
    p = p_ref[...]
    gates0 = Q_COLS + K_COLS + V_COLS
    gate_cols = lambda n: slice(gates0 + n * D_MODEL, gates0 + (n + 1) * D_MODEL)
    zq = _dot(u, w_ref[:, 0:Q_COLS])
    xr_ref[...] = _dot(u, w_ref[:, gate_cols(0)])
    q = zq * lax.rsqrt(_chunk_mean_square(zq, p) + NORM_EPS) * qg_ref[...]
    q_ref[...] = (q * q_scale).astype(BF16)
    zk = _dot(u, w_ref[:, Q_COLS:Q_COLS + K_COLS])
    yr_ref[...] = _dot(u, w_ref[:, gate_cols(1)])
    k = zk * lax.rsqrt(_chunk_mean_square(zk, p) + NORM_EPS) * kg_ref[...]
    kb_ref[...] = k.astype(BF16)
    c0 = Q_COLS + K_COLS
    v = _dot(u, w_ref[:, c0:c0 + V_COLS])
    ga_ref[...] = _dot(u, w_ref[:, gate_cols(2)])

    def store_heads(ref, z):
        if ref.ndim == 3:
            for h in range(N_KV_HEADS):
                ref[:, h, :] = z[:, h * LANES:(h + 1) * LANES]
        else:
            ref[...] = z
    store_heads(k_ref, k)
    gr_ref[...] = _dot(u, w_ref[:, gate_cols(3)])
    store_heads(v_ref, v)
    if vt_ref is not None:
        row = lax.broadcasted_iota(jnp.int32, (ONES_ROWS, v.shape[0]), 0)
        ones_rows = jnp.where(row == 0, 1.0, 0.0).astype(BF16)
        for h in range(N_KV_HEADS):
            vt_ref[h, 0:V_DIM, :] = v[:, h * V_DIM:(h + 1) * V_DIM].T.astype(BF16)
            vt_ref[h, V_DIM:V_DIM + ONES_ROWS, :] = ones_rows


def _mod_spec(mods, tile, rows_per_group, col_block):
    if mods.ndim == 3:
        tiles_per_group = rows_per_group // tile
        return pl.BlockSpec((None, 1, D_MODEL), lambda i: (i // tiles_per_group, 0, col_block))
    return pl.BlockSpec((tile, D_MODEL), lambda i: (i, col_block))


def _in_proj(x, mods, rows_per_group, q_scale, vt_block, layer, norm_g, w_in_bf, qg_row, kg_row, chunk_p):
    t = x.shape[0]
    tile = min(TOKEN_TILE, t)
    row = lambda n: pl.BlockSpec((tile, n), lambda i: (i, 0))
    outs = [(Q_COLS, BF16), (K_COLS, F32), (K_COLS, BF16), (V_COLS, F32),
            (D_MODEL, F32), (D_MODEL, F32), (D_MODEL, F32), (D_MODEL, F32)]
    out_specs = [row(n) for n, _ in outs]
    out_shape = [jax.ShapeDtypeStruct((t, n), dt) for n, dt in outs]
    if vt_block is not None:
        for n in (1, 3):
            out_specs[n] = pl.BlockSpec((tile, N_KV_HEADS, LANES), lambda i: (i, 0, 0))
            out_shape[n] = jax.ShapeDtypeStruct((t, N_KV_HEADS, LANES), F32)
        tiles_per_seq = rows_per_group // tile
        per_block = vt_block // tile
        vrows = V_DIM + ONES_ROWS
        out_specs.append(pl.BlockSpec(
            (N_KV_HEADS, None, vrows, tile),
            lambda i: (i // tiles_per_seq, (i % tiles_per_seq) // per_block, 0, (i % tiles_per_seq) % per_block)))
        out_shape.append(jax.ShapeDtypeStruct(
            (t // rows_per_group * N_KV_HEADS, rows_per_group // vt_block, vrows, vt_block), BF16))
    return pl.pallas_call(
        functools.partial(_in_proj_body, q_scale),
        grid=(t // tile,),
        in_specs=[row(D_MODEL),
                  _mod_spec(mods, tile, rows_per_group, 1),
                  _mod_spec(mods, tile, rows_per_group, 0),
                  _resident((1, D_MODEL)),
                  pl.BlockSpec((None, D_MODEL, D_IN), lambda i: (layer, 0, 0)),
                  _resident((1, Q_COLS)),
                  _resident((1, K_COLS)),
                  _resident((MXU_DIM, MXU_DIM))],
        out_specs=out_specs,
        out_shape=out_shape,
        compiler_params=_cparams(("arbitrary",)),
        name="in_proj",
    )(x, mods, mods, norm_g, w_in_bf, qg_row, kg_row, chunk_p)


def _bucket_of_distance(n):
    n = jnp.maximum(n, 0)
    nf = jnp.maximum(n, 1).astype(F32)
    large = MAX_EXACT + (jnp.log(nf / MAX_EXACT) / math.log(MAX_DISTANCE / MAX_EXACT)
                         * (NUM_BUCKETS - MAX_EXACT)).astype(jnp.int32)
    return jnp.where(n < MAX_EXACT, n, jnp.minimum(large, NUM_BUCKETS - 1))


def _bias_of_distance(rel_bias, n):
    b = rel_bias.astype(F32)[_bucket_of_distance(n)]
    b = jnp.where((n >= 0)[..., None], b, MASK_VALUE)
    return jnp.moveaxis(b, -1, 0)


def _bias_table(rel_bias, blk, offset):
    j = jnp.arange(2 * blk, dtype=jnp.int32)
    diff = jnp.where(j < blk, j, j - 2 * blk)
    return _bias_of_distance(rel_bias, diff + offset)[:, None, :]


_S_LAM, _S_OUT = 0, 1
_S_FAR = 2
_S_BMAX = _S_FAR + N_HEADS
_S_BRANGE = _S_BMAX + N_HEADS
BOUND_MARGIN = 1.01
SAFE_EXPONENT_SPAN = 100.0


def _attn_prompt_body(scal_ref, q_ref, k_ref, vt_ref, tdiag_ref, tprev_ref, sg_ref,
                      o_ref, qpad_sc, kmax_sc, bq_sc, m_sc, acc_sc, bdiag_ref, bprev_ref):
    h = pl.program_id(1)
    qi = pl.program_id(2)
    blk = q_ref.shape[0]
    ncombo = 2 * GQA_GROUP
    lam = scal_ref[_S_LAM]
    out_scale = scal_ref[_S_OUT]

    @pl.when(qi == 0)
    def _():
        kk = k_ref[...].astype(F32)
        r = lax.broadcasted_iota(jnp.int32, (LANES, LANES), 0) // HEAD_DIM
        c = lax.broadcasted_iota(jnp.int32, (LANES, LANES), 1) // HEAD_DIM
        same = jnp.where(r == c, 1.0, 0.0).astype(BF16)
        ksq = _dot((kk * kk).astype(BF16), same)
        kmax_sc[...] = jnp.sqrt(jnp.max(ksq, axis=0, keepdims=True))
        for table_ref, tile_ref in ((tdiag_ref, bdiag_ref), (tprev_ref, bprev_ref)):
            for g in range(GQA_GROUP):
                rows = jnp.broadcast_to(table_ref[g], (blk, 2 * blk))
                tile_ref[g] = pltpu.roll(rows, 0, 1, stride=1, stride_axis=0)[:, :blk]

    lane = lax.broadcasted_iota(jnp.int32, (blk, LANES), 1)
    ones = jnp.ones((SUBLANES, LANES), BF16)
    span = jnp.zeros((1, 1), F32)
    for g in range(GQA_GROUP):
        q2 = q_ref[:, g * LANES:(g + 1) * LANES]
        for c in range(2):
            idx = 2 * g + c
            qp = jnp.where((lane < HEAD_DIM) == (c == 0), q2, jnp.zeros_like(q2))
            qpad_sc[idx] = qp
            qf = qp.astype(F32)
            qsq = lax.dot_general(ones, (qf * qf).astype(BF16), _NT, preferred_element_type=F32)[0:1]
            kmax = jnp.max(kmax_sc[:, c * HEAD_DIM:(c + 1) * HEAD_DIM], axis=1, keepdims=True)
            bqk = BOUND_MARGIN * jnp.sqrt(qsq) * kmax
            bq_sc[idx] = bqk + scal_ref[_S_BMAX + h * GQA_GROUP + g]
            span = jnp.maximum(span, 2.0 * jnp.max(bqk, axis=1, keepdims=True)
                               + scal_ref[_S_BRANGE + h * GQA_GROUP + g])
    acc_sc[...] = jnp.zeros(acc_sc.shape, F32)
    far_bias = lambda g: scal_ref[_S_FAR + h * GQA_GROUP + g]
    n_far = jnp.maximum(qi - 1, 0)

    def k_block(kj):
        return k_ref[pl.ds(pl.multiple_of(kj * blk, blk), blk), :]

    def fast_chain(steps):
        work = [(s, idx) for s in range(len(steps)) for idx in range(ncombo)]
        k_blks = [k_block(kj) for kj, _, _ in steps]
        vts = [vt_ref[kj] for kj, _, _ in steps]
        qk = lambda w: lax.dot_general(k_blks[w[0]], qpad_sc[w[1]], _NT, preferred_element_type=F32)
        st_next = qk(work[0])
        for n, (s, idx) in enumerate(work):
            st = st_next
            if n + 1 < len(work):
                st_next = qk(work[n + 1])
            _, shift_of_group, tile_of_group = steps[s]
            g = idx // 2
            if tile_of_group is not None:
                st = st + tile_of_group(g)
            bound = bq_sc[idx] if shift_of_group is None else bq_sc[idx] - shift_of_group(g)
            acc_sc[idx] += _dot(vts[s], jnp.exp2(st - bound).astype(BF16))

    def exact_step(kj, shift_of_group, tile_of_group):
        k_blk = k_block(kj)
        vt = vt_ref[kj]
        for idx in range(ncombo):
            g = idx // 2
            st = lax.dot_general(k_blk, qpad_sc[idx], _NT, preferred_element_type=F32)
            if tile_of_group is not None:
                st = st + tile_of_group(g)
            shift = 0.0 if shift_of_group is None else shift_of_group(g)
            m_old = m_sc[idx]
            m_new = jnp.maximum(m_old, jnp.max(st, axis=0, keepdims=True) + shift)
            p = jnp.exp2(st - (m_new - shift)).astype(BF16)
            acc_sc[idx] = jnp.exp2(m_old - m_new) * acc_sc[idx] + _dot(vt, p)
            m_sc[idx] = m_new

    prev_tile = lambda g: bprev_ref[g]
    diag_tile = lambda g: bdiag_ref[g]

    def run_fast():
        far = lambda kj: (kj, far_bias, None)
        near = [(qi - 1, None, prev_tile), (qi, None, diag_tile)]
        n_quads = n_far // FAR_CHAIN

        def far_quad(j, carry):
            fast_chain([far(FAR_CHAIN * j + s) for s in range(FAR_CHAIN)])
            return carry
        lax.fori_loop(0, n_quads, far_quad, 0)
        left = n_far - FAR_CHAIN * n_quads

        @pl.when(left >= 2)
        def _():
            fast_chain([far(FAR_CHAIN * n_quads), far(FAR_CHAIN * n_quads + 1)])

        @pl.when(left % 2 == 1)
        def _():
            fast_chain([far(n_far - 1)] + near)

        @pl.when((left % 2 == 0) & (qi >= 1))
        def _():
            fast_chain(near)

        @pl.when(qi == 0)
        def _():
            fast_chain(near[1:])

    def run_exact():
        m_sc[...] = jnp.full(m_sc.shape, -jnp.inf, F32)

        def far_step(kj, carry):
            exact_step(kj, far_bias, None)
            return carry
        lax.fori_loop(0, n_far, far_step, 0)

        @pl.when(qi >= 1)
        def _():
            exact_step(qi - 1, None, prev_tile)
        exact_step(qi, None, diag_tile)

    lax.cond(span[0, 0] <= SAFE_EXPONENT_SPAN, run_fast, run_exact)

    for g in range(GQA_GROUP):
        a0 = acc_sc[2 * g]
        a1 = acc_sc[2 * g + 1]
        o = a0[:V_DIM] * (1.0 / a0[V_DIM:V_DIM + 1]) - a1[:V_DIM] * (lam / a1[V_DIM:V_DIM + 1])
        o = o * lax.rsqrt(jnp.mean(o * o, axis=0, keepdims=True) + NORM_EPS)
        o = o * sg_ref[...] * out_scale
        o_ref[:, g * V_DIM:(g + 1) * V_DIM] = o.T.astype(BF16)


def _attn_prompt(scal, q_bf, k_bf, vt_ext, bias_diag, bias_prev, sg_col, batch, seq):
    blk = ATTN_BLOCK
    nblk = seq // blk
    vrows = V_DIM + ONES_ROWS
    ncombo = 2 * GQA_GROUP
    grid_spec = pltpu.PrefetchScalarGridSpec(
        num_scalar_prefetch=1,
        grid=(batch, N_KV_HEADS, nblk),
        in_specs=[pl.BlockSpec((blk, GQA_GROUP * 2 * HEAD_DIM), lambda b, h, i, s: (b * nblk + i, h)),
                  pl.BlockSpec((seq, 2 * HEAD_DIM), lambda b, h, i, s: (b, h)),
                  pl.BlockSpec((None, nblk, vrows, blk), lambda b, h, i, s: (b * N_KV_HEADS + h, 0, 0, 0)),
                  pl.BlockSpec((GQA_GROUP, 1, 2 * blk), lambda b, h, i, s: (h, 0, 0)),
                  pl.BlockSpec((GQA_GROUP, 1, 2 * blk), lambda b, h, i, s: (h, 0, 0)),
                  pl.BlockSpec((V_DIM, 1), lambda b, h, i, s: (0, 0))],
        out_specs=pl.BlockSpec((blk, GQA_GROUP * V_DIM), lambda b, h, i, s: (b * nblk + i, h)),
        scratch_shapes=[pltpu.VMEM((ncombo, blk, LANES), BF16),
                        pltpu.VMEM((1, LANES), F32),
                        pltpu.VMEM((ncombo, 1, blk), F32),
                        pltpu.VMEM((ncombo, 1, blk), F32),
                        pltpu.VMEM((ncombo, vrows, blk), F32),
                        pltpu.VMEM((GQA_GROUP, blk, blk), F32),
                        pltpu.VMEM((GQA_GROUP, blk, blk), F32)],
    )
    return pl.pallas_call(
        _attn_prompt_body,
        grid_spec=grid_spec,
        out_shape=jax.ShapeDtypeStruct((batch * seq, N_HEADS * V_DIM), BF16),
        compiler_params=_cparams(("arbitrary", "arbitrary", "arbitrary")),
        name="attn_prompt",
    )(scal, q_bf, k_bf, vt_ext, bias_diag, bias_prev, sg_col)


def _attn_sample_body(n_pages, pt_ref, layer_ref, scal_ref, q_ref, knew_ref, vnew_ref, bias_ref, sg_ref, *refs):
    k_pages = refs[:n_pages]
    v_pages = refs[n_pages:2 * n_pages]
    o_ref, kb_sc, vb_sc = refs[2 * n_pages:]
    page_rows = k_pages[0].shape[2]
    past = n_pages * page_rows
    n_new = knew_ref.shape[1]
    lam = scal_ref[0]
    out_scale = scal_ref[1]
    half = q_ref.shape[1] // 2

    @pl.when(pl.program_id(0) == 0)
    def _():
        tail = kb_sc.shape[0] - past - n_new
        kb_sc[past + n_new:, :] = jnp.zeros((tail, LANES), BF16)
        vb_sc[past + n_new:, :] = jnp.zeros((tail, LANES), BF16)

    for j in range(n_pages):
        kb_sc[j * page_rows:(j + 1) * page_rows, :] = k_pages[j][0, 0].astype(BF16)
        vb_sc[j * page_rows:(j + 1) * page_rows, :] = v_pages[j][0, 0].astype(BF16)
    kb_sc[past:past + n_new, :] = knew_ref[0]
    vb_sc[past:past + n_new, :] = vnew_ref[0]
    s = lax.dot_general(q_ref[0], kb_sc[...], _NT, preferred_element_type=F32) + bias_ref[...]
    p = jnp.exp(s - jnp.max(s, axis=-1, keepdims=True))
    p = p / jnp.sum(p, axis=-1, keepdims=True)
    a = p[:half] - lam * p[half:]
    o = _dot(a.astype(BF16), vb_sc[...])
    o = o * lax.rsqrt(jnp.mean(o * o, axis=-1, keepdims=True) + NORM_EPS)
    o_ref[0] = o * sg_ref[...] * out_scale


def _attn_sample(page_table_flat, layer_idx, scal, q_rows, knew, vnew, bias_s, sg_row, cache_k, cache_v, n_pages):
    nb, rows, _ = q_rows.shape
    n_new = knew.shape[1]
    page_rows = cache_k.shape[2]
    keys = bias_s.shape[1]

    def page_spec(j):
        return pl.BlockSpec((1, 1, page_rows, LANES), lambda b, pt, l: (l[0], pt[b * n_pages + j], 0, 0))

    grid_spec = pltpu.PrefetchScalarGridSpec(
        num_scalar_prefetch=2,
        grid=(nb,),
        in_specs=[pl.BlockSpec(memory_space=pltpu.SMEM),
                  pl.BlockSpec((1, rows, LANES), lambda b, pt, l: (b, 0, 0)),
                  pl.BlockSpec((1, n_new, LANES), lambda b, pt, l: (b, 0, 0)),
                  pl.BlockSpec((1, n_new, LANES), lambda b, pt, l: (b, 0, 0)),
                  pl.BlockSpec((rows, keys), lambda b, pt, l: (0, 0)),
                  pl.BlockSpec((1, V_DIM), lambda b, pt, l: (0, 0))]
                 + [page_spec(j) for j in range(n_pages)] * 2,
        out_specs=pl.BlockSpec((1, rows // 2, V_DIM), lambda b, pt, l: (b, 0, 0)),
        scratch_shapes=[pltpu.VMEM((keys, LANES), BF16),
                        pltpu.VMEM((keys, LANES), BF16)],
    )
    return pl.pallas_call(
        functools.partial(_attn_sample_body, n_pages),
        grid_spec=grid_spec,
        out_shape=jax.ShapeDtypeStruct((nb, rows // 2, V_DIM), F32),
        compiler_params=_cparams(("arbitrary",)),
        name="attn_sample",
    )(page_table_flat, layer_idx, scal, q_rows, knew, vnew, bias_s, sg_row,
      *([cache_k] * n_pages), *([cache_v] * n_pages))


def _lru_coeffs(xc, wa_ref, wi_ref, ba_ref, bi_ref, sp_ref):
    xb = xc.astype(BF16)
    ra, ri = [], []
    for j in range(xc.shape[1] // MXU_DIM):
        cols = slice(j * MXU_DIM, (j + 1) * MXU_DIM)
        ra.append(_dot(xb[:, cols], wa_ref[j]))
        ri.append(_dot(xb[:, cols], wi_ref[j]))
    r = jax.nn.sigmoid(jnp.concatenate(ra, axis=1) + ba_ref[...])
    i = jax.nn.sigmoid(jnp.concatenate(ri, axis=1) + bi_ref[...])
    log_a = -LRU_C * r * sp_ref[...]
    a = jnp.exp(log_a)
    mult = jnp.sqrt(-jnp.tanh(log_a) * (a * a + 1.0))
    return a, mult, i


def _rglru_prompt_body(xr_ref, yr_ref, cw_ref, cb_ref, wa_ref, wi_ref, ba_ref, bi_ref, sp_ref,
                       rnn_ref, hl_ref, xe_sc, a_sc, b_sc, h_sc):
    i = pl.program_id(1)
    tt = xr_ref.shape[0]

    @pl.when(i == 0)
    def _():
        xe_sc[0:SUBLANES, :] = jnp.zeros((SUBLANES, D_MODEL), F32)
        h_sc[...] = jnp.zeros(h_sc.shape, F32)

    xe_sc[SUBLANES:SUBLANES + tt, :] = xr_ref[...]
    xc = cb_ref[...]
    for j in range(CONV_W):
        off = SUBLANES - (CONV_W - 1) + j
        xc = xc + xe_sc[off:off + tt, :] * cw_ref[j:j + 1, :]
    a, mult, gate_i = _lru_coeffs(xc, wa_ref, wi_ref, ba_ref, bi_ref, sp_ref)
    row = lax.broadcasted_iota(jnp.int32, (tt, 1), 0)
    mult = jnp.where((row == 0) & (i == 0), 1.0, mult)
    a_sc[...] = a
    b_sc[...] = mult * gate_i * xc

    def step(t, h):
        h = a_sc[pl.ds(t, 1), :] * h + b_sc[pl.ds(t, 1), :]
        b_sc[pl.ds(t, 1), :] = h
        return h

    h = lax.fori_loop(0, tt, step, h_sc[...], unroll=8)
    h_sc[...] = h
    hl_ref[...] = h
    rnn_ref[...] = (b_sc[...] * jax.nn.gelu(yr_ref[...])).astype(BF16)
    xe_sc[0:SUBLANES, :] = xe_sc[tt:tt + SUBLANES, :]


def _rglru_prompt(xr, yr, cw, cb, wa_bd, wi_bd, ba, bi, sp, batch, seq):
    tt = min(TIME_TILE, seq)
    nt = seq // tt
    row = pl.BlockSpec((tt, D_MODEL), lambda b, i: (b * nt + i, 0))
    nbd = D_MODEL // MXU_DIM
    return pl.pallas_call(
        _rglru_prompt_body,
        grid=(batch, nt),
        in_specs=[row, row,
                  pl.BlockSpec((CONV_W, D_MODEL), lambda b, i: (0, 0)),
                  pl.BlockSpec((1, D_MODEL), lambda b, i: (0, 0)),
                  pl.BlockSpec((nbd, MXU_DIM, MXU_DIM), lambda b, i: (0, 0, 0)),
                  pl.BlockSpec((nbd, MXU_DIM, MXU_DIM), lambda b, i: (0, 0, 0)),
                  pl.BlockSpec((1, D_MODEL), lambda b, i: (0, 0)),
                  pl.BlockSpec((1, D_MODEL), lambda b, i: (0, 0)),
                  pl.BlockSpec((1, D_MODEL), lambda b, i: (0, 0))],
        out_specs=[row, pl.BlockSpec((None, 1, D_MODEL), lambda b, i: (b, 0, 0))],
        out_shape=[jax.ShapeDtypeStruct((batch * seq, D_MODEL), BF16),
                   jax.ShapeDtypeStruct((batch, 1, D_MODEL), F32)],
        scratch_shapes=[pltpu.VMEM((tt + SUBLANES, D_MODEL), F32),
                        pltpu.VMEM((tt, D_MODEL), F32),
                        pltpu.VMEM((tt, D_MODEL), F32),
                        pltpu.VMEM((1, D_MODEL), F32)],
        compiler_params=_cparams(("arbitrary", "arbitrary")),
        name="rglru_prompt",
    )(xr, yr, cw, cb, wa_bd, wi_bd, ba, bi, sp)


def _rglru_sample_body(n_steps, first_is_start, xr_ref, yr_ref, buf_ref, h0_ref, cw_ref, cb_ref, wa_ref, wi_ref, ba_ref, bi_ref,
                       sp_ref, rnn_ref, hl_ref):
    nb = h0_ref.shape[0]
    taps = [buf_ref[j] for j in range(CONV_W - 1)] + [xr_ref[t * nb:(t + 1) * nb, :] for t in range(n_steps)]
    h = h0_ref[...]
    for t in range(n_steps):
        xc = cb_ref[...]
        for j in range(CONV_W):
            xc = xc + taps[t + j] * cw_ref[j:j + 1, :]
        a, mult, gate_i = _lru_coeffs(xc, wa_ref, wi_ref, ba_ref, bi_ref, sp_ref)
        if first_is_start and t == 0:
            mult = jnp.ones_like(mult)
        h = a * h + mult * gate_i * xc
        rnn_ref[t * nb:(t + 1) * nb, :] = (h * jax.nn.gelu(yr_ref[t * nb:(t + 1) * nb, :])).astype(BF16)
    hl_ref[...] = h


def _rglru_sample(xr, yr, buf, h0, cw, cb, wa_bd, wi_bd, ba, bi, sp, n_steps, first_is_start):
    t = xr.shape[0]
    return pl.pallas_call(
        functools.partial(_rglru_sample_body, n_steps, first_is_start),
        out_shape=[jax.ShapeDtypeStruct((t, D_MODEL), BF16),
                   jax.ShapeDtypeStruct(h0.shape, F32)],
        compiler_params=pltpu.CompilerParams(vmem_limit_bytes=VMEM_LIMIT_BYTES),
        name="rglru_sample",
    )(xr, yr, buf, h0, cw, cb, wa_bd, wi_bd, ba, bi, sp)


def _route(logits):
    lane = lax.broadcasted_iota(jnp.int32, logits.shape, 1).astype(F32)
    none = float(ROUTER_LANES)
    is_group = lane < N_GROUPS
    lg = jnp.where(is_group, logits, -jnp.inf)
    eg = jnp.exp(lg - jnp.max(lg, axis=-1, keepdims=True))
    gp = eg / jnp.sum(eg, axis=-1, keepdims=True)
    gw = jnp.max(gp, axis=-1, keepdims=True)
    gi = jnp.min(jnp.where(is_group & (gp == gw), lane, none), axis=-1, keepdims=True)
    first = N_GROUPS + gi * EXPERTS_PER_GROUP
    in_group = (lane >= first) & (lane < first + EXPERTS_PER_GROUP)
    le = jnp.where(in_group, logits, -jnp.inf)
    ee = jnp.exp(le - jnp.max(le, axis=-1, keepdims=True))
    pe = ee / jnp.sum(ee, axis=-1, keepdims=True)
    p1 = jnp.max(pe, axis=-1, keepdims=True)
    i1 = jnp.min(jnp.where(in_group & (pe == p1), lane, none), axis=-1, keepdims=True)
    rest = in_group & (lane != i1)
    pr = jnp.where(rest, pe, -1.0)
    p2 = jnp.max(pr, axis=-1, keepdims=True)
    i2 = jnp.min(jnp.where(rest & (pr == p2), lane, none), axis=-1, keepdims=True)
    den = p1 + p2
    w1 = gw * (p1 / den)
    w2 = gw * (p2 / den)
    rec = jnp.where(lane == 0.0, i1 - N_GROUPS, 0.0)
    rec = jnp.where(lane == 1.0, i2 - N_GROUPS, rec)
    rec = jnp.where(lane == 2.0, w1, rec)
    rec = jnp.where(lane == 3.0, w2, rec)
    return rec, jnp.where(lane == i1, 1.0, 0.0), jnp.where(lane == i2, 1.0, 0.0)


def _merge_body(x_ref, attn_ref, rnn_ref, ga_ref, gr_ref, g1_ref, sc2_ref, sh2_ref, n2g_ref,
                wpa_ref, wpr_ref, wo_ref, wrh_ref, wrl_ref, br_ref, ltri_ref, cnt_in_ref,
                x1_ref, u2_ref, rec_ref, cnt_ref, cnt_sc):
    tile = x_ref.shape[0]
    n_parts = 2 if tile % (2 * SUBLANES) == 0 else 1
    parts = [slice(p * tile // n_parts, (p + 1) * tile // n_parts) for p in range(n_parts)]
    per_row = lambda ref, rows: ref[rows, :] if ref.shape[0] == tile else ref[...]
    def proj(rows):
        return _dot(attn_ref[rows, :], wpa_ref[...]), _dot(rnn_ref[rows, :], wpr_ref[...])

    def mixed(rows, a, r):
        m = jax.nn.sigmoid(ga_ref[rows, :]) * a + jax.nn.sigmoid(gr_ref[rows, :]) * r
        return _dot(m.astype(BF16), wo_ref[...])

    def residual(rows, mx):
        x1 = x_ref[rows, :] + per_row(g1_ref, rows) * mx
        x1_ref[rows, :] = x1
        u2 = _rms_rows(x1, n2g_ref[...]) * (1.0 + per_row(sc2_ref, rows)) + per_row(sh2_ref, rows)
        u2_ref[rows, :] = u2
        u_hi = u2.astype(BF16)
        u_lo = (u2 - u_hi.astype(F32)).astype(BF16)
        return (_dot(u_hi, wrh_ref[...]) + _dot(u_lo, wrh_ref[...]) + _dot(u_hi, wrl_ref[...])) + br_ref[...]

    projs = [proj(parts[0])]
    mixes, logits = [], []
    for p in range(n_parts):
        if p + 1 < n_parts:
            projs.append(proj(parts[p + 1]))
        mixes.append(mixed(parts[p], *projs[p]))
        if p >= 1:
            logits.append(residual(parts[p - 1], mixes[p - 1]))
    logits.append(residual(parts[-1], mixes[-1]))
    rec, hot1, hot2 = _route(jnp.concatenate(logits, axis=0))

    @pl.when(pl.program_id(0) == 0)
    def _():
        cnt_sc[...] = cnt_in_ref[...]
    cnt = cnt_sc[...]
    tot1 = jnp.sum(hot1, axis=0, keepdims=True)
    tot2 = jnp.sum(hot2, axis=0, keepdims=True)
    before1 = _dot(ltri_ref[...], hot1.astype(BF16)) + cnt
    before2 = _dot(ltri_ref[...], hot2.astype(BF16)) + (cnt + tot1)
    rank1 = jnp.sum(hot1 * before1, axis=-1, keepdims=True)
    rank2 = jnp.sum(hot2 * before2, axis=-1, keepdims=True)
    cnt_sc[...] = cnt + tot1 + tot2
    cnt_ref[...] = cnt_sc[...]
    lane = lax.broadcasted_iota(jnp.int32, rec.shape, 1)
    rec = jnp.where(lane == 4, rank1, rec)
    rec_ref[...] = jnp.where(lane == 5, rank2, rec)


def _merge(x, attn, rnn, ga, gr, mods, rows_per_group, cnt_in, n2g, wpa, wpr, wo, wr_hi, wr_lo, br):
    t = x.shape[0]
    tile = min(MERGE_TILE, t)
    assert t % tile == 0 and (mods.ndim == 2 or rows_per_group % tile == 0)
    row = lambda n: pl.BlockSpec((tile, n), lambda i: (i, 0))
    sq = _resident((D_MODEL, D_MODEL))
    r = jnp.arange(tile)
    ltri = (r[None, :] < r[:, None]).astype(BF16)
    return pl.pallas_call(
        _merge_body,
        grid=(t // tile,),
        in_specs=[row(D_MODEL), row(D_MODEL), row(D_MODEL), row(D_MODEL), row(D_MODEL),
                  _mod_spec(mods, tile, rows_per_group, 2),
                  _mod_spec(mods, tile, rows_per_group, 4),
                  _mod_spec(mods, tile, rows_per_group, 3),
                  _resident((1, D_MODEL)), sq, sq, sq,
                  _resident((D_MODEL, ROUTER_LANES)), _resident((D_MODEL, ROUTER_LANES)),
                  _resident((1, ROUTER_LANES)), _resident((tile, tile)), _resident((1, ROUTER_LANES))],
        out_specs=[row(D_MODEL), row(D_MODEL), row(ROUTER_LANES), _resident((1, ROUTER_LANES))],
        out_shape=[jax.ShapeDtypeStruct((t, D_MODEL), F32),
                   jax.ShapeDtypeStruct((t, D_MODEL), F32),
                   jax.ShapeDtypeStruct((t, ROUTER_LANES), F32),
                   jax.ShapeDtypeStruct((1, ROUTER_LANES), F32)],
        scratch_shapes=[pltpu.VMEM((1, ROUTER_LANES), F32)],
        compiler_params=_cparams(("arbitrary",)),
        name="merge",
    )(x, attn, rnn, ga, gr, mods, mods, mods, n2g, wpa, wpr, wo, wr_hi, wr_lo, br, ltri, cnt_in)


ROW_UNROLL = 8


def _for_rows(n_rows, fn):
    def body(j, carry):
        for s in range(ROW_UNROLL):
            fn(j * ROW_UNROLL + s)
        return carry
    lax.fori_loop(0, n_rows // ROW_UNROLL, body, 0)


def _dispatch_body(t_all, n_first, pos_ref, ua_ref, ub_ref, init_hbm, us_hbm, sem):
    del init_hbm
    i = pl.program_id(0)

    def scatter_tile(u_ref, base):
        tile = u_ref.shape[0]

        def issue(r):
            for slot in range(2):
                pltpu.make_async_copy(u_ref.at[pl.ds(r, 1)],
                                      us_hbm.at[pl.ds(pos_ref[slot * t_all + base + r], 1)], sem.at[0]).start()
        _for_rows(tile, issue)
        for _ in range(2):
            pltpu.make_async_copy(u_ref, us_hbm.at[pl.ds(0, tile)], sem.at[0]).wait()

    @pl.when(i < n_first)
    def _():
        scatter_tile(ua_ref, i * ua_ref.shape[0])

    @pl.when(i >= n_first)
    def _():
        scatter_tile(ub_ref, n_first * ua_ref.shape[0] + (i - n_first) * ub_ref.shape[0])


def _dispatch(pos_flat, u_first, u_second, init):
    n_sorted_rows = init.shape[0]
    t_all = u_first.shape[0] + u_second.shape[0]
    tile_a = min(TOKEN_TILE, u_first.shape[0])
    tile_b = min(TOKEN_TILE, u_second.shape[0])
    n_a = u_first.shape[0] // tile_a
    n_b = u_second.shape[0] // tile_b
    assert n_a * tile_a == u_first.shape[0] and n_b * tile_b == u_second.shape[0]
    assert tile_a % ROW_UNROLL == 0 and tile_b % ROW_UNROLL == 0
    grid_spec = pltpu.PrefetchScalarGridSpec(
        num_scalar_prefetch=1,
        grid=(n_a + n_b,),
        in_specs=[pl.BlockSpec((tile_a, D_MODEL), lambda i, pos: (jnp.minimum(i, n_a - 1), 0)),
                  pl.BlockSpec((tile_b, D_MODEL), lambda i, pos: (jnp.maximum(i - n_a, 0), 0)),
                  pl.BlockSpec(memory_space=pl.ANY)],
        out_specs=pl.BlockSpec(memory_space=pl.ANY),
        scratch_shapes=[pltpu.SemaphoreType.DMA((1,))],
    )
    return pl.pallas_call(
        functools.partial(_dispatch_body, t_all, n_a),
        grid_spec=grid_spec,
        out_shape=jax.ShapeDtypeStruct((n_sorted_rows, D_MODEL), F32),
        input_output_aliases={3: 0},
        compiler_params=_cparams(("arbitrary",)),
        name="dispatch",
    )(pos_flat, u_first, u_second, init)


def _experts_body(te_ref, nv_ref, last_ref, layer_ref, run_ref, u_ref, wg_hbm, wu_hbm, wd_hbm, y_ref,
                  wg_sc, wu_sc, wd_sc, wg_buf, wu_buf, wd_buf, sem):
    del last_ref
    i = pl.program_id(0)
    n_tiles = pl.num_programs(0)
    layer = layer_ref[0]

    def weight_copies(expert, slot):
        return [pltpu.make_async_copy(hbm.at[layer, expert], buf.at[slot], sem.at[slot])
                for hbm, buf in ((wg_hbm, wg_buf), (wu_hbm, wu_buf), (wd_hbm, wd_buf))]

    @pl.when(i == 0)
    def _():
        for c in weight_copies(te_ref[0], 0):
            c.start()

    @pl.when(run_ref[i] == 1)
    def _():
        slot = run_ref[n_tiles + i]
        for c in weight_copies(te_ref[i], slot):
            c.wait()
        wg_sc[...] = wg_buf[slot].astype(BF16)
        wu_sc[...] = wu_buf[slot].astype(BF16)
        wd_sc[...] = wd_buf[slot].astype(BF16)
        following = run_ref[2 * n_tiles + i]

        @pl.when(following >= 0)
        def _():
            for c in weight_copies(following, 1 - slot):
                c.start()

    @pl.when(nv_ref[i] > 0)
    def _():
        u = u_ref[...].astype(BF16)
        hg = _dot(u, wg_sc[...])
        hid = (hg * jax.nn.sigmoid(hg)) * _dot(u, wu_sc[...])
        y_ref[...] = _dot(hid.astype(BF16), wd_sc[...])

    @pl.when(nv_ref[i] == 0)
    def _():
        y_ref[...] = jnp.zeros(y_ref.shape, F32)


def _experts(tile_expert, tile_valid, last_tile, layer_idx, runs, u_sorted, w_gate, w_up, w_down):
    tile = EXPERT_TILE
    n_tiles = tile_expert.shape[0]
    any_spec = pl.BlockSpec(memory_space=pl.ANY)
    grid_spec = pltpu.PrefetchScalarGridSpec(
        num_scalar_prefetch=5,
        grid=(n_tiles,),
        in_specs=[pl.BlockSpec((tile, D_MODEL), lambda i, te, nv, last, lyr, run: (jnp.minimum(i, last[0]), 0)),
                  any_spec, any_spec, any_spec],
        out_specs=pl.BlockSpec((tile, D_MODEL), lambda i, te, nv, last, lyr, run: (i, 0)),
        scratch_shapes=[pltpu.VMEM((D_MODEL, D_EXPERT), BF16),
                        pltpu.VMEM((D_MODEL, D_EXPERT), BF16),
                        pltpu.VMEM((D_EXPERT, D_MODEL), BF16),
                        pltpu.VMEM((2, D_MODEL, D_EXPERT), F32),
                        pltpu.VMEM((2, D_MODEL, D_EXPERT), F32),
                        pltpu.VMEM((2, D_EXPERT, D_MODEL), F32),
                        pltpu.SemaphoreType.DMA((2,))],
    )
    return pl.pallas_call(
        _experts_body,
        grid_spec=grid_spec,
        out_shape=jax.ShapeDtypeStruct(u_sorted.shape, F32),
        compiler_params=_cparams(("arbitrary",)),
        name="experts",
    )(tile_expert, tile_valid, last_tile, layer_idx, runs, u_sorted, w_gate, w_up, w_down)


def _dispatch_plan(rec_all, counts_row):
    t_all = rec_all.shape[0]
    tile = EXPERT_TILE
    n_tiles = -(-2 * t_all // tile) + N_EXPERTS
    counts = counts_row[0, N_GROUPS:N_GROUPS + N_EXPERTS].astype(jnp.int32)
    tiles_per_e = (counts + tile - 1) // tile
    tile_end = jnp.cumsum(tiles_per_e)
    tile_start = tile_end - tiles_per_e
    experts = jnp.arange(N_EXPERTS, dtype=jnp.int32)
    e = rec_all[:, 0:2].astype(jnp.int32)
    rank = rec_all[:, 4:6].astype(jnp.int32)
    row_start = jnp.sum(jnp.where(e[:, :, None] == experts, tile_start * tile, 0), axis=-1)
    pos_flat = (row_start + rank).T.reshape(-1)
    n_used = tile_end[-1]
    tile_ids = jnp.arange(n_tiles, dtype=jnp.int32)
    clamped = jnp.minimum(tile_ids, n_used - 1)
    te = jnp.sum((clamped[:, None] >= tile_end[None, :]).astype(jnp.int32), axis=-1)
    te = jnp.minimum(te, N_EXPERTS - 1)
    first = jnp.sum(jnp.where(te[:, None] == experts, tile_start, 0), axis=-1)
    cnt_t = jnp.sum(jnp.where(te[:, None] == experts, counts, 0), axis=-1)
    valid = jnp.clip(cnt_t - (tile_ids - first) * tile, 0, tile)
    valid = jnp.where(tile_ids < n_used, valid, 0).astype(jnp.int32)
    starts = (tile_ids < n_used) & (tile_ids == first)
    slot = (jnp.cumsum(starts.astype(jnp.int32)) - 1) % 2
    used = counts > 0
    later = jnp.where(used[None, :] & (experts[None, :] > experts[:, None]), experts[None, :], N_EXPERTS)
    following_of = jnp.min(later, axis=-1)
    following = jnp.sum(jnp.where(te[:, None] == experts, following_of, 0), axis=-1)
    following = jnp.where(following >= N_EXPERTS, -1, following)
    runs = jnp.concatenate([starts.astype(jnp.int32), slot, following]).astype(jnp.int32)
    return pos_flat, te, valid, (n_used - 1).reshape(1).astype(jnp.int32), runs, n_tiles * tile


def _combine_body(t_all, row_offset, pos_ref, x_ref, rec_ref, g2_ref, ys_hbm, o_ref, ybuf, sem):
    i = pl.program_id(0)
    n = pl.num_programs(0)
    tile = x_ref.shape[0]
    slot = i % 2

    def start_gather(tile_idx, buf):
        base = row_offset + tile_idx * tile

        def issue(r):
            for k in range(2):
                pltpu.make_async_copy(ys_hbm.at[pl.ds(pos_ref[k * t_all + base + r], 1)],
                                      ybuf.at[buf, k, pl.ds(r, 1)], sem.at[buf]).start()
        _for_rows(tile, issue)

    @pl.when(i == 0)
    def _():
        start_gather(0, 0)

    @pl.when(i + 1 < n)
    def _():
        start_gather(i + 1, 1 - slot)

    for k in range(2):
        pltpu.make_async_copy(ys_hbm.at[pl.ds(0, tile)], ybuf.at[slot, k], sem.at[slot]).wait()
    w1 = rec_ref[:, 2:3]
    w2 = rec_ref[:, 3:4]
    o_ref[...] = x_ref[...] + g2_ref[...] * (w1 * ybuf[slot, 0] + w2 * ybuf[slot, 1])


def _combine(pos_flat, x1, rec, y_sorted, mods, rows_per_group, row_offset, t_all):
    t = x1.shape[0]
    tile = min(TOKEN_TILE, t)
    assert tile % ROW_UNROLL == 0
    if mods.ndim == 3:
        tiles_per_group = rows_per_group // tile
        g2_spec = pl.BlockSpec((None, 1, D_MODEL), lambda i, pos: (i // tiles_per_group, 0, 5))
    else:
        g2_spec = pl.BlockSpec((tile, D_MODEL), lambda i, pos: (i, 5))
    grid_spec = pltpu.PrefetchScalarGridSpec(
        num_scalar_prefetch=1,
        grid=(t // tile,),
        in_specs=[pl.BlockSpec((tile, D_MODEL), lambda i, pos: (i, 0)),
                  pl.BlockSpec((tile, ROUTER_LANES), lambda i, pos: (i, 0)),
                  g2_spec,
                  pl.BlockSpec(memory_space=pl.ANY)],
        out_specs=pl.BlockSpec((tile, D_MODEL), lambda i, pos: (i, 0)),
        scratch_shapes=[pltpu.VMEM((2, 2, tile, D_MODEL), F32),
                        pltpu.SemaphoreType.DMA((2,))],
    )
    return pl.pallas_call(
        functools.partial(_combine_body, t_all, row_offset),
        grid_spec=grid_spec,
        out_shape=jax.ShapeDtypeStruct((t, D_MODEL), F32),
        compiler_params=_cparams(("arbitrary",)),
        name="combine",
    )(pos_flat, x1, rec, mods, y_sorted)


def _block_diag(w):
    per = MXU_DIM // RNN_BLOCK
    w4 = w.reshape(N_RNN_BLOCKS // per, per, RNN_BLOCK, RNN_BLOCK)
    eye = jnp.eye(per, dtype=w.dtype)
    return jnp.einsum("jmcd,mn->jmcnd", w4, eye).reshape(N_RNN_BLOCKS // per, MXU_DIM, MXU_DIM)


def kernel(x_prompt, x_sample, cache_k, cache_v, state_rnn, state_conv, page_table, c_prompt, c_sample, rel_bias, w_mod, b_mod, norm1_g, norm2_g, w_in, q_norm_g, k_norm_g, lam_q1, lam_k1, lam_q2, lam_k2, subln_g, conv_w, conv_b, lru_wa, lru_ba, lru_wi, lru_bi, lru_lambda, w_pa, w_pr, w_o, w_rg, b_rg, w_re, b_re, w_e_gate, w_e_up, w_e_down):
    bp, tp, _ = x_prompt.shape
    bs, ts, _ = x_sample.shape
    depth = w_in.shape[0]
    n_pool = cache_k.shape[1]
    n_pages = page_table.shape[1]
    past_len = n_pages * PAGE_SIZE
    t_p = bp * tp
    t_s = bs * ts
    t_all = t_p + t_s
    blk = ATTN_BLOCK
    nblk = tp // blk
    assert tp % blk == 0 and tp % TIME_TILE == 0 and t_p % TOKEN_TILE == 0
    assert t_s % SUBLANES == 0 and t_p % min(TOKEN_TILE, t_s) == 0
    log2e = math.log2(math.e)

    rows_c = -(-(bp + bs) // SUBLANES) * SUBLANES
    c_all = jnp.zeros((rows_c, D_MODEL), F32).at[:bp].set(c_prompt).at[bp:bp + bs].set(c_sample)
    mods = _mods(c_all, w_mod, b_mod)
    mods_p = mods[:, :bp].reshape(depth, bp, 1, 6 * D_MODEL)
    mods_s = jnp.tile(mods[:, bp:bp + bs], (1, ts, 1))

    w_in_bf = w_in.astype(BF16)
    w_pa_bf, w_pr_bf, w_o_bf = w_pa.astype(BF16), w_pr.astype(BF16), w_o.astype(BF16)
    qg_row = jnp.tile(q_norm_g, (1, Q_COLS // HEAD_DIM)).reshape(depth, 1, Q_COLS)
    kg_row = jnp.tile(k_norm_g, (1, K_COLS // HEAD_DIM)).reshape(depth, 1, K_COLS)
    seg = jnp.arange(MXU_DIM) // HEAD_DIM
    chunk_p = ((seg[:, None] == seg[None, :]).astype(F32) / HEAD_DIM).astype(BF16)
    wr = jnp.concatenate([w_rg, w_re], axis=-1)
    wr = jnp.pad(wr, ((0, 0), (0, 0), (0, ROUTER_LANES - wr.shape[-1])))
    wr_hi = wr.astype(BF16)
    wr_lo = (wr - wr_hi.astype(F32)).astype(BF16)
    br = jnp.pad(jnp.concatenate([b_rg, b_re], axis=-1), ((0, 0), (0, ROUTER_LANES - N_GROUPS - N_EXPERTS)))
    br = br.reshape(depth, 1, ROUTER_LANES)
    wa_bd = jax.vmap(_block_diag)(lru_wa).astype(BF16)
    wi_bd = jax.vmap(_block_diag)(lru_wi).astype(BF16)
    ba = lru_ba.reshape(depth, 1, D_MODEL)
    bi = lru_bi.reshape(depth, 1, D_MODEL)
    sp = jax.nn.softplus(-lru_lambda.astype(F32)).reshape(depth, 1, D_MODEL)
    cb = conv_b.reshape(depth, 1, D_MODEL)
    lam_f = lambda a, b: jnp.exp(jnp.sum(a.astype(F32) * b.astype(F32), axis=-1))
    lam_base = lam_f(lam_q1, lam_k1) - lam_f(lam_q2, lam_k2)

    bias_diag = _bias_table(rel_bias, blk, 0) * log2e
    bias_prev = _bias_table(rel_bias, blk, blk) * log2e
    rb = rel_bias.astype(F32) * log2e
    bias_scal = jnp.concatenate([rb[NUM_BUCKETS - 1],
                                 jnp.max(rb, axis=0), jnp.max(rb, axis=0) - jnp.min(rb, axis=0)])

    page_rows = PAGE_SIZE * N_KV_HEADS
    n_new = -(-ts * N_KV_HEADS // (2 * SUBLANES)) * (2 * SUBLANES)
    keys_s = n_pages * page_rows + -(-n_new // LANES) * LANES
    n_pos = past_len + ts
    dist = jnp.arange(-(ts - 1), n_pos, dtype=jnp.int32)
    flipped = _bias_of_distance(rel_bias, dist)[:, ::-1]
    by_token = jnp.stack([flipped[:, ts - 1 - t:ts - 1 - t + n_pos] for t in range(ts)], axis=1)
    by_row = jnp.transpose(by_token.reshape(N_KV_HEADS, GQA_GROUP, ts, n_pos), (0, 2, 1, 3))
    own_head = (jnp.arange(N_KV_HEADS)[:, None, None, None, None]
                == jnp.arange(N_KV_HEADS)[None, None, None, None, :])
    bias_rows = jnp.where(own_head, by_row[..., None], MASK_VALUE)
    bias_rows = bias_rows.reshape(N_KV_HEADS * ts * GQA_GROUP, n_pos * N_KV_HEADS)
    bias_rows = jnp.pad(bias_rows, ((0, 0), (0, keys_s - n_pos * N_KV_HEADS)), constant_values=MASK_VALUE)
    bias_s = jnp.concatenate([bias_rows, bias_rows], axis=0)
    pt_flat = page_table.reshape(-1).astype(jnp.int32)
    cache_k_rows = cache_k.reshape(depth, n_pool, page_rows, LANES)
    cache_v_rows = cache_v.reshape(depth, n_pool, page_rows, LANES)

    xp = x_prompt.reshape(t_p, D_MODEL)
    xs = jnp.swapaxes(x_sample, 0, 1).reshape(t_s, D_MODEL)
    outs = [[] for _ in range(8)]
    for l in range(depth):
        lam_init = 0.8 - 0.6 * math.exp(-0.3 * l)
        lam = lam_base[l] + lam_init
        sg_col = subln_g[l].astype(F32).reshape(V_DIM, 1)
        sg_row = subln_g[l].astype(F32).reshape(1, V_DIM)
        scal_s = jnp.stack([lam, jnp.asarray(1.0 - lam_init, F32)]).astype(F32)
        scal_p = jnp.concatenate([scal_s, bias_scal])
        lp = (norm1_g[l].reshape(1, D_MODEL), w_in_bf, qg_row[l], kg_row[l], chunk_p)
        rg = (conv_w[l], cb[l], wa_bd[l], wi_bd[l], ba[l], bi[l], sp[l])
        mg = (norm2_g[l].reshape(1, D_MODEL), w_pa_bf[l], w_pr_bf[l], w_o_bf[l], wr_hi[l], wr_lo[l], br[l])

        q, k, kb, v, xr, yr, ga, gr, vt_ext = _in_proj(xp, mods_p[l], tp, ATTN_SCALE * log2e, blk, l, *lp)
        attn = _attn_prompt(scal_p, q, kb, vt_ext, bias_diag, bias_prev, sg_col, bp, tp)
        rnn, h_last = _rglru_prompt(xr, yr, *rg, bp, tp)
        x1p, u2p, recp, cnt = _merge(xp, attn, rnn, ga, gr, mods_p[l], tp,
                                     jnp.zeros((1, ROUTER_LANES), F32), *mg)
        outs[0].append(k.reshape(bp, tp, N_KV_HEADS, 2 * HEAD_DIM))
        outs[1].append(v.reshape(bp, tp, N_KV_HEADS, V_DIM))
        outs[2].append(h_last.reshape(bp, D_MODEL))
        outs[3].append(xr.reshape(bp, tp, D_MODEL)[:, tp - (CONV_W - 1):])

        q, k, kb, v, xr, yr, ga, gr = _in_proj(xs, mods_s[l], ts, ATTN_SCALE, None, l, *lp)
        q6 = q.reshape(ts, bs, N_KV_HEADS, GQA_GROUP, 2, HEAD_DIM)
        q6 = jnp.transpose(q6, (1, 4, 2, 0, 3, 5))
        zq = jnp.zeros_like(q6[:, 0])
        q_rows = jnp.stack([jnp.concatenate([q6[:, 0], zq], axis=-1),
                            jnp.concatenate([zq, q6[:, 1]], axis=-1)], axis=1)
        q_rows = q_rows.reshape(bs, 2 * N_KV_HEADS * ts * GQA_GROUP, 2 * HEAD_DIM)
        new_rows = lambda z: jnp.pad(jnp.swapaxes(z.reshape(ts, bs, N_KV_HEADS, LANES), 0, 1)
                                     .reshape(bs, ts * N_KV_HEADS, LANES),
                                     ((0, 0), (0, n_new - ts * N_KV_HEADS), (0, 0)))
        o_s = _attn_sample(pt_flat, jnp.full((1,), l, jnp.int32), scal_s, q_rows, new_rows(kb),
                           new_rows(v.astype(BF16)), bias_s, sg_row, cache_k_rows, cache_v_rows, n_pages)
        attn = jnp.transpose(o_s.reshape(bs, N_KV_HEADS, ts, GQA_GROUP, V_DIM), (2, 0, 1, 3, 4))
        attn = attn.reshape(t_s, N_HEADS * V_DIM).astype(BF16)
        buf = jnp.swapaxes(state_conv[l], 0, 1)
        rnn, h_last = _rglru_sample(xr, yr, buf, state_rnn[l], *rg, ts, past_len == 0)
        x1s, u2s, recs, cnt = _merge(xs, attn, rnn, ga, gr, mods_s[l], ts, cnt, *mg)
        to_bt = lambda z, n: jnp.swapaxes(z.reshape(ts, bs, n), 0, 1)
        outs[4].append(to_bt(k, K_COLS).reshape(bs, ts, N_KV_HEADS, 2 * HEAD_DIM))
        outs[5].append(to_bt(v, V_COLS).reshape(bs, ts, N_KV_HEADS, V_DIM))
        outs[6].append(h_last)
        xin = jnp.concatenate([state_conv[l], to_bt(xr, D_MODEL)], axis=1)
        outs[7].append(xin[:, -(CONV_W - 1):])

        pos_flat, te, valid, last_tile, runs, n_sorted = _dispatch_plan(jnp.concatenate([recp, recs], axis=0), cnt)
        u_sorted = _dispatch(pos_flat, u2p, u2s, jnp.zeros((n_sorted, D_MODEL), F32) if l == 0 else u_sorted)
        y_sorted = _experts(te, valid, last_tile, jnp.full((1,), l, jnp.int32), runs, u_sorted,
                            w_e_gate, w_e_up, w_e_down)
        xp = _combine(pos_flat, x1p, recp, y_sorted, mods_p[l], tp, 0, t_all)
        xs = _combine(pos_flat, x1s, recs, y_sorted, mods_s[l], ts, t_p, t_all)

    y_sample = jnp.swapaxes(xs.reshape(ts, bs, D_MODEL), 0, 1)
    stack = lambda i: jnp.stack(outs[i])
    return (xp.reshape(bp, tp, D_MODEL), y_sample, stack(0), stack(1), stack(2), stack(3),
            stack(4), stack(5), stack(6), stack(7))
```

```python
import functools
import math

import jax
import jax.numpy as jnp
from jax import lax
from jax.experimental import pallas as pl
from jax.experimental.pallas import tpu as pltpu

F32 = jnp.float32
BF16 = jnp.bfloat16

D_MODEL = 1024
HEAD_DIM = 64
N_HEADS = 8
N_KV_HEADS = 4
GQA_GROUP = 2
V_DIM = 128
ATTN_SCALE = HEAD_DIM ** -0.5
MASK_VALUE = -1e30
NUM_BUCKETS = 32
MAX_EXACT = 16
MAX_DISTANCE = 128
N_RNN_BLOCKS = 16
RNN_BLOCK = 64
CONV_W = 4
LRU_C = 8.0
N_GROUPS = 4
EXPERTS_PER_GROUP = 8
N_EXPERTS = 32
D_EXPERT = 512
NORM_EPS = 1e-6
PAGE_SIZE = 128
Q_COLS = 1024
K_COLS = 512
V_COLS = 512
D_IN = 6144

LANES = 128
SUBLANES = 8
MXU_DIM = 256
VMEM_LIMIT_BYTES = 56 * 1024 * 1024

TOKEN_TILE = 256
MERGE_TILE = 512
ATTN_BLOCK = 512
FAR_CHAIN = 4
TIME_TILE = 512
EXPERT_TILE = 512
ROUTER_LANES = 128
ONES_ROWS = 16

_NT = (((1,), (1,)), ((), ()))


def _cparams(sem):
    return pltpu.CompilerParams(dimension_semantics=sem, vmem_limit_bytes=VMEM_LIMIT_BYTES)


def _dot(a, b):
    return jnp.dot(a, b, preferred_element_type=F32)


def _resident(shape):
    n = len(shape)
    return pl.BlockSpec(shape, lambda *_: (0,) * n)


_MODS_COLS = 1536


def _mods_body(c_ref, w_ref, b_ref, o_ref):
    c = c_ref[...]
    s = c * jax.nn.sigmoid(c)
    o_ref[...] = jnp.dot(s, w_ref[...], precision=lax.Precision.HIGHEST,
                         preferred_element_type=F32) + b_ref[...]


def _mods(c_all, w_mod, b_mod):
    rows = c_all.shape[0]
    depth, _, n = w_mod.shape
    return pl.pallas_call(
        _mods_body,
        grid=(depth, n // _MODS_COLS),
        in_specs=[pl.BlockSpec((rows, D_MODEL), lambda l, j: (0, 0)),
                  pl.BlockSpec((None, D_MODEL, _MODS_COLS), lambda l, j: (l, 0, j)),
                  pl.BlockSpec((None, 1, _MODS_COLS), lambda l, j: (l, 0, j))],
        out_specs=pl.BlockSpec((None, rows, _MODS_COLS), lambda l, j: (l, 0, j)),
        out_shape=jax.ShapeDtypeStruct((depth, rows, n), F32),
        compiler_params=_cparams(("arbitrary", "arbitrary")),
        name="mods",
    )(c_all, w_mod, b_mod.reshape(depth, 1, n))


def _rms_rows(x, g):
    ms = jnp.mean(x * x, axis=-1, keepdims=True)
    return x * lax.rsqrt(ms + NORM_EPS) * g


def _chunk_mean_square(z, p):
    parts = []
    for j in range(z.shape[1] // MXU_DIM):
        zz = z[:, j * MXU_DIM:(j + 1) * MXU_DIM]
        parts.append(_dot((zz * zz).astype(BF16), p))
    return jnp.concatenate(parts, axis=1)


def _in_proj_body(q_scale, x_ref, sc_ref, sh_ref, g_ref, w_ref, qg_ref, kg_ref, p_ref,
                  q_ref, k_ref, kb_ref, v_ref, xr_ref, yr_ref, ga_ref, gr_ref, vt_ref=None):
    u = (_rms_rows(x_ref[...], g_ref[...]) * (1.0 + sc_ref[...]) + sh_ref[...]).astype(BF16)
    p = p_ref[...]
    gates0 = Q_COLS + K_COLS + V_COLS
    gate_cols = lambda n: slice(gates0 + n * D_MODEL, gates0 + (n + 1) * D_MODEL)
    zq = _dot(u, w_ref[:, 0:Q_COLS])
    xr_ref[...] = _dot(u, w_ref[:, gate_cols(0)])
    q = zq * lax.rsqrt(_chunk_mean_square(zq, p) + NORM_EPS) * qg_ref[...]
    q_ref[...] = (q * q_scale).astype(BF16)
    zk = _dot(u, w_ref[:, Q_COLS:Q_COLS + K_COLS])
    yr_ref[...] = _dot(u, w_ref[:, gate_cols(1)])
    k = zk * lax.rsqrt(_chunk_mean_square(zk, p) + NORM_EPS) * kg_ref[...]
    kb_ref[...] = k.astype(BF16)
    c0 = Q_COLS + K_COLS
    v = _dot(u, w_ref[:, c0:c0 + V_COLS])
    ga_ref[...] = _dot(u, w_ref[:, gate_cols(2)])

    def store_heads(ref, z):
        if ref.ndim == 3:
            for h in range(N_KV_HEADS):
                ref[:, h, :] = z[:, h * LANES:(h + 1) * LANES]
        else:
            ref[...] = z
    store_heads(k_ref, k)
    gr_ref[...] = _dot(u, w_ref[:, gate_cols(3)])
    store_heads(v_ref, v)
    if vt_ref is not None:
        row = lax.broadcasted_iota(jnp.int32, (ONES_ROWS, v.shape[0]), 0)
        ones_rows = jnp.where(row == 0, 1.0, 0.0).astype(BF16)
        for h in range(N_KV_HEADS):
            vt_ref[h, 0:V_DIM, :] = v[:, h * V_DIM:(h + 1) * V_DIM].T.astype(BF16)
            vt_ref[h, V_DIM:V_DIM + ONES_ROWS, :] = ones_rows


def _mod_spec(mods, tile, rows_per_group, col_block):
    if mods.ndim == 3:
        tiles_per_group = rows_per_group // tile
        return pl.BlockSpec((None, 1, D_MODEL), lambda i: (i // tiles_per_group, 0, col_block))
    return pl.BlockSpec((tile, D_MODEL), lambda i: (i, col_block))


def _in_proj(x, mods, rows_per_group, q_scale, vt_block, layer, norm_g, w_in_bf, qg_row, kg_row, chunk_p):
    t = x.shape[0]
    tile = min(TOKEN_TILE, t)
    row = lambda n: pl.BlockSpec((tile, n), lambda i: (i, 0))
    outs = [(Q_COLS, BF16), (K_COLS, F32), (K_COLS, BF16), (V_COLS, F32),
            (D_MODEL, F32), (D_MODEL, F32), (D_MODEL, F32), (D_MODEL, F32)]
    out_specs = [row(n) for n, _ in outs]
    out_shape = [jax.ShapeDtypeStruct((t, n), dt) for n, dt in outs]
    if vt_block is not None:
        for n in (1, 3):
            out_specs[n] = pl.BlockSpec((tile, N_KV_HEADS, LANES), lambda i: (i, 0, 0))
            out_shape[n] = jax.ShapeDtypeStruct((t, N_KV_HEADS, LANES), F32)
        tiles_per_seq = rows_per_group // tile
        per_block = vt_block // tile
        vrows = V_DIM + ONES_ROWS
        out_specs.append(pl.BlockSpec(
            (N_KV_HEADS, None, vrows, tile),
            lambda i: (i // tiles_per_seq, (i % tiles_per_seq) // per_block, 0, (i % tiles_per_seq) % per_block)))
        out_shape.append(jax.ShapeDtypeStruct(
            (t // rows_per_group * N_KV_HEADS, rows_per_group // vt_block, vrows, vt_block), BF16))
    return pl.pallas_call(
        functools.partial(_in_proj_body, q_scale),
        grid=(t // tile,),
        in_specs=[row(D_MODEL),
                  _mod_spec(mods, tile, rows_per_group, 1),
                  _mod_spec(mods, tile, rows_per_group, 0),
                  _resident((1, D_MODEL)),
                  pl.BlockSpec((None, D_MODEL, D_IN), lambda i: (layer, 0, 0)),
                  _resident((1, Q_COLS)),
                  _resident((1, K_COLS)),
                  _resident((MXU_DIM, MXU_DIM))],
        out_specs=out_specs,
        out_shape=out_shape,
        compiler_params=_cparams(("arbitrary",)),
        name="in_proj",
    )(x, mods, mods, norm_g, w_in_bf, qg_row, kg_row, chunk_p)


def _bucket_of_distance(n):
    n = jnp.maximum(n, 0)
    nf = jnp.maximum(n, 1).astype(F32)
    large = MAX_EXACT + (jnp.log(nf / MAX_EXACT) / math.log(MAX_DISTANCE / MAX_EXACT)
                         * (NUM_BUCKETS - MAX_EXACT)).astype(jnp.int32)
    return jnp.where(n < MAX_EXACT, n, jnp.minimum(large, NUM_BUCKETS - 1))


def _bias_of_distance(rel_bias, n):
    b = rel_bias.astype(F32)[_bucket_of_distance(n)]
    b = jnp.where((n >= 0)[..., None], b, MASK_VALUE)
    return jnp.moveaxis(b, -1, 0)


def _bias_table(rel_bias, blk, offset):
    j = jnp.arange(2 * blk, dtype=jnp.int32)
    diff = jnp.where(j < blk, j, j - 2 * blk)
    return _bias_of_distance(rel_bias, diff + offset)[:, None, :]


_S_LAM, _S_OUT = 0, 1
_S_FAR = 2
_S_BMAX = _S_FAR + N_HEADS
_S_BRANGE = _S_BMAX + N_HEADS
BOUND_MARGIN = 1.01
SAFE_EXPONENT_SPAN = 100.0


def _attn_prompt_body(scal_ref, q_ref, k_ref, vt_ref, tdiag_ref, tprev_ref, sg_ref,
                      o_ref, qpad_sc, kmax_sc, bq_sc, m_sc, acc_sc, bdiag_ref, bprev_ref):
    h = pl.program_id(1)
    qi = pl.program_id(2)
    blk = q_ref.shape[0]
    ncombo = 2 * GQA_GROUP
    lam = scal_ref[_S_LAM]
    out_scale = scal_ref[_S_OUT]

    @pl.when(qi == 0)
    def _():
        kk = k_ref[...].astype(F32)
        r = lax.broadcasted_iota(jnp.int32, (LANES, LANES), 0) // HEAD_DIM
        c = lax.broadcasted_iota(jnp.int32, (LANES, LANES), 1) // HEAD_DIM
        same = jnp.where(r == c, 1.0, 0.0).astype(BF16)
        ksq = _dot((kk * kk).astype(BF16), same)
        kmax_sc[...] = jnp.sqrt(jnp.max(ksq, axis=0, keepdims=True))
        for table_ref, tile_ref in ((tdiag_ref, bdiag_ref), (tprev_ref, bprev_ref)):
            for g in range(GQA_GROUP):
                rows = jnp.broadcast_to(table_ref[g], (blk, 2 * blk))
                tile_ref[g] = pltpu.roll(rows, 0, 1, stride=1, stride_axis=0)[:, :blk]

    lane = lax.broadcasted_iota(jnp.int32, (blk, LANES), 1)
    ones = jnp.ones((SUBLANES, LANES), BF16)
    span = jnp.zeros((1, 1), F32)
    for g in range(GQA_GROUP):
        q2 = q_ref[:, g * LANES:(g + 1) * LANES]
        for c in range(2):
            idx = 2 * g + c
            qp = jnp.where((lane < HEAD_DIM) == (c == 0), q2, jnp.zeros_like(q2))
            qpad_sc[idx] = qp
            qf = qp.astype(F32)
            qsq = lax.dot_general(ones, (qf * qf).astype(BF16), _NT, preferred_element_type=F32)[0:1]
            kmax = jnp.max(kmax_sc[:, c * HEAD_DIM:(c + 1) * HEAD_DIM], axis=1, keepdims=True)
            bqk = BOUND_MARGIN * jnp.sqrt(qsq) * kmax
            bq_sc[idx] = bqk + scal_ref[_S_BMAX + h * GQA_GROUP + g]
            span = jnp.maximum(span, 2.0 * jnp.max(bqk, axis=1, keepdims=True)
                               + scal_ref[_S_BRANGE + h * GQA_GROUP + g])
    acc_sc[...] = jnp.zeros(acc_sc.shape, F32)
    far_bias = lambda g: scal_ref[_S_FAR + h * GQA_GROUP + g]
    n_far = jnp.maximum(qi - 1, 0)

    def k_block(kj):
        return k_ref[pl.ds(pl.multiple_of(kj * blk, blk), blk), :]

    def fast_chain(steps):
        work = [(s, idx) for s in range(len(steps)) for idx in range(ncombo)]
        k_blks = [k_block(kj) for kj, _, _ in steps]
        vts = [vt_ref[kj] for kj, _, _ in steps]
        qk = lambda w: lax.dot_general(k_blks[w[0]], qpad_sc[w[1]], _NT, preferred_element_type=F32)
        st_next = qk(work[0])
        for n, (s, idx) in enumerate(work):
            st = st_next
            if n + 1 < len(work):
                st_next = qk(work[n + 1])
            _, shift_of_group, tile_of_group = steps[s]
            g = idx // 2
            if tile_of_group is not None:
                st = st + tile_of_group(g)
            bound = bq_sc[idx] if shift_of_group is None else bq_sc[idx] - shift_of_group(g)
            acc_sc[idx] += _dot(vts[s], jnp.exp2(st - bound).astype(BF16))

    def exact_step(kj, shift_of_group, tile_of_group):
        k_blk = k_block(kj)
        vt = vt_ref[kj]
        for idx in range(ncombo):
            g = idx // 2
            st = lax.dot_general(k_blk, qpad_sc[idx], _NT, preferred_element_type=F32)
            if tile_of_group is not None:
                st = st + tile_of_group(g)
            shift = 0.0 if shift_of_group is None else shift_of_group(g)
            m_old = m_sc[idx]
            m_new = jnp.maximum(m_old, jnp.max(st, axis=0, keepdims=True) + shift)
            p = jnp.exp2(st - (m_new - shift)).astype(BF16)
            acc_sc[idx] = jnp.exp2(m_old - m_new) * acc_sc[idx] + _dot(vt, p)
            m_sc[idx] = m_new

    prev_tile = lambda g: bprev_ref[g]
    diag_tile = lambda g: bdiag_ref[g]

    def run_fast():
        far = lambda kj: (kj, far_bias, None)
        near = [(qi - 1, None, prev_tile), (qi, None, diag_tile)]
        n_quads = n_far // FAR_CHAIN

        def far_quad(j, carry):
            fast_chain([far(FAR_CHAIN * j + s) for s in range(FAR_CHAIN)])
            return carry
        lax.fori_loop(0, n_quads, far_quad, 0)
        left = n_far - FAR_CHAIN * n_quads

        @pl.when(left >= 2)
        def _():
            fast_chain([far(FAR_CHAIN * n_quads), far(FAR_CHAIN * n_quads + 1)])

        @pl.when(left % 2 == 1)
        def _():
            fast_chain([far(n_far - 1)] + near)

        @pl.when((left % 2 == 0) & (qi >= 1))
        def _():
            fast_chain(near)

        @pl.when(qi == 0)
        def _():
            fast_chain(near[1:])

    def run_exact():
        m_sc[...] = jnp.full(m_sc.shape, -jnp.inf, F32)

        def far_step(kj, carry):
            exact_step(kj, far_bias, None)
            return carry
        lax.fori_loop(0, n_far, far_step, 0)

        @pl.when(qi >= 1)
        def _():
            exact_step(qi - 1, None, prev_tile)
        exact_step(qi, None, diag_tile)

    lax.cond(span[0, 0] <= SAFE_EXPONENT_SPAN, run_fast, run_exact)

    for g in range(GQA_GROUP):
        a0 = acc_sc[2 * g]
        a1 = acc_sc[2 * g + 1]
        o = a0[:V_DIM] * (1.0 / a0[V_DIM:V_DIM + 1]) - a1[:V_DIM] * (lam / a1[V_DIM:V_DIM + 1])
        o = o * lax.rsqrt(jnp.mean(o * o, axis=0, keepdims=True) + NORM_EPS)
        o = o * sg_ref[...] * out_scale
        o_ref[:, g * V_DIM:(g + 1) * V_DIM] = o.T.astype(BF16)


def _attn_prompt(scal, q_bf, k_bf, vt_ext, bias_diag, bias_prev, sg_col, batch, seq):
    blk = ATTN_BLOCK
    nblk = seq // blk
    vrows = V_DIM + ONES_ROWS
    ncombo = 2 * GQA_GROUP
    grid_spec = pltpu.PrefetchScalarGridSpec(
        num_scalar_prefetch=1,
        grid=(batch, N_KV_HEADS, nblk),
        in_specs=[pl.BlockSpec((blk, GQA_GROUP * 2 * HEAD_DIM), lambda b, h, i, s: (b * nblk + i, h)),
                  pl.BlockSpec((seq, 2 * HEAD_DIM), lambda b, h, i, s: (b, h)),
                  pl.BlockSpec((None, nblk, vrows, blk), lambda b, h, i, s: (b * N_KV_HEADS + h, 0, 0, 0)),
                  pl.BlockSpec((GQA_GROUP, 1, 2 * blk), lambda b, h, i, s: (h, 0, 0)),
                  pl.BlockSpec((GQA_GROUP, 1, 2 * blk), lambda b, h, i, s: (h, 0, 0)),
                  pl.BlockSpec((V_DIM, 1), lambda b, h, i, s: (0, 0))],
        out_specs=pl.BlockSpec((blk, GQA_GROUP * V_DIM), lambda b, h, i, s: (b * nblk + i, h)),
        scratch_shapes=[pltpu.VMEM((ncombo, blk, LANES), BF16),
                        pltpu.VMEM((1, LANES), F32),
                        pltpu.VMEM((ncombo, 1, blk), F32),
                        pltpu.VMEM((ncombo, 1, blk), F32),
                        pltpu.VMEM((ncombo, vrows, blk), F32),
                        pltpu.VMEM((GQA_GROUP, blk, blk), F32),
                        pltpu.VMEM((GQA_GROUP, blk, blk), F32)],
    )
    return pl.pallas_call(
        _attn_prompt_body,
        grid_spec=grid_spec,
        out_shape=jax.ShapeDtypeStruct((batch * seq, N_HEADS * V_DIM), BF16),
        compiler_params=_cparams(("arbitrary", "arbitrary", "arbitrary")),
        name="attn_prompt",
    )(scal, q_bf, k_bf, vt_ext, bias_diag, bias_prev, sg_col)


def _attn_sample_body(n_pages, pt_ref, layer_ref, scal_ref, q_ref, knew_ref, vnew_ref, bias_ref, sg_ref, *refs):
    k_pages = refs[:n_pages]
    v_pages = refs[n_pages:2 * n_pages]
    o_ref, kb_sc, vb_sc = refs[2 * n_pages:]
    page_rows = k_pages[0].shape[2]
    past = n_pages * page_rows
    n_new = knew_ref.shape[1]
    lam = scal_ref[0]
    out_scale = scal_ref[1]
    half = q_ref.shape[1] // 2

    @pl.when(pl.program_id(0) == 0)
    def _():
        tail = kb_sc.shape[0] - past - n_new
        kb_sc[past + n_new:, :] = jnp.zeros((tail, LANES), BF16)
        vb_sc[past + n_new:, :] = jnp.zeros((tail, LANES), BF16)

    for j in range(n_pages):
        kb_sc[j * page_rows:(j + 1) * page_rows, :] = k_pages[j][0, 0].astype(BF16)
        vb_sc[j * page_rows:(j + 1) * page_rows, :] = v_pages[j][0, 0].astype(BF16)
    kb_sc[past:past + n_new, :] = knew_ref[0]
    vb_sc[past:past + n_new, :] = vnew_ref[0]
    s = lax.dot_general(q_ref[0], kb_sc[...], _NT, preferred_element_type=F32) + bias_ref[...]
    p = jnp.exp(s - jnp.max(s, axis=-1, keepdims=True))
    p = p / jnp.sum(p, axis=-1, keepdims=True)
    a = p[:half] - lam * p[half:]
    o = _dot(a.astype(BF16), vb_sc[...])
    o = o * lax.rsqrt(jnp.mean(o * o, axis=-1, keepdims=True) + NORM_EPS)
    o_ref[0] = o * sg_ref[...] * out_scale


def _attn_sample(page_table_flat, layer_idx, scal, q_rows, knew, vnew, bias_s, sg_row, cache_k, cache_v, n_pages):
    nb, rows, _ = q_rows.shape
    n_new = knew.shape[1]
    page_rows = cache_k.shape[2]
    keys = bias_s.shape[1]

    def page_spec(j):
        return pl.BlockSpec((1, 1, page_rows, LANES), lambda b, pt, l: (l[0], pt[b * n_pages + j], 0, 0))

    grid_spec = pltpu.PrefetchScalarGridSpec(
        num_scalar_prefetch=2,
        grid=(nb,),
        in_specs=[pl.BlockSpec(memory_space=pltpu.SMEM),
                  pl.BlockSpec((1, rows, LANES), lambda b, pt, l: (b, 0, 0)),
                  pl.BlockSpec((1, n_new, LANES), lambda b, pt, l: (b, 0, 0)),
                  pl.BlockSpec((1, n_new, LANES), lambda b, pt, l: (b, 0, 0)),
                  pl.BlockSpec((rows, keys), lambda b, pt, l: (0, 0)),
                  pl.BlockSpec((1, V_DIM), lambda b, pt, l: (0, 0))]
                 + [page_spec(j) for j in range(n_pages)] * 2,
        out_specs=pl.BlockSpec((1, rows // 2, V_DIM), lambda b, pt, l: (b, 0, 0)),
        scratch_shapes=[pltpu.VMEM((keys, LANES), BF16),
                        pltpu.VMEM((keys, LANES), BF16)],
    )
    return pl.pallas_call(
        functools.partial(_attn_sample_body, n_pages),
        grid_spec=grid_spec,
        out_shape=jax.ShapeDtypeStruct((nb, rows // 2, V_DIM), F32),
        compiler_params=_cparams(("arbitrary",)),
        name="attn_sample",
    )(page_table_flat, layer_idx, scal, q_rows, knew, vnew, bias_s, sg_row,
      *([cache_k] * n_pages), *([cache_v] * n_pages))


def _lru_coeffs(xc, wa_ref, wi_ref, ba_ref, bi_ref, sp_ref):
    xb = xc.astype(BF16)
    ra, ri = [], []
    for j in range(xc.shape[1] // MXU_DIM):
        cols = slice(j * MXU_DIM, (j + 1) * MXU_DIM)
        ra.append(_dot(xb[:, cols], wa_ref[j]))
        ri.append(_dot(xb[:, cols], wi_ref[j]))
    r = jax.nn.sigmoid(jnp.concatenate(ra, axis=1) + ba_ref[...])
    i = jax.nn.sigmoid(jnp.concatenate(ri, axis=1) + bi_ref[...])
    log_a = -LRU_C * r * sp_ref[...]
    a = jnp.exp(log_a)
    mult = jnp.sqrt(-jnp.tanh(log_a) * (a * a + 1.0))
    return a, mult, i


def _rglru_prompt_body(xr_ref, yr_ref, cw_ref, cb_ref, wa_ref, wi_ref, ba_ref, bi_ref, sp_ref,
                       rnn_ref, hl_ref, xe_sc, a_sc, b_sc, h_sc):
    i = pl.program_id(1)
    tt = xr_ref.shape[0]

    @pl.when(i == 0)
    def _():
        xe_sc[0:SUBLANES, :] = jnp.zeros((SUBLANES, D_MODEL), F32)
        h_sc[...] = jnp.zeros(h_sc.shape, F32)

    xe_sc[SUBLANES:SUBLANES + tt, :] = xr_ref[...]
    xc = cb_ref[...]
    for j in range(CONV_W):
        off = SUBLANES - (CONV_W - 1) + j
        xc = xc + xe_sc[off:off + tt, :] * cw_ref[j:j + 1, :]
    a, mult, gate_i = _lru_coeffs(xc, wa_ref, wi_ref, ba_ref, bi_ref, sp_ref)
    row = lax.broadcasted_iota(jnp.int32, (tt, 1), 0)
    mult = jnp.where((row == 0) & (i == 0), 1.0, mult)
    a_sc[...] = a
    b_sc[...] = mult * gate_i * xc

    def step(t, h):
        h = a_sc[pl.ds(t, 1), :] * h + b_sc[pl.ds(t, 1), :]
        b_sc[pl.ds(t, 1), :] = h
        return h

    h = lax.fori_loop(0, tt, step, h_sc[...], unroll=8)
    h_sc[...] = h
    hl_ref[...] = h
    rnn_ref[...] = (b_sc[...] * jax.nn.gelu(yr_ref[...])).astype(BF16)
    xe_sc[0:SUBLANES, :] = xe_sc[tt:tt + SUBLANES, :]


def _rglru_prompt(xr, yr, cw, cb, wa_bd, wi_bd, ba, bi, sp, batch, seq):
    tt = min(TIME_TILE, seq)
    nt = seq // tt
    row = pl.BlockSpec((tt, D_MODEL), lambda b, i: (b * nt + i, 0))
    nbd = D_MODEL // MXU_DIM
    return pl.pallas_call(
        _rglru_prompt_body,
        grid=(batch, nt),
        in_specs=[row, row,
                  pl.BlockSpec((CONV_W, D_MODEL), lambda b, i: (0, 0)),
                  pl.BlockSpec((1, D_MODEL), lambda b, i: (0, 0)),
                  pl.BlockSpec((nbd, MXU_DIM, MXU_DIM), lambda b, i: (0, 0, 0)),
                  pl.BlockSpec((nbd, MXU_DIM, MXU_DIM), lambda b, i: (0, 0, 0)),
                  pl.BlockSpec((1, D_MODEL), lambda b, i: (0, 0)),
                  pl.BlockSpec((1, D_MODEL), lambda b, i: (0, 0)),
                  pl.BlockSpec((1, D_MODEL), lambda b, i: (0, 0))],
        out_specs=[row, pl.BlockSpec((None, 1, D_MODEL), lambda b, i: (b, 0, 0))],
        out_shape=[jax.ShapeDtypeStruct((batch * seq, D_MODEL), BF16),
                   jax.ShapeDtypeStruct((batch, 1, D_MODEL), F32)],
        scratch_shapes=[pltpu.VMEM((tt + SUBLANES, D_MODEL), F32),
                        pltpu.VMEM((tt, D_MODEL), F32),
                        pltpu.VMEM((tt, D_MODEL), F32),
                        pltpu.VMEM((1, D_MODEL), F32)],
        compiler_params=_cparams(("arbitrary", "arbitrary")),
        name="rglru_prompt",
    )(xr, yr, cw, cb, wa_bd, wi_bd, ba, bi, sp)


def _rglru_sample_body(n_steps, first_is_start, xr_ref, yr_ref, buf_ref, h0_ref, cw_ref, cb_ref, wa_ref, wi_ref, ba_ref, bi_ref,
                       sp_ref, rnn_ref, hl_ref):
    nb = h0_ref.shape[0]
    taps = [buf_ref[j] for j in range(CONV_W - 1)] + [xr_ref[t * nb:(t + 1) * nb, :] for t in range(n_steps)]
    h = h0_ref[...]
    for t in range(n_steps):
        xc = cb_ref[...]
        for j in range(CONV_W):
            xc = xc + taps[t + j] * cw_ref[j:j + 1, :]
        a, mult, gate_i = _lru_coeffs(xc, wa_ref, wi_ref, ba_ref, bi_ref, sp_ref)
        if first_is_start and t == 0:
            mult = jnp.ones_like(mult)
        h = a * h + mult * gate_i * xc
        rnn_ref[t * nb:(t + 1) * nb, :] = (h * jax.nn.gelu(yr_ref[t * nb:(t + 1) * nb, :])).astype(BF16)
    hl_ref[...] = h


def _rglru_sample(xr, yr, buf, h0, cw, cb, wa_bd, wi_bd, ba, bi, sp, n_steps, first_is_start):
    t = xr.shape[0]
    return pl.pallas_call(
        functools.partial(_rglru_sample_body, n_steps, first_is_start),
        out_shape=[jax.ShapeDtypeStruct((t, D_MODEL), BF16),
                   jax.ShapeDtypeStruct(h0.shape, F32)],
        compiler_params=pltpu.CompilerParams(vmem_limit_bytes=VMEM_LIMIT_BYTES),
        name="rglru_sample",
    )(xr, yr, buf, h0, cw, cb, wa_bd, wi_bd, ba, bi, sp)


def _route(logits):
    lane = lax.broadcasted_iota(jnp.int32, logits.shape, 1).astype(F32)
    none = float(ROUTER_LANES)
    is_group = lane < N_GROUPS
    lg = jnp.where(is_group, logits, -jnp.inf)
    eg = jnp.exp(lg - jnp.max(lg, axis=-1, keepdims=True))
    gp = eg / jnp.sum(eg, axis=-1, keepdims=True)
    gw = jnp.max(gp, axis=-1, keepdims=True)
    gi = jnp.min(jnp.where(is_group & (gp == gw), lane, none), axis=-1, keepdims=True)
    first = N_GROUPS + gi * EXPERTS_PER_GROUP
    in_group = (lane >= first) & (lane < first + EXPERTS_PER_GROUP)
    le = jnp.where(in_group, logits, -jnp.inf)
    ee = jnp.exp(le - jnp.max(le, axis=-1, keepdims=True))
    pe = ee / jnp.sum(ee, axis=-1, keepdims=True)
    p1 = jnp.max(pe, axis=-1, keepdims=True)
    i1 = jnp.min(jnp.where(in_group & (pe == p1), lane, none), axis=-1, keepdims=True)
    rest = in_group & (lane != i1)
    pr = jnp.where(rest, pe, -1.0)
    p2 = jnp.max(pr, axis=-1, keepdims=True)
    i2 = jnp.min(jnp.where(rest & (pr == p2), lane, none), axis=-1, keepdims=True)
    den = p1 + p2
    w1 = gw * (p1 / den)
    w2 = gw * (p2 / den)
    rec = jnp.where(lane == 0.0, i1 - N_GROUPS, 0.0)
    rec = jnp.where(lane == 1.0, i2 - N_GROUPS, rec)
    rec = jnp.where(lane == 2.0, w1, rec)
    rec = jnp.where(lane == 3.0, w2, rec)
    return rec, jnp.where(lane == i1, 1.0, 0.0), jnp.where(lane == i2, 1.0, 0.0)


def _merge_body(x_ref, attn_ref, rnn_ref, ga_ref, gr_ref, g1_ref, sc2_ref, sh2_ref, n2g_ref,
                wpa_ref, wpr_ref, wo_ref, wrh_ref, wrl_ref, br_ref, ltri_ref, cnt_in_ref,
                x1_ref, u2_ref, rec_ref, cnt_ref, cnt_sc):
    tile = x_ref.shape[0]
    n_parts = 2 if tile % (2 * SUBLANES) == 0 else 1
    parts = [slice(p * tile // n_parts, (p + 1) * tile // n_parts) for p in range(n_parts)]
    per_row = lambda ref, rows: ref[rows, :] if ref.shape[0] == tile else ref[...]
    def proj(rows):
        return _dot(attn_ref[rows, :], wpa_ref[...]), _dot(rnn_ref[rows, :], wpr_ref[...])

    def mixed(rows, a, r):
        m = jax.nn.sigmoid(ga_ref[rows, :]) * a + jax.nn.sigmoid(gr_ref[rows, :]) * r
        return _dot(m.astype(BF16), wo_ref[...])

    def residual(rows, mx):
        x1 = x_ref[rows, :] + per_row(g1_ref, rows) * mx
        x1_ref[rows, :] = x1
        u2 = _rms_rows(x1, n2g_ref[...]) * (1.0 + per_row(sc2_ref, rows)) + per_row(sh2_ref, rows)
        u2_ref[rows, :] = u2
        u_hi = u2.astype(BF16)
        u_lo = (u2 - u_hi.astype(F32)).astype(BF16)
        return (_dot(u_hi, wrh_ref[...]) + _dot(u_lo, wrh_ref[...]) + _dot(u_hi, wrl_ref[...])) + br_ref[...]

    projs = [proj(parts[0])]
    mixes, logits = [], []
    for p in range(n_parts):
        if p + 1 < n_parts:
            projs.append(proj(parts[p + 1]))
        mixes.append(mixed(parts[p], *projs[p]))
        if p >= 1:
            logits.append(residual(parts[p - 1], mixes[p - 1]))
    logits.append(residual(parts[-1], mixes[-1]))
    rec, hot1, hot2 = _route(jnp.concatenate(logits, axis=0))

    @pl.when(pl.program_id(0) == 0)
    def _():
        cnt_sc[...] = cnt_in_ref[...]
    cnt = cnt_sc[...]
    tot1 = jnp.sum(hot1, axis=0, keepdims=True)
    tot2 = jnp.sum(hot2, axis=0, keepdims=True)
    before1 = _dot(ltri_ref[...], hot1.astype(BF16)) + cnt
    before2 = _dot(ltri_ref[...], hot2.astype(BF16)) + (cnt + tot1)
    rank1 = jnp.sum(hot1 * before1, axis=-1, keepdims=True)
    rank2 = jnp.sum(hot2 * before2, axis=-1, keepdims=True)
    cnt_sc[...] = cnt + tot1 + tot2
    cnt_ref[...] = cnt_sc[...]
    lane = lax.broadcasted_iota(jnp.int32, rec.shape, 1)
    rec = jnp.where(lane == 4, rank1, rec)
    rec_ref[...] = jnp.where(lane == 5, rank2, rec)


def _merge(x, attn, rnn, ga, gr, mods, rows_per_group, cnt_in, n2g, wpa, wpr, wo, wr_hi, wr_lo, br):
    t = x.shape[0]
    tile = min(MERGE_TILE, t)
    assert t % tile == 0 and (mods.ndim == 2 or rows_per_group % tile == 0)
    row = lambda n: pl.BlockSpec((tile, n), lambda i: (i, 0))
    sq = _resident((D_MODEL, D_MODEL))
    r = jnp.arange(tile)
    ltri = (r[None, :] < r[:, None]).astype(BF16)
    return pl.pallas_call(
        _merge_body,
        grid=(t // tile,),
        in_specs=[row(D_MODEL), row(D_MODEL), row(D_MODEL), row(D_MODEL), row(D_MODEL),
                  _mod_spec(mods, tile, rows_per_group, 2),
                  _mod_spec(mods, tile, rows_per_group, 4),
                  _mod_spec(mods, tile, rows_per_group, 3),
                  _resident((1, D_MODEL)), sq, sq, sq,
                  _resident((D_MODEL, ROUTER_LANES)), _resident((D_MODEL, ROUTER_LANES)),
                  _resident((1, ROUTER_LANES)), _resident((tile, tile)), _resident((1, ROUTER_LANES))],
        out_specs=[row(D_MODEL), row(D_MODEL), row(ROUTER_LANES), _resident((1, ROUTER_LANES))],
        out_shape=[jax.ShapeDtypeStruct((t, D_MODEL), F32),
                   jax.ShapeDtypeStruct((t, D_MODEL), F32),
                   jax.ShapeDtypeStruct((t, ROUTER_LANES), F32),
                   jax.ShapeDtypeStruct((1, ROUTER_LANES), F32)],
        scratch_shapes=[pltpu.VMEM((1, ROUTER_LANES), F32)],
        compiler_params=_cparams(("arbitrary",)),
        name="merge",
    )(x, attn, rnn, ga, gr, mods, mods, mods, n2g, wpa, wpr, wo, wr_hi, wr_lo, br, ltri, cnt_in)


ROW_UNROLL = 8


def _for_rows(n_rows, fn):
    def body(j, carry):
        for s in range(ROW_UNROLL):
            fn(j * ROW_UNROLL + s)
        return carry
    lax.fori_loop(0, n_rows // ROW_UNROLL, body, 0)


def _dispatch_body(t_all, n_first, pos_ref, ua_ref, ub_ref, init_hbm, us_hbm, sem):
    del init_hbm
    i = pl.program_id(0)

    def scatter_tile(u_ref, base):
        tile = u_ref.shape[0]

        def issue(r):
            for slot in range(2):
                pltpu.make_async_copy(u_ref.at[pl.ds(r, 1)],
                                      us_hbm.at[pl.ds(pos_ref[slot * t_all + base + r], 1)],
                                      sem.at[0]).start(priority=slot)
        _for_rows(tile, issue)
        for _ in range(2):
            pltpu.make_async_copy(u_ref, us_hbm.at[pl.ds(0, tile)], sem.at[0]).wait()

    @pl.when(i < n_first)
    def _():
        scatter_tile(ua_ref, i * ua_ref.shape[0])

    @pl.when(i >= n_first)
    def _():
        scatter_tile(ub_ref, n_first * ua_ref.shape[0] + (i - n_first) * ub_ref.shape[0])


def _dispatch(pos_flat, u_first, u_second, init):
    n_sorted_rows = init.shape[0]
    t_all = u_first.shape[0] + u_second.shape[0]
    tile_a = min(TOKEN_TILE, u_first.shape[0])
    tile_b = min(TOKEN_TILE, u_second.shape[0])
    n_a = u_first.shape[0] // tile_a
    n_b = u_second.shape[0] // tile_b
    assert n_a * tile_a == u_first.shape[0] and n_b * tile_b == u_second.shape[0]
    assert tile_a % ROW_UNROLL == 0 and tile_b % ROW_UNROLL == 0
    grid_spec = pltpu.PrefetchScalarGridSpec(
        num_scalar_prefetch=1,
        grid=(n_a + n_b,),
        in_specs=[pl.BlockSpec((tile_a, D_MODEL), lambda i, pos: (jnp.minimum(i, n_a - 1), 0)),
                  pl.BlockSpec((tile_b, D_MODEL), lambda i, pos: (jnp.maximum(i - n_a, 0), 0)),
                  pl.BlockSpec(memory_space=pl.ANY)],
        out_specs=pl.BlockSpec(memory_space=pl.ANY),
        scratch_shapes=[pltpu.SemaphoreType.DMA((1,))],
    )
    return pl.pallas_call(
        functools.partial(_dispatch_body, t_all, n_a),
        grid_spec=grid_spec,
        out_shape=jax.ShapeDtypeStruct((n_sorted_rows, D_MODEL), F32),
        input_output_aliases={3: 0},
        compiler_params=_cparams(("arbitrary",)),
        name="dispatch",
    )(pos_flat, u_first, u_second, init)


def _experts_body(te_ref, nv_ref, last_ref, layer_ref, run_ref, u_ref, wg_hbm, wu_hbm, wd_hbm, y_ref,
                  wg_sc, wu_sc, wd_sc, wg_buf, wu_buf, wd_buf, sem):
    del last_ref
    i = pl.program_id(0)
    n_tiles = pl.num_programs(0)
    layer = layer_ref[0]

    def weight_copies(expert, slot):
        return [pltpu.make_async_copy(hbm.at[layer, expert], buf.at[slot], sem.at[slot])
                for hbm, buf in ((wg_hbm, wg_buf), (wu_hbm, wu_buf), (wd_hbm, wd_buf))]

    @pl.when(i == 0)
    def _():
        for c in weight_copies(te_ref[0], 0):
            c.start()

    @pl.when(run_ref[i] == 1)
    def _():
        slot = run_ref[n_tiles + i]
        for c in weight_copies(te_ref[i], slot):
            c.wait()
        wg_sc[...] = wg_buf[slot].astype(BF16)
        wu_sc[...] = wu_buf[slot].astype(BF16)
        wd_sc[...] = wd_buf[slot].astype(BF16)
        following = run_ref[2 * n_tiles + i]

        @pl.when(following >= 0)
        def _():
            for c in weight_copies(following, 1 - slot):
                c.start()

    @pl.when(nv_ref[i] > 0)
    def _():
        u = u_ref[...].astype(BF16)
        hg = _dot(u, wg_sc[...])
        hid = (hg * jax.nn.sigmoid(hg)) * _dot(u, wu_sc[...])
        y_ref[...] = _dot(hid.astype(BF16), wd_sc[...])

    @pl.when(nv_ref[i] == 0)
    def _():
        y_ref[...] = jnp.zeros(y_ref.shape, F32)


def _experts(tile_expert, tile_valid, last_tile, layer_idx, runs, u_sorted, w_gate, w_up, w_down):
    tile = EXPERT_TILE
    n_tiles = tile_expert.shape[0]
    any_spec = pl.BlockSpec(memory_space=pl.ANY)
    grid_spec = pltpu.PrefetchScalarGridSpec(
        num_scalar_prefetch=5,
        grid=(n_tiles,),
        in_specs=[pl.BlockSpec((tile, D_MODEL), lambda i, te, nv, last, lyr, run: (jnp.minimum(i, last[0]), 0)),
                  any_spec, any_spec, any_spec],
        out_specs=pl.BlockSpec((tile, D_MODEL), lambda i, te, nv, last, lyr, run: (i, 0)),
        scratch_shapes=[pltpu.VMEM((D_MODEL, D_EXPERT), BF16),
                        pltpu.VMEM((D_MODEL, D_EXPERT), BF16),
                        pltpu.VMEM((D_EXPERT, D_MODEL), BF16),
                        pltpu.VMEM((2, D_MODEL, D_EXPERT), F32),
                        pltpu.VMEM((2, D_MODEL, D_EXPERT), F32),
                        pltpu.VMEM((2, D_EXPERT, D_MODEL), F32),
                        pltpu.SemaphoreType.DMA((2,))],
    )
    return pl.pallas_call(
        _experts_body,
        grid_spec=grid_spec,
        out_shape=jax.ShapeDtypeStruct(u_sorted.shape, F32),
        compiler_params=_cparams(("arbitrary",)),
        name="experts",
    )(tile_expert, tile_valid, last_tile, layer_idx, runs, u_sorted, w_gate, w_up, w_down)


def _dispatch_plan(rec_all, counts_row):
    t_all = rec_all.shape[0]
    tile = EXPERT_TILE
    n_tiles = -(-2 * t_all // tile) + N_EXPERTS
    counts = counts_row[0, N_GROUPS:N_GROUPS + N_EXPERTS].astype(jnp.int32)
    tiles_per_e = (counts + tile - 1) // tile
    tile_end = jnp.cumsum(tiles_per_e)
    tile_start = tile_end - tiles_per_e
    experts = jnp.arange(N_EXPERTS, dtype=jnp.int32)
    e = rec_all[:, 0:2].astype(jnp.int32)
    rank = rec_all[:, 4:6].astype(jnp.int32)
    row_start = jnp.sum(jnp.where(e[:, :, None] == experts, tile_start * tile, 0), axis=-1)
    pos_flat = (row_start + rank).T.reshape(-1)
    n_used = tile_end[-1]
    tile_ids = jnp.arange(n_tiles, dtype=jnp.int32)
    clamped = jnp.minimum(tile_ids, n_used - 1)
    te = jnp.sum((clamped[:, None] >= tile_end[None, :]).astype(jnp.int32), axis=-1)
    te = jnp.minimum(te, N_EXPERTS - 1)
    first = jnp.sum(jnp.where(te[:, None] == experts, tile_start, 0), axis=-1)
    cnt_t = jnp.sum(jnp.where(te[:, None] == experts, counts, 0), axis=-1)
    valid = jnp.clip(cnt_t - (tile_ids - first) * tile, 0, tile)
    valid = jnp.where(tile_ids < n_used, valid, 0).astype(jnp.int32)
    starts = (tile_ids < n_used) & (tile_ids == first)
    slot = (jnp.cumsum(starts.astype(jnp.int32)) - 1) % 2
    used = counts > 0
    later = jnp.where(used[None, :] & (experts[None, :] > experts[:, None]), experts[None, :], N_EXPERTS)
    following_of = jnp.min(later, axis=-1)
    following = jnp.sum(jnp.where(te[:, None] == experts, following_of, 0), axis=-1)
    following = jnp.where(following >= N_EXPERTS, -1, following)
    runs = jnp.concatenate([starts.astype(jnp.int32), slot, following]).astype(jnp.int32)
    return pos_flat, te, valid, (n_used - 1).reshape(1).astype(jnp.int32), runs, n_tiles * tile


def _combine_body(t_all, row_offset, pos_ref, x_ref, rec_ref, g2_ref, ys_hbm, o_ref, ybuf, sem):
    i = pl.program_id(0)
    n = pl.num_programs(0)
    tile = x_ref.shape[0]
    slot = i % 2

    def start_gather(tile_idx, buf):
        base = row_offset + tile_idx * tile

        def issue(r):
            for k in range(2):
                pltpu.make_async_copy(ys_hbm.at[pl.ds(pos_ref[k * t_all + base + r], 1)],
                                      ybuf.at[buf, k, pl.ds(r, 1)],
                                      sem.at[buf]).start(priority=k)
        _for_rows(tile, issue)

    @pl.when(i == 0)
    def _():
        start_gather(0, 0)

    @pl.when(i + 1 < n)
    def _():
        start_gather(i + 1, 1 - slot)

    for k in range(2):
        pltpu.make_async_copy(ys_hbm.at[pl.ds(0, tile)], ybuf.at[slot, k], sem.at[slot]).wait()
    w1 = rec_ref[:, 2:3]
    w2 = rec_ref[:, 3:4]
    o_ref[...] = x_ref[...] + g2_ref[...] * (w1 * ybuf[slot, 0] + w2 * ybuf[slot, 1])


def _combine(pos_flat, x1, rec, y_sorted, mods, rows_per_group, row_offset, t_all):
    t = x1.shape[0]
    tile = min(TOKEN_TILE, t)
    assert tile % ROW_UNROLL == 0
    if mods.ndim == 3:
        tiles_per_group = rows_per_group // tile
        g2_spec = pl.BlockSpec((None, 1, D_MODEL), lambda i, pos: (i // tiles_per_group, 0, 5))
    else:
        g2_spec = pl.BlockSpec((tile, D_MODEL), lambda i, pos: (i, 5))
    grid_spec = pltpu.PrefetchScalarGridSpec(
        num_scalar_prefetch=1,
        grid=(t // tile,),
        in_specs=[pl.BlockSpec((tile, D_MODEL), lambda i, pos: (i, 0)),
                  pl.BlockSpec((tile, ROUTER_LANES), lambda i, pos: (i, 0)),
                  g2_spec,
                  pl.BlockSpec(memory_space=pl.ANY)],
        out_specs=pl.BlockSpec((tile, D_MODEL), lambda i, pos: (i, 0)),
        scratch_shapes=[pltpu.VMEM((2, 2, tile, D_MODEL), F32),
                        pltpu.SemaphoreType.DMA((2,))],
    )
    return pl.pallas_call(
        functools.partial(_combine_body, t_all, row_offset),
        grid_spec=grid_spec,
        out_shape=jax.ShapeDtypeStruct((t, D_MODEL), F32),
        compiler_params=_cparams(("arbitrary",)),
        name="combine",
    )(pos_flat, x1, rec, mods, y_sorted)


def _block_diag(w):
    per = MXU_DIM // RNN_BLOCK
    w4 = w.reshape(N_RNN_BLOCKS // per, per, RNN_BLOCK, RNN_BLOCK)
    eye = jnp.eye(per, dtype=w.dtype)
    return jnp.einsum("jmcd,mn->jmcnd", w4, eye).reshape(N_RNN_BLOCKS // per, MXU_DIM, MXU_DIM)


def kernel(x_prompt, x_sample, cache_k, cache_v, state_rnn, state_conv, page_table, c_prompt, c_sample, rel_bias, w_mod, b_mod, norm1_g, norm2_g, w_in, q_norm_g, k_norm_g, lam_q1, lam_k1, lam_q2, lam_k2, subln_g, conv_w, conv_b, lru_wa, lru_ba, lru_wi, lru_bi, lru_lambda, w_pa, w_pr, w_o, w_rg, b_rg, w_re, b_re, w_e_gate, w_e_up, w_e_down):
    bp, tp, _ = x_prompt.shape
    bs, ts, _ = x_sample.shape
    depth = w_in.shape[0]
    n_pool = cache_k.shape[1]
    n_pages = page_table.shape[1]
    past_len = n_pages * PAGE_SIZE
    t_p = bp * tp
    t_s = bs * ts
    t_all = t_p + t_s
    blk = ATTN_BLOCK
    nblk = tp // blk
    assert tp % blk == 0 and tp % TIME_TILE == 0 and t_p % TOKEN_TILE == 0
    assert t_s % SUBLANES == 0 and t_p % min(TOKEN_TILE, t_s) == 0
    log2e = math.log2(math.e)

    rows_c = -(-(bp + bs) // SUBLANES) * SUBLANES
    c_all = jnp.zeros((rows_c, D_MODEL), F32).at[:bp].set(c_prompt).at[bp:bp + bs].set(c_sample)
    mods = _mods(c_all, w_mod, b_mod)
    mods_p = mods[:, :bp].reshape(depth, bp, 1, 6 * D_MODEL)
    mods_s = jnp.tile(mods[:, bp:bp + bs], (1, ts, 1))

    w_in_bf = w_in.astype(BF16)
    w_pa_bf, w_pr_bf, w_o_bf = w_pa.astype(BF16), w_pr.astype(BF16), w_o.astype(BF16)
    qg_row = jnp.tile(q_norm_g, (1, Q_COLS // HEAD_DIM)).reshape(depth, 1, Q_COLS)
    kg_row = jnp.tile(k_norm_g, (1, K_COLS // HEAD_DIM)).reshape(depth, 1, K_COLS)
    seg = jnp.arange(MXU_DIM) // HEAD_DIM
    chunk_p = ((seg[:, None] == seg[None, :]).astype(F32) / HEAD_DIM).astype(BF16)
    wr = jnp.concatenate([w_rg, w_re], axis=-1)
    wr = jnp.pad(wr, ((0, 0), (0, 0), (0, ROUTER_LANES - wr.shape[-1])))
    wr_hi = wr.astype(BF16)
    wr_lo = (wr - wr_hi.astype(F32)).astype(BF16)
    br = jnp.pad(jnp.concatenate([b_rg, b_re], axis=-1), ((0, 0), (0, ROUTER_LANES - N_GROUPS - N_EXPERTS)))
    br = br.reshape(depth, 1, ROUTER_LANES)
    wa_bd = jax.vmap(_block_diag)(lru_wa).astype(BF16)
    wi_bd = jax.vmap(_block_diag)(lru_wi).astype(BF16)
    ba = lru_ba.reshape(depth, 1, D_MODEL)
    bi = lru_bi.reshape(depth, 1, D_MODEL)
    sp = jax.nn.softplus(-lru_lambda.astype(F32)).reshape(depth, 1, D_MODEL)
    cb = conv_b.reshape(depth, 1, D_MODEL)
    lam_f = lambda a, b: jnp.exp(jnp.sum(a.astype(F32) * b.astype(F32), axis=-1))
    lam_base = lam_f(lam_q1, lam_k1) - lam_f(lam_q2, lam_k2)

    bias_diag = _bias_table(rel_bias, blk, 0) * log2e
    bias_prev = _bias_table(rel_bias, blk, blk) * log2e
    rb = rel_bias.astype(F32) * log2e
    bias_scal = jnp.concatenate([rb[NUM_BUCKETS - 1],
                                 jnp.max(rb, axis=0), jnp.max(rb, axis=0) - jnp.min(rb, axis=0)])

    page_rows = PAGE_SIZE * N_KV_HEADS
    n_new = -(-ts * N_KV_HEADS // (2 * SUBLANES)) * (2 * SUBLANES)
    keys_s = n_pages * page_rows + -(-n_new // LANES) * LANES
    n_pos = past_len + ts
    dist = jnp.arange(-(ts - 1), n_pos, dtype=jnp.int32)
    flipped = _bias_of_distance(rel_bias, dist)[:, ::-1]
    by_token = jnp.stack([flipped[:, ts - 1 - t:ts - 1 - t + n_pos] for t in range(ts)], axis=1)
    by_row = jnp.transpose(by_token.reshape(N_KV_HEADS, GQA_GROUP, ts, n_pos), (0, 2, 1, 3))
    own_head = (jnp.arange(N_KV_HEADS)[:, None, None, None, None]
                == jnp.arange(N_KV_HEADS)[None, None, None, None, :])
    bias_rows = jnp.where(own_head, by_row[..., None], MASK_VALUE)
    bias_rows = bias_rows.reshape(N_KV_HEADS * ts * GQA_GROUP, n_pos * N_KV_HEADS)
    bias_rows = jnp.pad(bias_rows, ((0, 0), (0, keys_s - n_pos * N_KV_HEADS)), constant_values=MASK_VALUE)
    bias_s = jnp.concatenate([bias_rows, bias_rows], axis=0)
    pt_flat = page_table.reshape(-1).astype(jnp.int32)
    cache_k_rows = cache_k.reshape(depth, n_pool, page_rows, LANES)
    cache_v_rows = cache_v.reshape(depth, n_pool, page_rows, LANES)

    xp = x_prompt.reshape(t_p, D_MODEL)
    xs = jnp.swapaxes(x_sample, 0, 1).reshape(t_s, D_MODEL)
    outs = [[] for _ in range(8)]
    for l in range(depth):
        lam_init = 0.8 - 0.6 * math.exp(-0.3 * l)
        lam = lam_base[l] + lam_init
        sg_col = subln_g[l].astype(F32).reshape(V_DIM, 1)
        sg_row = subln_g[l].astype(F32).reshape(1, V_DIM)
        scal_s = jnp.stack([lam, jnp.asarray(1.0 - lam_init, F32)]).astype(F32)
        scal_p = jnp.concatenate([scal_s, bias_scal])
        lp = (norm1_g[l].reshape(1, D_MODEL), w_in_bf, qg_row[l], kg_row[l], chunk_p)
        rg = (conv_w[l], cb[l], wa_bd[l], wi_bd[l], ba[l], bi[l], sp[l])
        mg = (norm2_g[l].reshape(1, D_MODEL), w_pa_bf[l], w_pr_bf[l], w_o_bf[l], wr_hi[l], wr_lo[l], br[l])

        q, k, kb, v, xr, yr, ga, gr, vt_ext = _in_proj(xp, mods_p[l], tp, ATTN_SCALE * log2e, blk, l, *lp)
        attn = _attn_prompt(scal_p, q, kb, vt_ext, bias_diag, bias_prev, sg_col, bp, tp)
        rnn, h_last = _rglru_prompt(xr, yr, *rg, bp, tp)
        x1p, u2p, recp, cnt = _merge(xp, attn, rnn, ga, gr, mods_p[l], tp,
                                     jnp.zeros((1, ROUTER_LANES), F32), *mg)
        outs[0].append(k.reshape(bp, tp, N_KV_HEADS, 2 * HEAD_DIM))
        outs[1].append(v.reshape(bp, tp, N_KV_HEADS, V_DIM))
        outs[2].append(h_last.reshape(bp, D_MODEL))
        outs[3].append(xr.reshape(bp, tp, D_MODEL)[:, tp - (CONV_W - 1):])

        q, k, kb, v, xr, yr, ga, gr = _in_proj(xs, mods_s[l], ts, ATTN_SCALE, None, l, *lp)
        q6 = q.reshape(ts, bs, N_KV_HEADS, GQA_GROUP, 2, HEAD_DIM)
        q6 = jnp.transpose(q6, (1, 4, 2, 0, 3, 5))
        zq = jnp.zeros_like(q6[:, 0])
        q_rows = jnp.stack([jnp.concatenate([q6[:, 0], zq], axis=-1),
                            jnp.concatenate([zq, q6[:, 1]], axis=-1)], axis=1)
        q_rows = q_rows.reshape(bs, 2 * N_KV_HEADS * ts * GQA_GROUP, 2 * HEAD_DIM)
        new_rows = lambda z: jnp.pad(jnp.swapaxes(z.reshape(ts, bs, N_KV_HEADS, LANES), 0, 1)
                                     .reshape(bs, ts * N_KV_HEADS, LANES),
                                     ((0, 0), (0, n_new - ts * N_KV_HEADS), (0, 0)))
        o_s = _attn_sample(pt_flat, jnp.full((1,), l, jnp.int32), scal_s, q_rows, new_rows(kb),
                           new_rows(v.astype(BF16)), bias_s, sg_row, cache_k_rows, cache_v_rows, n_pages)
        attn = jnp.transpose(o_s.reshape(bs, N_KV_HEADS, ts, GQA_GROUP, V_DIM), (2, 0, 1, 3, 4))
        attn = attn.reshape(t_s, N_HEADS * V_DIM).astype(BF16)
        buf = jnp.swapaxes(state_conv[l], 0, 1)
        rnn, h_last = _rglru_sample(xr, yr, buf, state_rnn[l], *rg, ts, past_len == 0)
        x1s, u2s, recs, cnt = _merge(xs, attn, rnn, ga, gr, mods_s[l], ts, cnt, *mg)
        to_bt = lambda z, n: jnp.swapaxes(z.reshape(ts, bs, n), 0, 1)
        outs[4].append(to_bt(k, K_COLS).reshape(bs, ts, N_KV_HEADS, 2 * HEAD_DIM))
        outs[5].append(to_bt(v, V_COLS).reshape(bs, ts, N_KV_HEADS, V_DIM))
        outs[6].append(h_last)
        xin = jnp.concatenate([state_conv[l], to_bt(xr, D_MODEL)], axis=1)
        outs[7].append(xin[:, -(CONV_W - 1):])

        pos_flat, te, valid, last_tile, runs, n_sorted = _dispatch_plan(jnp.concatenate([recp, recs], axis=0), cnt)
        u_sorted = _dispatch(pos_flat, u2p, u2s, jnp.zeros((n_sorted, D_MODEL), F32) if l == 0 else u_sorted)
        y_sorted = _experts(te, valid, last_tile, jnp.full((1,), l, jnp.int32), runs, u_sorted,
                            w_e_gate, w_e_up, w_e_down)
        xp = _combine(pos_flat, x1p, recp, y_sorted, mods_p[l], tp, 0, t_all)
        xs = _combine(pos_flat, x1s, recs, y_sorted, mods_s[l], ts, t_p, t_all)

    y_sample = jnp.swapaxes(xs.reshape(ts, bs, D_MODEL), 0, 1)
    stack = lambda i: jnp.stack(outs[i])
    return (xp.reshape(bp, tp, D_MODEL), y_sample, stack(0), stack(1), stack(2), stack(3),
            stack(4), stack(5), stack(6), stack(7))
```
